```python
import math
import jax
import jax.numpy as jnp
from jax import lax
import numpy as np

D_MODEL = 1024
BATCH = 8
SEQ = 4096
DEPTH = 4

N_EVEN = (DEPTH + 1) // 2
N_ODD = DEPTH // 2
HEAD_DIM = 64
NORM_EPS = 1e-6
BLOCK = 128

RW_WIDTH = D_MODEL
RW_HEADS = RW_WIDTH // HEAD_DIM
RW_DECAY_LORA = 64
RW_ICLR_LORA = 64
RW_PROJ = 3 * RW_WIDTH + RW_DECAY_LORA + RW_ICLR_LORA
RW_GN_EPS = 64e-5

S5_WIDTH = D_MODEL
S5_GROUP = 16
S5_GROUPS = S5_WIDTH // S5_GROUP
S5_STATE = 64

M2_WIDTH = D_MODEL
M2_HEADS = M2_WIDTH // HEAD_DIM
M2_GROUPS = 2
M2_STATE = 128
M2_CONV = 4
M2_CONV_DIM = M2_WIDTH + 2 * M2_GROUPS * M2_STATE

MLA_HEADS = 8
MLA_NOPE = 128
MLA_ROPE = 64
MLA_V = 128
MLA_Q_RANK = 384
MLA_KV_RANK = 256
MLA_WIDTH = MLA_HEADS * MLA_V
ROPE_THETA = 10000.0

N_MEM = 256
MEM_HEADS = 4
MEM_WIDTH = MEM_HEADS * HEAD_DIM

EVEN_WIDTH = RW_WIDTH + S5_WIDTH + MEM_WIDTH
ODD_WIDTH = M2_WIDTH + MLA_WIDTH + MEM_WIDTH
EVEN_SIZES = (RW_PROJ, S5_WIDTH, MEM_WIDTH, EVEN_WIDTH)
ODD_SIZES = (M2_CONV_DIM, M2_HEADS, MLA_Q_RANK, MLA_KV_RANK, MLA_ROPE, MEM_WIDTH, ODD_WIDTH)
EVEN_PROJ = sum(EVEN_SIZES)
ODD_PROJ = sum(ODD_SIZES)

kernel_name = 'hybrid_rwkv7_s5_mamba2_mla_memory'


def split_last(x, sizes):
    out, start = [], 0
    for n in sizes:
        out.append(x[..., start:start + n])
        start += n
    return out


def rms_norm(x, w, eps=NORM_EPS):
    xf = x.astype(jnp.float32)
    ms = jnp.mean(xf * xf, axis=-1, keepdims=True)
    return (xf * lax.rsqrt(ms + eps)).astype(x.dtype) * w


def token_shift(x):
    return jnp.pad(x, ((0, 0), (1, 0), (0, 0)))[:, :-1]


def rope_tables(positions):
    inv = 1.0 / (ROPE_THETA ** (jnp.arange(0, MLA_ROPE, 2, dtype=jnp.float32) / MLA_ROPE))
    ang = positions.astype(jnp.float32)[..., None] * inv
    return jnp.cos(ang), jnp.sin(ang)


def apply_rope(x, cos, sin):
    half = x.shape[-1] // 2
    x1, x2 = x[..., :half], x[..., half:]
    return jnp.concatenate([x1 * cos - x2 * sin, x1 * sin + x2 * cos], -1).astype(x.dtype)


def segsum(a):
    t = a.shape[-1]
    a_rep = jnp.broadcast_to(a[..., :, None], a.shape + (t,))
    strict = jnp.tril(jnp.ones((t, t), dtype=bool), -1)
    cs = jnp.cumsum(jnp.where(strict, a_rep, 0), axis=-2)
    return jnp.where(jnp.tril(jnp.ones((t, t), dtype=bool)), cs, -jnp.inf)


def causal_depthwise_conv(x, w, b):
    k, c = w.shape
    y = lax.conv_general_dilated(x, w[:, None, :], window_strides=(1,), padding=[(k - 1, 0)],
                                 dimension_numbers=('NWC', 'WIO', 'NWC'), feature_group_count=c)
    return y + b


def rwkv7_time_mix(p, mu, w0, w2, a0, a2, k_k, k_a, r_k, ln_w, ln_b):
    b, s, _ = p.shape
    h, n = RW_HEADS, HEAD_DIM
    p = p + (token_shift(p) - p) * mu
    r, k, v, w_lat, a_lat = split_last(p, (RW_WIDTH, RW_WIDTH, RW_WIDTH, RW_DECAY_LORA, RW_ICLR_LORA))
    log_w = -jax.nn.softplus(-(w0 + jnp.tanh(w_lat) @ w2)) - 0.5
    decay = jnp.exp(-jnp.exp(log_w))
    iclr = jax.nn.sigmoid(a0 + a_lat @ a2)
    kk = (k * k_k).reshape(b, s, h, n).astype(jnp.float32)
    kk = (kk * lax.rsqrt(jnp.maximum(jnp.sum(kk * kk, -1, keepdims=True), 1e-24))).astype(p.dtype)
    k = k * (1.0 + (iclr - 1.0) * k_a)
    heads = lambda t: t.reshape(b, s, h, n)
    r, decay, k, v, iclr = map(heads, (r, decay, k, v, iclr))
    removal = kk * iclr

    def step(state, inp):
        r_t, w_t, k_t, v_t, kk_t, b_t = inp
        sa = -jnp.einsum('bhij,bhj->bhi', state, kk_t)
        state = (state * w_t[:, :, None, :] + sa[..., None] * b_t[:, :, None, :]
                 + v_t[..., None] * k_t[:, :, None, :])
        return state, jnp.einsum('bhij,bhj->bhi', state, r_t)

    seq_major = lambda t: jnp.swapaxes(t, 0, 1)
    state0 = jnp.zeros((b, h, n, n), p.dtype)
    _, y = lax.scan(step, state0, tuple(map(seq_major, (r, decay, k, v, kk, removal))))
    y = seq_major(y).astype(jnp.float32)
    mean = jnp.mean(y, -1, keepdims=True)
    var = jnp.mean(jnp.square(y - mean), -1, keepdims=True)
    y = ((y - mean) * lax.rsqrt(var + RW_GN_EPS)).astype(p.dtype).reshape(b, s, h * n) * ln_w + ln_b
    bonus = jnp.sum(r * k * r_k, -1, keepdims=True) * v
    return y + bonus.reshape(b, s, h * n)


def s5_ssm(u, lam_re, lam_im, b_re, b_im, c_re, c_im, d_skip, log_dt, glu_w, glu_b):
    bsz, s, _ = u.shape
    g, n, pch, l = S5_GROUPS, S5_STATE, S5_GROUP, BLOCK
    nc = s // l
    dt = jnp.exp(log_dt)[:, None]
    mag = jnp.exp(lam_re * dt)
    ab_re, ab_im = mag * jnp.cos(lam_im * dt), mag * jnp.sin(lam_im * dt)
    den = lam_re * lam_re + lam_im * lam_im
    nr, ni = ab_re - 1.0, ab_im
    f_re = (nr * lam_re + ni * lam_im) / den
    f_im = (ni * lam_re - nr * lam_im) / den
    bb_re = f_re[..., None] * b_re - f_im[..., None] * b_im
    bb_im = f_re[..., None] * b_im + f_im[..., None] * b_re
    steps = jnp.arange(1, l + 1, dtype=lam_re.dtype)[:, None, None]
    pmag = jnp.exp(steps * lam_re * dt)
    pw_re, pw_im = pmag * jnp.cos(steps * lam_im * dt), pmag * jnp.sin(steps * lam_im * dt)
    a_re = jnp.broadcast_to(ab_re, (bsz, l, g, n))
    a_im = jnp.broadcast_to(ab_im, (bsz, l, g, n))

    def combine(e1, e2):
        a1r, a1i, b1r, b1i = e1
        a2r, a2i, b2r, b2i = e2
        return (a2r * a1r - a2i * a1i, a2r * a1i + a2i * a1r,
                a2r * b1r - a2i * b1i + b2r, a2r * b1i + a2i * b1r + b2i)

    def chunk_step(carry, u_c):
        h_re, h_im = carry
        bu_re = jnp.einsum('blgp,gnp->blgn', u_c, bb_re)
        bu_im = jnp.einsum('blgp,gnp->blgn', u_c, bb_im)
        _, _, s_re, s_im = lax.associative_scan(combine, (a_re, a_im, bu_re, bu_im), axis=1)
        s_re = s_re + pw_re * h_re[:, None] - pw_im * h_im[:, None]
        s_im = s_im + pw_re * h_im[:, None] + pw_im * h_re[:, None]
        y = jnp.einsum('blgn,gpn->blgp', s_re, c_re) - jnp.einsum('blgn,gpn->blgp', s_im, c_im)
        return (s_re[:, -1], s_im[:, -1]), y

    uc = u.reshape(bsz, nc, l, g, pch).transpose(1, 0, 2, 3, 4)
    h0 = jnp.zeros((bsz, g, n), u.dtype)
    _, y = lax.scan(chunk_step, (h0, h0), uc)
    y = y.transpose(1, 0, 2, 3, 4).reshape(bsz, s, g * pch) + d_skip * u
    y = jax.nn.gelu(y)
    return y * jax.nn.sigmoid(y @ glu_w + glu_b)


def mamba2_ssd(xbc, dt_raw, z, conv_w, conv_b, dt_bias, a_log, d_skip, norm_w):
    bsz, s, _ = xbc.shape
    h, p, g, n, l = M2_HEADS, HEAD_DIM, M2_GROUPS, M2_STATE, BLOCK
    j, nc = h // g, s // l
    xbc = jax.nn.silu(causal_depthwise_conv(xbc, conv_w, conv_b))
    xs, bm, cm = split_last(xbc, (M2_WIDTH, g * n, g * n))
    dt = jax.nn.softplus(dt_raw + dt_bias)
    a_dt = (dt * -jnp.exp(a_log)).reshape(bsz, nc, l, g, j).transpose(0, 3, 4, 1, 2)
    xh = xs.reshape(bsz, s, h, p)
    xdt = (xh * dt[..., None]).reshape(bsz, nc, l, g, j, p)
    bc = bm.reshape(bsz, nc, l, g, n)
    cc = cm.reshape(bsz, nc, l, g, n)
    a_cum = jnp.cumsum(a_dt, axis=-1)
    lmat = jnp.exp(segsum(a_dt))
    cb = jnp.einsum('bclgn,bcsgn->bgcls', cc, bc)
    y_diag = jnp.einsum('bgcls,bgjcls,bcsgjp->bclgjp', cb, lmat, xdt)
    decay_states = jnp.exp(a_cum[..., -1:] - a_cum)
    states = jnp.einsum('bcsgn,bgjcs,bcsgjp->bcgjpn', bc, decay_states, xdt)
    states = jnp.concatenate([jnp.zeros_like(states[:, :1]), states], axis=1)
    chunk_tot = jnp.pad(a_cum[..., -1], ((0, 0), (0, 0), (0, 0), (1, 0)))
    decay_chunk = jnp.exp(segsum(chunk_tot))
    states = jnp.einsum('bgjzc,bcgjpn->bzgjpn', decay_chunk, states)[:, :-1]
    y_off = jnp.einsum('bclgn,bcgjpn,bgjcl->bclgjp', cc, states, jnp.exp(a_cum))
    y = (y_diag + y_off).reshape(bsz, s, h, p) + xh * d_skip[:, None]
    y = y.reshape(bsz, s, h * p) * jax.nn.silu(z)
    y = rms_norm(y.reshape(bsz, s, g, h * p // g), norm_w.reshape(g, -1))
    return y.reshape(bsz, s, h * p)


def mla_attend(cq, ckv, k_rope, cos, sin, q_norm_w, wq_up, kv_norm_w, wkv_up):
    bsz, s, _ = cq.shape
    h = MLA_HEADS
    q = (rms_norm(cq, q_norm_w) @ wq_up).reshape(bsz, s, h, MLA_NOPE + MLA_ROPE)
    q_nope = q[..., :MLA_NOPE]
    q_pe = apply_rope(q[..., MLA_NOPE:], cos[:, :, None], sin[:, :, None])
    kv = (rms_norm(ckv, kv_norm_w) @ wkv_up).reshape(bsz, s, h, MLA_NOPE + MLA_V)
    k_nope, v = kv[..., :MLA_NOPE], kv[..., MLA_NOPE:]
    k_pe = apply_rope(k_rope, cos, sin)
    scale = (MLA_NOPE + MLA_ROPE) ** -0.5
    outs = []
    for start in range(0, s, BLOCK):
        end = start + BLOCK
        sc = (jnp.einsum('bqhd,bkhd->bhqk', q_nope[:, start:end], k_nope[:, :end])
              + jnp.einsum('bqhr,bkr->bhqk', q_pe[:, start:end], k_pe[:, :end]))
        sc = sc.astype(jnp.float32) * scale
        causal = (start + jnp.arange(BLOCK))[:, None] >= jnp.arange(end)[None, :]
        pr = jax.nn.softmax(jnp.where(causal, sc, -jnp.inf), axis=-1).astype(v.dtype)
        outs.append(jnp.einsum('bhqk,bkhd->bqhd', pr, v[:, :end]))
    return jnp.concatenate(outs, axis=1).reshape(bsz, s, h * MLA_V)


def memory_attend(q, mem_n, mem_kv_w):
    bsz, s, _ = q.shape
    kv = mem_n @ mem_kv_w
    k = kv[..., :MEM_WIDTH].reshape(bsz, -1, MEM_HEADS, HEAD_DIM)
    v = kv[..., MEM_WIDTH:].reshape(bsz, -1, MEM_HEADS, HEAD_DIM)
    q = q.reshape(bsz, s, MEM_HEADS, HEAD_DIM)
    sc = jnp.einsum('bqhd,bmhd->bhqm', q, k).astype(jnp.float32) * HEAD_DIM ** -0.5
    pr = jax.nn.softmax(sc, axis=-1).astype(v.dtype)
    return jnp.einsum('bhqm,bmhd->bqhd', pr, v).reshape(bsz, s, MEM_WIDTH)


def even_mixer(xn, mem_n, in_w, out_w, mem_kv_w, rw_mu, rw_w0, rw_w2, rw_a0, rw_a2, rw_k_k, rw_k_a,
               rw_r_k, rw_ln_w, rw_ln_b, lam_re, lam_im, b_re, b_im, c_re, c_im, s5_d, log_dt,
               glu_w, glu_b):
    proj = xn @ in_w
    rw_in, u, q_mem, z = split_last(proj, EVEN_SIZES)
    a_out = rwkv7_time_mix(rw_in, rw_mu, rw_w0, rw_w2, rw_a0, rw_a2, rw_k_k, rw_k_a, rw_r_k,
                           rw_ln_w, rw_ln_b)
    b_out = s5_ssm(u, lam_re, lam_im, b_re, b_im, c_re, c_im, s5_d, log_dt, glu_w, glu_b)
    m_out = memory_attend(q_mem, mem_n, mem_kv_w)
    y = jnp.concatenate([a_out, b_out, m_out], axis=-1) * jax.nn.silu(z)
    return y @ out_w


def odd_mixer(xn, mem_n, cos, sin, in_w, out_w, mem_kv_w, conv_w, conv_b, dt_bias, a_log, m2_d,
              m2_norm_w, q_norm_w, wq_up, kv_norm_w, wkv_up):
    proj = xn @ in_w
    xbc, dt_raw, cq, ckv, k_rope, q_mem, z = split_last(proj, ODD_SIZES)
    z_c, z_d, z_m = split_last(z, (M2_WIDTH, MLA_WIDTH, MEM_WIDTH))
    c_out = mamba2_ssd(xbc, dt_raw, z_c, conv_w, conv_b, dt_bias, a_log, m2_d, m2_norm_w)
    d_out = mla_attend(cq, ckv, k_rope, cos, sin, q_norm_w, wq_up, kv_norm_w, wkv_up) * jax.nn.silu(z_d)
    m_out = memory_attend(q_mem, mem_n, mem_kv_w) * jax.nn.silu(z_m)
    return jnp.concatenate([c_out, d_out, m_out], axis=-1) @ out_w


def setup_inputs(seed: int = 0) -> dict:
    key = jax.random.key(seed)
    ks = iter(jax.random.split(key, 64))
    f32 = jnp.float32
    nrm = lambda shape, scale: scale * jax.random.normal(next(ks), shape, f32)
    gain = lambda shape: 1.0 + nrm(shape, 0.02)
    unif = lambda shape, lo, hi: jax.random.uniform(next(ks), shape, f32, lo, hi)
    e, o = N_EVEN, N_ODD
    m2_dt = jnp.exp(unif((o, M2_HEADS), math.log(1e-3), math.log(1e-1)))
    return {
        'x': nrm((BATCH, SEQ, D_MODEL), 1.0),
        'mem': nrm((BATCH, N_MEM, D_MODEL), 1.0),
        'positions': jnp.tile(jnp.arange(SEQ, dtype=jnp.int32)[None, :], (BATCH, 1)),
        'norm_w': gain((DEPTH, D_MODEL)),
        'mem_norm_w': gain((D_MODEL,)),
        'final_norm_w': gain((D_MODEL,)),
        'mem_kv_w': nrm((DEPTH, D_MODEL, 2 * MEM_WIDTH), D_MODEL ** -0.5),
        'ev_in_w': nrm((e, D_MODEL, EVEN_PROJ), D_MODEL ** -0.5),
        'ev_out_w': nrm((e, EVEN_WIDTH, D_MODEL), EVEN_WIDTH ** -0.5),
        'rw_mu': unif((e, RW_PROJ), 0.0, 1.0),
        'rw_w0': unif((e, RW_WIDTH), -6.0, 1.0),
        'rw_w2': nrm((e, RW_DECAY_LORA, RW_WIDTH), 0.5 * RW_DECAY_LORA ** -0.5),
        'rw_a0': nrm((e, RW_WIDTH), 0.1),
        'rw_a2': nrm((e, RW_ICLR_LORA, RW_WIDTH), 0.5 * RW_ICLR_LORA ** -0.5),
        'rw_k_k': 0.85 + nrm((e, RW_WIDTH), 0.02),
        'rw_k_a': gain((e, RW_WIDTH)),
        'rw_r_k': nrm((e, RW_HEADS, HEAD_DIM), 0.1),
        'rw_ln_w': gain((e, RW_WIDTH)),
        'rw_ln_b': nrm((e, RW_WIDTH), 0.02),
        's5_lambda_re': -0.5 + nrm((e, S5_GROUPS, S5_STATE), 0.01),
        's5_lambda_im': math.pi * jnp.arange(S5_STATE, dtype=f32) + nrm((e, S5_GROUPS, S5_STATE), 0.01),
        's5_b_re': nrm((e, S5_GROUPS, S5_STATE, S5_GROUP), (2 * S5_GROUP) ** -0.5),
        's5_b_im': nrm((e, S5_GROUPS, S5_STATE, S5_GROUP), (2 * S5_GROUP) ** -0.5),
        's5_c_re': nrm((e, S5_GROUPS, S5_GROUP, S5_STATE), (2 * S5_STATE) ** -0.5),
        's5_c_im': nrm((e, S5_GROUPS, S5_GROUP, S5_STATE), (2 * S5_STATE) ** -0.5),
        's5_d': nrm((e, S5_WIDTH), 0.5),
        's5_log_dt': unif((e, S5_GROUPS), math.log(1e-3), math.log(1e-1)),
        's5_glu_w': nrm((e, S5_WIDTH, S5_WIDTH), S5_WIDTH ** -0.5),
        's5_glu_b': nrm((e, S5_WIDTH), 0.02),
        'od_in_w': nrm((o, D_MODEL, ODD_PROJ), D_MODEL ** -0.5),
        'od_out_w': nrm((o, ODD_WIDTH, D_MODEL), ODD_WIDTH ** -0.5),
        'm2_conv_w': nrm((o, M2_CONV, M2_CONV_DIM), M2_CONV ** -0.5),
        'm2_conv_b': nrm((o, M2_CONV_DIM), 0.02),
        'm2_dt_bias': m2_dt + jnp.log(-jnp.expm1(-m2_dt)),
        'm2_a_log': jnp.log(unif((o, M2_HEADS), 1.0, 16.0)),
        'm2_d': gain((o, M2_HEADS)),
        'm2_norm_w': gain((o, M2_WIDTH)),
        'mla_q_norm_w': gain((o, MLA_Q_RANK)),
        'mla_wq_up': nrm((o, MLA_Q_RANK, MLA_HEADS * (MLA_NOPE + MLA_ROPE)), MLA_Q_RANK ** -0.5),
        'mla_kv_norm_w': gain((o, MLA_KV_RANK)),
        'mla_wkv_up': nrm((o, MLA_KV_RANK, MLA_HEADS * (MLA_NOPE + MLA_V)), MLA_KV_RANK ** -0.5),
    }


def reference(x, mem, positions, norm_w, mem_norm_w, final_norm_w, mem_kv_w, ev_in_w, ev_out_w,
              rw_mu, rw_w0, rw_w2, rw_a0, rw_a2, rw_k_k, rw_k_a, rw_r_k, rw_ln_w, rw_ln_b,
              s5_lambda_re, s5_lambda_im, s5_b_re, s5_b_im, s5_c_re, s5_c_im, s5_d, s5_log_dt,
              s5_glu_w, s5_glu_b, od_in_w, od_out_w, m2_conv_w, m2_conv_b, m2_dt_bias, m2_a_log,
              m2_d, m2_norm_w, mla_q_norm_w, mla_wq_up, mla_kv_norm_w, mla_wkv_up):
    mem_n = rms_norm(mem, mem_norm_w)
    cos, sin = rope_tables(positions)
    h = x
    for layer in range(DEPTH):
        i = layer // 2
        xn = rms_norm(h, norm_w[layer])
        if layer % 2 == 0:
            h = h + even_mixer(xn, mem_n, ev_in_w[i], ev_out_w[i], mem_kv_w[layer], rw_mu[i], rw_w0[i],
                               rw_w2[i], rw_a0[i], rw_a2[i], rw_k_k[i], rw_k_a[i], rw_r_k[i],
                               rw_ln_w[i], rw_ln_b[i], s5_lambda_re[i], s5_lambda_im[i], s5_b_re[i],
                               s5_b_im[i], s5_c_re[i], s5_c_im[i], s5_d[i], s5_log_dt[i],
                               s5_glu_w[i], s5_glu_b[i])
        else:
            h = h + odd_mixer(xn, mem_n, cos, sin, od_in_w[i], od_out_w[i], mem_kv_w[layer],
                              m2_conv_w[i], m2_conv_b[i], m2_dt_bias[i], m2_a_log[i], m2_d[i],
                              m2_norm_w[i], mla_q_norm_w[i], mla_wq_up[i], mla_kv_norm_w[i],
                              mla_wkv_up[i])
    return rms_norm(h, final_norm_w)
```

```python
import functools
import math

import jax
import jax.numpy as jnp
from jax import lax
from jax.experimental import pallas as pl
from jax.experimental.pallas import tpu as pltpu

F32 = jnp.float32
BF16 = jnp.bfloat16
HIGHEST = lax.Precision.HIGHEST

LANES = 128
SUBLANES = 8
VMEM_LIMIT_BYTES = 56 * 1024 * 1024

D_MODEL = 1024
HEAD_DIM = 64
NORM_EPS = 1e-6
RW_WIDTH = 1024
RW_LORA = 64
RW_GN_EPS = 64e-5
RW_CHUNK = 64
S5_GROUP = 16
S5_STATE = 64
S5_SUB = 16
M2_HEADS = 16
M2_GROUPS = 2
M2_STATE = 128
M2_CONV = 4
M2_CHUNK = 128
MLA_HEADS = 8
MLA_NOPE = 128
MLA_ROPE = 64
MLA_V = 128
MLA_Q_RANK = 384
MLA_KV_RANK = 256
ROPE_THETA = 10000.0
MEM_HEADS = 4
MEM_WIDTH = MEM_HEADS * HEAD_DIM


def _dot(a, b, precision=None):
    return jnp.dot(a, b, preferred_element_type=F32, precision=precision)


def _dot_nt(a, b, precision=None):
    return lax.dot_general(a, b, (((1,), (1,)), ((), ())), preferred_element_type=F32,
                           precision=precision)


def _sigmoid(x):
    return 1.0 / (1.0 + jnp.exp(-x))


def _silu(x):
    return x * _sigmoid(x)


def _softplus(x):
    return jnp.maximum(x, 0.0) + jnp.log(1.0 + jnp.exp(-jnp.abs(x)))


def _gelu_tanh(x):
    return 0.5 * x * (1.0 + jnp.tanh(math.sqrt(2.0 / math.pi) * (x + 0.044715 * (x * x * x))))


def _rms(x, w, eps=NORM_EPS):
    ms = jnp.mean(x * x, axis=-1, keepdims=True)
    return x * lax.rsqrt(ms + eps) * w


def _params(*sem):
    return pltpu.CompilerParams(dimension_semantics=sem, vmem_limit_bytes=VMEM_LIMIT_BYTES)


def _norm_proj_kernel(x_ref, nw_ref, w_ref, *out_refs, splits):
    xn = _rms(x_ref[...], nw_ref[...]).astype(BF16)
    off = 0
    for o_ref, n in zip(out_refs, splits):
        o_ref[...] = _dot(xn, w_ref[:, off:off + n]).astype(o_ref.dtype)
        off += n


def _norm_proj(x, nw, w, splits, dtypes, tm=256):
    t, d = x.shape
    n = w.shape[1]
    assert sum(splits) == n and t % tm == 0
    return pl.pallas_call(
        functools.partial(_norm_proj_kernel, splits=splits),
        grid=(t // tm,),
        in_specs=[pl.BlockSpec((tm, d), lambda i: (i, 0)),
                  pl.BlockSpec((1, d), lambda i: (0, 0)),
                  pl.BlockSpec((d, n), lambda i: (0, 0))],
        out_specs=[pl.BlockSpec((tm, s), lambda i: (i, 0)) for s in splits],
        out_shape=[jax.ShapeDtypeStruct((t, s), dt) for s, dt in zip(splits, dtypes)],
        compiler_params=_params("arbitrary"),
        name="norm_proj",
    )(x, nw.reshape(1, d), w)


def _mem_kv_kernel(mem_ref, nw_ref, w_ref, k_ref, v_ref):
    mn = _rms(mem_ref[0], nw_ref[...]).astype(BF16)
    kv = _dot(mn, w_ref[0])
    k_ref[0, 0] = kv[:, :MEM_WIDTH].astype(k_ref.dtype)
    v_ref[0, 0] = kv[:, MEM_WIDTH:].astype(v_ref.dtype)


def _mem_kv(mem, nw, w):
    b, m, d = mem.shape
    depth = w.shape[0]
    shp = jax.ShapeDtypeStruct((depth, b, m, MEM_WIDTH), BF16)
    return pl.pallas_call(
        _mem_kv_kernel,
        grid=(depth, b),
        in_specs=[pl.BlockSpec((1, m, d), lambda l, i: (i, 0, 0)),
                  pl.BlockSpec((1, d), lambda l, i: (0, 0)),
                  pl.BlockSpec((1, d, 2 * MEM_WIDTH), lambda l, i: (l, 0, 0))],
        out_specs=[pl.BlockSpec((1, 1, m, MEM_WIDTH), lambda l, i: (l, i, 0, 0))] * 2,
        out_shape=[shp, shp],
        compiler_params=_params("arbitrary", "arbitrary"),
        name="mem_kv",
    )(mem, nw.reshape(1, d), w)


def _mem_attend(q, k, v):
    lane = lax.broadcasted_iota(jnp.int32, (1, MEM_WIDTH), 1)
    qb = (q * (HEAD_DIM ** -0.5)).astype(BF16)
    out = jnp.zeros(q.shape, F32)
    for h in range(MEM_HEADS):
        hm = (lane >> 6) == h
        sc = _dot_nt(qb, jnp.where(hm, k, jnp.zeros_like(k)))
        sc = sc - jnp.max(sc, axis=-1, keepdims=True)
        p = jnp.exp(sc)
        p = p / jnp.sum(p, axis=-1, keepdims=True)
        out = out + _dot(p.astype(BF16), jnp.where(hm, v, jnp.zeros_like(v)))
    return out


def _rwkv_kernel(pr_ref, pk_ref, pv_ref, plat_ref, pp_ref, wa_ref, out_ref, st_ref, prev_ref, *, L):
    c = pl.program_id(2)

    @pl.when(c == 0)
    def _():
        st_ref[...] = jnp.zeros_like(st_ref)
        prev_ref[...] = jnp.zeros_like(prev_ref)

    pp = pp_ref[0]
    prm = lambda i: pp[i:i + 1, :]
    mu_r, mu_k, mu_v, mu_l, w0, a0, k_k, k_a, r_k, ln_w, ln_b = [prm(i) for i in range(11)]
    lane = lax.broadcasted_iota(jnp.int32, (1, LANES), 1)
    row_l = lax.broadcasted_iota(jnp.int32, (L, LANES), 0)

    def shift_mix(x, slot, mu):
        xs = pltpu.roll(x, 1, axis=0)
        xs = jnp.where(row_l == 0, prev_ref[slot:slot + 1, :], xs)
        prev_ref[slot:slot + 1, :] = x[L - 1:L, :]
        return x + (xs - x) * mu

    r = shift_mix(pr_ref[...], 0, mu_r)
    k = shift_mix(pk_ref[...], 1, mu_k)
    v = shift_mix(pv_ref[...], 2, mu_v)
    lat = shift_mix(plat_ref[...], 3, mu_l)

    lat_t = jnp.where(lane < RW_LORA, jnp.tanh(lat), lat)
    la = _dot(lat_t, wa_ref[0], precision=HIGHEST)
    log_w = -_softplus(-(w0 + la[:, :LANES])) - 0.5
    lw = -jnp.exp(log_w)
    iclr = _sigmoid(a0 + la[:, LANES:])

    i2 = lax.broadcasted_iota(jnp.int32, (LANES, LANES), 0)
    j2 = lax.broadcasted_iota(jnp.int32, (LANES, LANES), 1)
    same_head = (i2 >> 6) == (j2 >> 6)
    head_ones = same_head.astype(F32)

    kk = k * k_k
    ss = _dot(kk * kk, head_ones, precision=HIGHEST)
    kk = kk * lax.rsqrt(jnp.maximum(ss, 1e-24))
    kp = k * (1.0 + (iclr - 1.0) * k_a)
    bvec = kk * iclr
    bonus = _dot(r * kp * r_k, head_ones, precision=HIGHEST) * v

    il = lax.broadcasted_iota(jnp.int32, (L, L), 0)
    jl = lax.broadcasted_iota(jnp.int32, (L, L), 1)
    cum = _dot((jl <= il).astype(F32), lw, precision=HIGHEST)
    cum_l = cum[L - 1:L, :]
    e_neg = jnp.exp(-cum)
    at = (-kk) * jnp.exp(cum - lw)
    rt = r * jnp.exp(cum)
    bt = bvec * e_neg
    kt = kp * e_neg
    dend = jnp.exp(cum_l - cum)
    bd = bvec * dend
    kd = kp * dend

    m0 = lane < HEAD_DIM
    stack = lambda x: jnp.concatenate([jnp.where(m0, x, 0.0), jnp.where(m0, 0.0, x)], axis=0)
    a2, r2, b2, k2 = [stack(x).astype(BF16) for x in (at, rt, bt, kt)]
    v2 = stack(v).astype(BF16)
    bd2, kd2 = stack(bd).astype(BF16), stack(kd).astype(BF16)

    strict = same_head & (j2 < i2)
    incl = same_head & (j2 <= i2)
    aab = jnp.where(strict, _dot_nt(a2, b2), 0.0)
    aak = jnp.where(strict, _dot_nt(a2, k2), 0.0)
    rb = jnp.where(incl, _dot_nt(r2, b2), 0.0)
    rk = jnp.where(incl, _dot_nt(r2, k2), 0.0)

    inv = (i2 == j2).astype(F32)
    for s in (1, 2, 4, 8, 16, 32):
        lower_left = ((i2 & -(2 * s)) == (j2 & -(2 * s))) & ((i2 & s) != 0) & ((j2 & s) == 0)
        cblk = jnp.where(lower_left, -aab, 0.0)
        inv = inv - _dot(_dot(inv, cblk, precision=HIGHEST), inv, precision=HIGHEST)

    st = st_ref[...]
    st_b = st.astype(BF16)
    rhs = _dot_nt(a2, st_b) + _dot(aak.astype(BF16), v2)
    u2 = _dot(inv, rhs, precision=HIGHEST)
    u2b = u2.astype(BF16)
    y2 = _dot_nt(r2, st_b) + _dot(rb.astype(BF16), u2b) + _dot(rk.astype(BF16), v2)
    y = y2[:L, :] + y2[L:, :]

    uv_t = jnp.concatenate([u2.T, stack(v).T], axis=1).astype(BF16)
    st_ref[...] = st * jnp.exp(cum_l) + _dot(uv_t, jnp.concatenate([bd2, kd2], axis=0))

    mean = _dot(y, head_ones, precision=HIGHEST) * (1.0 / HEAD_DIM)
    dcen = y - mean
    var = _dot(dcen * dcen, head_ones, precision=HIGHEST) * (1.0 / HEAD_DIM)
    out_ref[...] = dcen * lax.rsqrt(var + RW_GN_EPS) * ln_w + ln_b + bonus


def _rwkv(p, pp, wa, batch, seq):
    L = RW_CHUNK
    t = p.shape[0]
    nc = seq // L
    npair = RW_WIDTH // LANES
    spec = lambda off: pl.BlockSpec((L, LANES), lambda b, h, c, off=off: (b * nc + c, h + off))
    return pl.pallas_call(
        functools.partial(_rwkv_kernel, L=L),
        grid=(batch, npair, nc),
        in_specs=[spec(0), spec(npair), spec(2 * npair),
                  pl.BlockSpec((L, LANES), lambda b, h, c: (b * nc + c, 3 * npair)),
                  pl.BlockSpec((1, 16, LANES), lambda b, h, c: (h, 0, 0)),
                  pl.BlockSpec((1, LANES, 2 * LANES), lambda b, h, c: (h, 0, 0))],
        out_specs=pl.BlockSpec((L, LANES), lambda b, h, c: (b * nc + c, h)),
        out_shape=jax.ShapeDtypeStruct((t, RW_WIDTH), F32),
        scratch_shapes=[pltpu.VMEM((LANES, LANES), F32), pltpu.VMEM((SUBLANES, LANES), F32)],
        compiler_params=_params("arbitrary", "arbitrary", "arbitrary"),
        name="rwkv7",
    )(p, p, p, p, pp, wa)


def _rwkv_param_pack(mu, w0, w2, a0, a2, k_k, k_a, r_k, ln_w, ln_b):
    npair = RW_WIDTH // LANES
    sl = lambda x: x.reshape(npair, 1, LANES)
    mu_rkv = mu[:3 * RW_WIDTH].reshape(3, npair, 1, LANES)
    mu_lat = jnp.broadcast_to(mu[3 * RW_WIDTH:].reshape(1, 1, LANES), (npair, 1, LANES))
    rows = [mu_rkv[0], mu_rkv[1], mu_rkv[2], mu_lat, sl(w0), sl(a0), sl(k_k), sl(k_a),
            sl(r_k.reshape(-1)), sl(ln_w), sl(ln_b)]
    pp = jnp.concatenate(rows + [jnp.zeros((npair, 16 - len(rows), LANES), F32)], axis=1)
    w2p = w2.reshape(RW_LORA, npair, LANES).transpose(1, 0, 2)
    a2p = a2.reshape(RW_LORA, npair, LANES).transpose(1, 0, 2)
    z = jnp.zeros_like(w2p)
    wa = jnp.concatenate([jnp.concatenate([w2p, z], axis=2), jnp.concatenate([z, a2p], axis=2)], axis=1)
    return pp, wa


def _s5_kernel(u_ref, tz_ref, gm_ref, cc_ref, pw_ref, y_ref, *, nj):
    u = u_ref[0]
    y = _dot(u, tz_ref[0])
    x = _dot(u, gm_ref[0])
    rows = x.shape[0]
    jrow = lax.broadcasted_iota(jnp.int32, (rows, LANES), 0) & (nj - 1)
    pw = pw_ref[0]
    d, lvl = 1, 0
    while d < nj:
        xs = jnp.where(jrow >= d, pltpu.roll(x, d, axis=0), 0.0)
        xsw = pltpu.roll(xs, S5_STATE, axis=1)
        x = x + xs * pw[2 * lvl:2 * lvl + 1, :] + xsw * pw[2 * lvl + 1:2 * lvl + 2, :]
        d, lvl = 2 * d, lvl + 1
    h_in = jnp.where(jrow >= 1, pltpu.roll(x, 1, axis=0), 0.0)
    y_ref[0] = y + _dot(h_in.astype(BF16), cc_ref[0])


def _s5(u_g, tz, gm, cc, pw, nj):
    g, rows, width = u_g.shape
    return pl.pallas_call(
        functools.partial(_s5_kernel, nj=nj),
        grid=(g,),
        in_specs=[pl.BlockSpec((1, rows, width), lambda i: (i, 0, 0)),
                  pl.BlockSpec((1, width, width), lambda i: (i, 0, 0)),
                  pl.BlockSpec((1, width, 2 * S5_STATE), lambda i: (i, 0, 0)),
                  pl.BlockSpec((1, 2 * S5_STATE, width), lambda i: (i, 0, 0)),
                  pl.BlockSpec((1, 16, 2 * S5_STATE), lambda i: (i, 0, 0))],
        out_specs=pl.BlockSpec((1, rows, width), lambda i: (i, 0, 0)),
        out_shape=jax.ShapeDtypeStruct((g, rows, width), F32),
        compiler_params=_params("arbitrary"),
        name="s5",
    )(u_g, tz, gm, cc, pw)


def _s5_param_pack(lam_re, lam_im, b_re, b_im, c_re, c_im, log_dt, nj):
    ls = S5_SUB
    dt = jnp.exp(log_dt)[:, None]
    zr, zi = lam_re * dt, lam_im * dt

    def powers(steps):
        st = steps[:, None, None]
        mag = jnp.exp(st * zr)
        return mag * jnp.cos(st * zi), mag * jnp.sin(st * zi)

    ab_re, ab_im = powers(jnp.ones((1,), F32))
    ab_re, ab_im = ab_re[0], ab_im[0]
    den = lam_re * lam_re + lam_im * lam_im
    nr, ni = ab_re - 1.0, ab_im
    f_re = (nr * lam_re + ni * lam_im) / den
    f_im = (ni * lam_re - nr * lam_im) / den
    bb_re = f_re[..., None] * b_re - f_im[..., None] * b_im
    bb_im = f_re[..., None] * b_im + f_im[..., None] * b_re
    pr, pi = powers(jnp.arange(0, ls + 1, dtype=F32))
    cl_re = c_re[None] * pr[:, :, None, :] - c_im[None] * pi[:, :, None, :]
    cl_im = c_re[None] * pi[:, :, None, :] + c_im[None] * pr[:, :, None, :]
    kern = (jnp.einsum('dgpn,gnq->dgpq', cl_re[:ls], bb_re, precision=HIGHEST)
            - jnp.einsum('dgpn,gnq->dgpq', cl_im[:ls], bb_im, precision=HIGHEST))
    lag = jnp.arange(ls)[None, :] - jnp.arange(ls)[:, None]
    tz = jnp.where((lag >= 0)[:, :, None, None, None], kern[jnp.clip(lag, 0, ls - 1)], 0.0)
    g = lam_re.shape[0]
    pch = S5_GROUP
    tz = tz.transpose(2, 0, 4, 1, 3).reshape(g, ls * pch, ls * pch)
    rr, ri = pr[ls - 1::-1][:ls], pi[ls - 1::-1][:ls]
    gm_re = rr[..., None] * bb_re[None] - ri[..., None] * bb_im[None]
    gm_im = rr[..., None] * bb_im[None] + ri[..., None] * bb_re[None]
    gm = jnp.concatenate([gm_re, gm_im], axis=2)
    gm = gm.transpose(1, 0, 3, 2).reshape(g, ls * pch, 2 * S5_STATE)
    cc = jnp.concatenate([cl_re[1:], -cl_im[1:]], axis=3)
    cc = cc.transpose(1, 3, 0, 2).reshape(g, 2 * S5_STATE, ls * pch)
    lv = []
    d = 1
    while d < nj:
        lv.append(float(ls * d))
        d *= 2
    qr, qi = powers(jnp.asarray(lv, F32))
    pw = jnp.stack([jnp.concatenate([qr, qr], axis=-1), jnp.concatenate([-qi, qi], axis=-1)], axis=1)
    pw = pw.reshape(2 * len(lv), g, 2 * S5_STATE).transpose(1, 0, 2)
    pw = jnp.concatenate([pw, jnp.zeros((g, 16 - 2 * len(lv), 2 * S5_STATE), F32)], axis=1)
    return tz.astype(BF16), gm.astype(BF16), cc.astype(BF16), pw


def _ssd_kernel(xbc_ref, misc_ref, zc_ref, cw_ref, vec_ref, nw_ref, out_ref,
                st_ref, halo_ref, buf_ref, *, L):
    c = pl.program_id(1)
    width = M2_HEADS * HEAD_DIM
    gs = M2_STATE

    @pl.when(c == 0)
    def _():
        st_ref[...] = jnp.zeros_like(st_ref)
        halo_ref[...] = jnp.zeros_like(halo_ref)

    xbc = xbc_ref[...]
    buf_ref[0:SUBLANES, :] = halo_ref[...]
    buf_ref[SUBLANES:SUBLANES + L, :] = xbc
    halo_ref[...] = xbc[L - SUBLANES:L, :]
    cw = cw_ref[...]
    acc = cw[M2_CONV:M2_CONV + 1, :]
    for tap in range(M2_CONV):
        s0 = SUBLANES - (M2_CONV - 1) + tap
        acc = acc + cw[tap:tap + 1, :] * buf_ref[s0:s0 + L, :]
    xc = _silu(acc)
    xs = xc[:, :width]

    vec = vec_ref[...]
    dt = _softplus(misc_ref[:, 2 * LANES:3 * LANES] + vec[0:1, :])
    a_dt = dt * vec[1:2, :]
    il = lax.broadcasted_iota(jnp.int32, (L, L), 0)
    jl = lax.broadcasted_iota(jnp.int32, (L, L), 1)
    tril = jl <= il
    cum = _dot(tril.astype(F32), a_dt, precision=HIGHEST)
    cum_t = cum.T

    eh = lax.broadcasted_iota(jnp.int32, (LANES, width), 0)
    ec = lax.broadcasted_iota(jnp.int32, (LANES, width), 1)
    expand = ((ec >> 6) == eh).astype(F32)
    dt_x = _dot(dt, expand, precision=HIGHEST)
    cum_x = _dot(cum, expand, precision=HIGHEST)
    cum_lx = cum_x[L - 1:L, :]
    xdt = xs * dt_x
    xdd = xdt * jnp.exp(cum_lx - cum_x)
    ecum = jnp.exp(cum_x)

    rh = lax.broadcasted_iota(jnp.int32, (width, LANES), 0)
    rc = lax.broadcasted_iota(jnp.int32, (width, LANES), 1)
    expand_t = ((rh >> 6) == rc).astype(F32)
    tot_rows = _dot(expand_t, jnp.broadcast_to(cum_t[:, L - 1:L], (LANES, LANES)), precision=HIGHEST)

    lane = lax.broadcasted_iota(jnp.int32, (1, LANES), 1)
    m0 = lane < HEAD_DIM
    ys = []
    for pair in range(M2_HEADS // 2):
        g = pair // (M2_HEADS // 2 // M2_GROUPS)
        bm = xc[:, width + g * gs:width + (g + 1) * gs].astype(BF16)
        cm = xc[:, width + M2_GROUPS * gs + g * gs:width + M2_GROUPS * gs + (g + 1) * gs].astype(BF16)
        cb = _dot_nt(cm, bm)
        sl = slice(pair * LANES, (pair + 1) * LANES)
        xdt_p = xdt[:, sl]
        yp = jnp.zeros((L, LANES), F32)
        for hh in range(2):
            h = 2 * pair + hh
            diff = jnp.where(tril, cum[:, h:h + 1] - cum_t[h:h + 1, :], 0.0)
            lm = jnp.where(tril, jnp.exp(diff), 0.0)
            hm = m0 if hh == 0 else jnp.logical_not(m0)
            yp = yp + _dot((cb * lm).astype(BF16), jnp.where(hm, xdt_p, 0.0).astype(BF16))
        st_p = st_ref[sl, :]
        yp = yp + _dot_nt(cm, st_p.astype(BF16)) * ecum[:, sl]
        st_ref[sl, :] = st_p * jnp.exp(tot_rows[sl, :]) + _dot(xdd[:, sl].T.astype(BF16), bm)
        ys.append(yp)
    y = jnp.concatenate(ys, axis=1) + xs * nw_ref[1:2, :]
    y = y * _silu(zc_ref[...])
    gw = width // M2_GROUPS
    outs = []
    for g in range(M2_GROUPS):
        yg = y[:, g * gw:(g + 1) * gw]
        outs.append(_rms(yg, nw_ref[0:1, g * gw:(g + 1) * gw]))
    out_ref[...] = jnp.concatenate(outs, axis=1)


def _ssd(xbc, misc, z, cw, vec, nw, batch, seq):
    L = M2_CHUNK
    t, cd = xbc.shape
    nc = seq // L
    width = M2_HEADS * HEAD_DIM
    return pl.pallas_call(
        functools.partial(_ssd_kernel, L=L),
        grid=(batch, nc),
        in_specs=[pl.BlockSpec((L, cd), lambda b, c: (b * nc + c, 0)),
                  pl.BlockSpec((L, misc.shape[1]), lambda b, c: (b * nc + c, 0)),
                  pl.BlockSpec((L, width), lambda b, c: (b * nc + c, 0)),
                  pl.BlockSpec((SUBLANES, cd), lambda b, c: (0, 0)),
                  pl.BlockSpec((SUBLANES, LANES), lambda b, c: (0, 0)),
                  pl.BlockSpec((SUBLANES, width), lambda b, c: (0, 0))],
        out_specs=pl.BlockSpec((L, width), lambda b, c: (b * nc + c, 0)),
        out_shape=jax.ShapeDtypeStruct((t, width), F32),
        scratch_shapes=[pltpu.VMEM((width, M2_STATE), F32), pltpu.VMEM((SUBLANES, cd), F32),
                        pltpu.VMEM((L + SUBLANES, cd), F32)],
        compiler_params=_params("arbitrary", "arbitrary"),
        name="ssd",
    )(xbc, misc, z, cw, vec, nw)


def _mla_prep_kernel(cq_ref, ckv_ref, misc_ref, cos_ref, sin_ref, qnw_ref, kvnw_ref, wq_ref, wkv_ref,
                     q_ref, k_ref, v_ref):
    nh = MLA_HEADS
    cos, sin = cos_ref[...], sin_ref[...]
    q = _dot(_rms(cq_ref[...], qnw_ref[...]).astype(BF16), wq_ref[...])
    scale = (MLA_NOPE + MLA_ROPE) ** -0.5
    rope0 = nh * MLA_NOPE
    rot0 = rope0 + nh * MLA_ROPE
    for pair in range(nh // 2):
        sl = slice(pair * LANES, (pair + 1) * LANES)
        qr = (q[:, rope0:rot0][:, sl] * cos + q[:, rot0:][:, sl] * sin) * scale
        for hh in range(2):
            h = 2 * pair + hh
            q_ref[:, h * 2 * LANES:h * 2 * LANES + LANES] = (
                q[:, h * MLA_NOPE:(h + 1) * MLA_NOPE] * scale).astype(q_ref.dtype)
            q_ref[:, h * 2 * LANES + LANES:(h + 1) * 2 * LANES] = qr.astype(q_ref.dtype)
    kv = _dot(_rms(ckv_ref[...], kvnw_ref[...]).astype(BF16), wkv_ref[...])
    kpe = misc_ref[:, :LANES] * cos + misc_ref[:, LANES:2 * LANES] * sin
    lane = lax.broadcasted_iota(jnp.int32, (1, LANES), 1)
    slots = [jnp.where(lane < MLA_ROPE, kpe, 0.0), jnp.where(lane < MLA_ROPE, 0.0, kpe)]
    for h in range(nh):
        k_ref[:, h * 2 * LANES:h * 2 * LANES + LANES] = kv[:, h * MLA_NOPE:(h + 1) * MLA_NOPE].astype(k_ref.dtype)
        k_ref[:, h * 2 * LANES + LANES:(h + 1) * 2 * LANES] = slots[h % 2].astype(k_ref.dtype)
    v_ref[...] = kv[:, nh * MLA_NOPE:].astype(v_ref.dtype)


def _mla_prep(cq, ckv, misc, cos, sin, qnw, kvnw, wq, wkv, tm=256):
    t = cq.shape[0]
    nh = MLA_HEADS
    row = lambda n: pl.BlockSpec((tm, n), lambda i: (i, 0))
    full = lambda a: pl.BlockSpec(a.shape, lambda i: (0, 0))
    qnw, kvnw = qnw.reshape(1, -1), kvnw.reshape(1, -1)
    return pl.pallas_call(
        _mla_prep_kernel,
        grid=(t // tm,),
        in_specs=[row(cq.shape[1]), row(ckv.shape[1]), row(misc.shape[1]), row(LANES), row(LANES),
                  full(qnw), full(kvnw), full(wq), full(wkv)],
        out_specs=[row(nh * 2 * LANES), row(nh * 2 * LANES), row(nh * MLA_V)],
        out_shape=[jax.ShapeDtypeStruct((t, nh * 2 * LANES), BF16),
                   jax.ShapeDtypeStruct((t, nh * 2 * LANES), BF16),
                   jax.ShapeDtypeStruct((t, nh * MLA_V), BF16)],
        compiler_params=_params("arbitrary"),
        name="mla_prep",
    )(cq, ckv, misc, cos, sin, qnw, kvnw, wq, wkv)


def _flash_kernel(q_ref, k_ref, v_ref, o_ref, m_ref, l_ref, acc_ref, *, tq):
    qi, ki = pl.program_id(2), pl.program_id(3)

    @pl.when(ki == 0)
    def _():
        m_ref[...] = jnp.full_like(m_ref, -jnp.inf)
        l_ref[...] = jnp.zeros_like(l_ref)
        acc_ref[...] = jnp.zeros_like(acc_ref)

    @pl.when(ki <= qi)
    def _():
        s = _dot_nt(q_ref[...], k_ref[...])
        ri = lax.broadcasted_iota(jnp.int32, (tq, tq), 0)
        ci = lax.broadcasted_iota(jnp.int32, (tq, tq), 1)
        s = jnp.where((ki < qi) | (ci <= ri), s, -jnp.inf)
        m_prev = m_ref[...]
        m_new = jnp.maximum(m_prev, jnp.max(s, axis=-1, keepdims=True))
        alpha = jnp.exp(m_prev - m_new)
        p = jnp.exp(s - m_new)
        l_ref[...] = alpha * l_ref[...] + jnp.sum(p, axis=-1, keepdims=True)
        acc_ref[...] = alpha * acc_ref[...] + _dot(p.astype(BF16), v_ref[...])
        m_ref[...] = m_new

    @pl.when(ki == qi)
    def _():
        o_ref[...] = (acc_ref[...] / l_ref[...]).astype(o_ref.dtype)


def _flash(q, k, v, batch, seq, tq=256):
    t = q.shape[0]
    nq = seq // tq
    nh = MLA_HEADS
    return pl.pallas_call(
        functools.partial(_flash_kernel, tq=tq),
        grid=(batch, nh, nq, nq),
        in_specs=[pl.BlockSpec((tq, 2 * LANES), lambda b, h, i, j: (b * nq + i, h)),
                  pl.BlockSpec((tq, 2 * LANES), lambda b, h, i, j: (b * nq + jnp.minimum(i, j), h)),
                  pl.BlockSpec((tq, MLA_V), lambda b, h, i, j: (b * nq + jnp.minimum(i, j), h))],
        out_specs=pl.BlockSpec((tq, MLA_V), lambda b, h, i, j: (b * nq + i, h)),
        out_shape=jax.ShapeDtypeStruct((t, nh * MLA_V), F32),
        scratch_shapes=[pltpu.VMEM((tq, 1), F32), pltpu.VMEM((tq, 1), F32), pltpu.VMEM((tq, MLA_V), F32)],
        compiler_params=_params("arbitrary", "arbitrary", "arbitrary", "arbitrary"),
        name="mla_flash",
    )(q, k, v)


def _finish(h_ref, acc, fnw_ref, o_ref, final):
    hn = h_ref[...] + acc
    o_ref[...] = _rms(hn, fnw_ref[...]) if final else hn


def _even_tail_kernel(a_ref, ys_ref, u_ref, q_ref, z_ref, h_ref, mk_ref, mv_ref, vec_ref, gw_ref, ow_ref,
                      fnw_ref, o_ref, *, final):
    w = RW_WIDTH
    z = z_ref[...]
    ga = (a_ref[...] * _silu(z[:, :w])).astype(BF16)
    yb = _gelu_tanh(ys_ref[...] + vec_ref[0:1, :] * u_ref[...])
    gate = _sigmoid(_dot(yb.astype(BF16), gw_ref[...]) + vec_ref[1:2, :])
    gb = (yb * gate * _silu(z[:, w:2 * w])).astype(BF16)
    gm = (_mem_attend(q_ref[...], mk_ref[0], mv_ref[0]) * _silu(z[:, 2 * w:])).astype(BF16)
    acc = _dot(ga, ow_ref[0:w, :]) + _dot(gb, ow_ref[w:2 * w, :]) + _dot(gm, ow_ref[2 * w:, :])
    _finish(h_ref, acc, fnw_ref, o_ref, final)


def _odd_tail_kernel(c_ref, d_ref, q_ref, z_ref, h_ref, mk_ref, mv_ref, ow_ref, fnw_ref, o_ref, *, final):
    w = D_MODEL
    z = z_ref[...]
    gc = c_ref[...].astype(BF16)
    gd = (d_ref[...] * _silu(z[:, :w])).astype(BF16)
    gm = (_mem_attend(q_ref[...], mk_ref[0], mv_ref[0]) * _silu(z[:, w:])).astype(BF16)
    acc = _dot(gc, ow_ref[0:w, :]) + _dot(gd, ow_ref[w:2 * w, :]) + _dot(gm, ow_ref[2 * w:, :])
    _finish(h_ref, acc, fnw_ref, o_ref, final)


def _tail_call(kernel, rows, consts, h, mk, mv, fnw, batch, seq, final, name, tm=256):
    t, d = h.shape
    nt = seq // tm
    row = lambda a: pl.BlockSpec((tm, a.shape[1]), lambda b, i: (b * nt + i, 0))
    full = lambda a: pl.BlockSpec(a.shape, lambda b, i: (0,) * a.ndim)
    memspec = pl.BlockSpec((1,) + mk.shape[1:], lambda b, i: (b, 0, 0))
    fnw = fnw.reshape(1, d)
    return pl.pallas_call(
        functools.partial(kernel, final=final),
        grid=(batch, nt),
        in_specs=[row(a) for a in rows] + [row(h), memspec, memspec] + [full(a) for a in consts] + [full(fnw)],
        out_specs=row(h),
        out_shape=jax.ShapeDtypeStruct((t, d), F32),
        compiler_params=_params("arbitrary", "arbitrary"),
        name=name,
    )(*rows, h, mk, mv, *consts, fnw)


def _pad_rows(rows, n=SUBLANES):
    width = rows[0].shape[-1]
    return jnp.concatenate([r.reshape(1, width) for r in rows] + [jnp.zeros((n - len(rows), width), F32)], axis=0)


def _even_layer(h, batch, seq, mk, mv, nw, in_w, out_w, fnw, final, mu, w0, w2, a0, a2, k_k, k_a, r_k,
                ln_w, ln_b, lam_re, lam_im, b_re, b_im, c_re, c_im, s5_d, log_dt, glu_w, glu_b):
    rw_proj = 3 * RW_WIDTH + 2 * RW_LORA
    ev_width = 2 * RW_WIDTH + MEM_WIDTH
    splits = (rw_proj, RW_WIDTH, MEM_WIDTH, ev_width)
    p, u, q_mem, z = _norm_proj(h, nw, in_w.astype(BF16), splits, (F32, F32, F32, F32))
    pp, wa = _rwkv_param_pack(mu, w0, w2, a0, a2, k_k, k_a, r_k, ln_w, ln_b)
    a_out = _rwkv(p, pp, wa, batch, seq)

    nj = seq // S5_SUB
    groups = RW_WIDTH // S5_GROUP
    tz, gm, cc, pw = _s5_param_pack(lam_re, lam_im, b_re, b_im, c_re, c_im, log_dt, nj)
    u_g = u.astype(BF16).reshape(batch * nj, S5_SUB, groups, S5_GROUP).transpose(2, 0, 1, 3)
    u_g = u_g.reshape(groups, batch * nj, S5_SUB * S5_GROUP)
    y_g = _s5(u_g, tz, gm, cc, pw, nj)
    ys = y_g.reshape(groups, batch * nj, S5_SUB, S5_GROUP).transpose(1, 2, 0, 3).reshape(batch * seq, RW_WIDTH)

    vec = _pad_rows([s5_d, glu_b])
    return _tail_call(_even_tail_kernel, [a_out, ys, u, q_mem, z], [vec, glu_w.astype(BF16), out_w.astype(BF16)],
                      h, mk, mv, fnw, batch, seq, final, "even_tail")


def _odd_layer(h, batch, seq, mk, mv, cos, sin, nw, in_w, out_w, fnw, final, conv_w, conv_b, dt_bias, a_log,
               m2_d, m2_norm_w, q_norm_w, wq_up, kv_norm_w, wkv_up):
    width = M2_HEADS * HEAD_DIM
    conv_dim = width + 2 * M2_GROUPS * M2_STATE
    nh = MLA_HEADS
    o_dt = conv_dim
    o_cq = o_dt + M2_HEADS
    o_ckv = o_cq + MLA_Q_RANK
    o_kr = o_ckv + MLA_KV_RANK
    o_qm = o_kr + MLA_ROPE
    o_z = o_qm + MEM_WIDTH
    half = MLA_ROPE // 2
    w_kr = in_w[:, o_kr:o_qm]
    w_kr_sw = jnp.concatenate([w_kr[:, half:], w_kr[:, :half]], axis=1)
    w_misc = jnp.concatenate([w_kr, w_kr, w_kr_sw, w_kr_sw, in_w[:, o_dt:o_cq],
                              jnp.zeros((in_w.shape[0], LANES - M2_HEADS), in_w.dtype)], axis=1)
    w_all = jnp.concatenate([in_w[:, :o_dt], in_w[:, o_cq:o_ckv], in_w[:, o_ckv:o_kr], in_w[:, o_qm:o_z],
                             in_w[:, o_z:], w_misc], axis=1).astype(BF16)
    splits = (conv_dim, MLA_Q_RANK, MLA_KV_RANK, MEM_WIDTH, 2 * D_MODEL + MEM_WIDTH, 3 * LANES)
    xbc, cq, ckv, q_mem, z, misc = _norm_proj(h, nw, w_all, splits, (F32,) * 6)

    cw = _pad_rows(list(conv_w) + [conv_b])
    pad16 = lambda x: jnp.concatenate([x, jnp.zeros((LANES - M2_HEADS,), F32)])
    vec = _pad_rows([pad16(dt_bias), pad16(-jnp.exp(a_log))])
    nwd = _pad_rows([m2_norm_w, jnp.repeat(m2_d, HEAD_DIM)])
    c_out = _ssd(xbc, misc, z[:, :width], cw, vec, nwd, batch, seq)

    wq = wq_up.reshape(MLA_Q_RANK, nh, MLA_NOPE + MLA_ROPE)
    wq_r = wq[:, :, MLA_NOPE:]
    wq_sw = jnp.concatenate([wq_r[:, :, half:], wq_r[:, :, :half]], axis=2)
    wq_all = jnp.concatenate([wq[:, :, :MLA_NOPE].reshape(MLA_Q_RANK, -1), wq_r.reshape(MLA_Q_RANK, -1),
                              wq_sw.reshape(MLA_Q_RANK, -1)], axis=1).astype(BF16)
    wkv = wkv_up.reshape(MLA_KV_RANK, nh, MLA_NOPE + MLA_V)
    wkv_all = jnp.concatenate([wkv[:, :, :MLA_NOPE].reshape(MLA_KV_RANK, -1),
                               wkv[:, :, MLA_NOPE:].reshape(MLA_KV_RANK, -1)], axis=1).astype(BF16)
    qc, kc, vv = _mla_prep(cq, ckv, misc, cos, sin, q_norm_w, kv_norm_w, wq_all, wkv_all)
    d_attn = _flash(qc, kc, vv, batch, seq)

    return _tail_call(_odd_tail_kernel, [c_out, d_attn, q_mem, z[:, width:]], [out_w.astype(BF16)],
                      h, mk, mv, fnw, batch, seq, final, "odd_tail")


def _rope_tables(positions):
    inv = 1.0 / (ROPE_THETA ** (jnp.arange(0, MLA_ROPE, 2, dtype=F32) / MLA_ROPE))
    ang = positions.astype(F32).reshape(-1, 1) * inv
    cos, sin = jnp.cos(ang), jnp.sin(ang)
    return jnp.tile(cos, (1, 4)), jnp.tile(jnp.concatenate([-sin, sin], axis=1), (1, 2))


def kernel(x, mem, positions, norm_w, mem_norm_w, final_norm_w, mem_kv_w, ev_in_w, ev_out_w, rw_mu, rw_w0, rw_w2, rw_a0, rw_a2, rw_k_k, rw_k_a, rw_r_k, rw_ln_w, rw_ln_b, s5_lambda_re, s5_lambda_im, s5_b_re, s5_b_im, s5_c_re, s5_c_im, s5_d, s5_log_dt, s5_glu_w, s5_glu_b, od_in_w, od_out_w, m2_conv_w, m2_conv_b, m2_dt_bias, m2_a_log, m2_d, m2_norm_w, mla_q_norm_w, mla_wq_up, mla_kv_norm_w, mla_wkv_up):
    batch, seq, d = x.shape
    depth = norm_w.shape[0]
    mk, mv = _mem_kv(mem, mem_norm_w, mem_kv_w.astype(BF16))
    cos, sin = _rope_tables(positions)
    h = x.reshape(batch * seq, d)
    for layer in range(depth):
        i = layer // 2
        final = layer == depth - 1
        if layer % 2 == 0:
            h = _even_layer(h, batch, seq, mk[layer], mv[layer], norm_w[layer], ev_in_w[i], ev_out_w[i],
                            final_norm_w, final, rw_mu[i], rw_w0[i], rw_w2[i], rw_a0[i], rw_a2[i], rw_k_k[i],
                            rw_k_a[i], rw_r_k[i], rw_ln_w[i], rw_ln_b[i], s5_lambda_re[i], s5_lambda_im[i],
                            s5_b_re[i], s5_b_im[i], s5_c_re[i], s5_c_im[i], s5_d[i], s5_log_dt[i],
                            s5_glu_w[i], s5_glu_b[i])
        else:
            h = _odd_layer(h, batch, seq, mk[layer], mv[layer], cos, sin, norm_w[layer], od_in_w[i], od_out_w[i],
                           final_norm_w, final, m2_conv_w[i], m2_conv_b[i], m2_dt_bias[i], m2_a_log[i], m2_d[i],
                           m2_norm_w[i], mla_q_norm_w[i], mla_wq_up[i], mla_kv_norm_w[i], mla_wkv_up[i])
    return h.reshape(batch, seq, d)
```

```python
import functools
import math

import jax
import jax.numpy as jnp
from jax import lax
from jax.experimental import pallas as pl
from jax.experimental.pallas import tpu as pltpu

F32 = jnp.float32
BF16 = jnp.bfloat16
HIGHEST = lax.Precision.HIGHEST

LANES = 128
SUBLANES = 8
VMEM_LIMIT_BYTES = 56 * 1024 * 1024

D_MODEL = 1024
HEAD_DIM = 64
NORM_EPS = 1e-6
RW_WIDTH = 1024
RW_LORA = 64
RW_GN_EPS = 64e-5
RW_CHUNK = 64
S5_GROUP = 16
S5_STATE = 64
S5_SUB = 16
M2_HEADS = 16
M2_GROUPS = 2
M2_STATE = 128
M2_CONV = 4
M2_CHUNK = 128
MLA_HEADS = 8
MLA_NOPE = 128
MLA_ROPE = 64
MLA_V = 128
MLA_Q_RANK = 384
MLA_KV_RANK = 256
ROPE_THETA = 10000.0
MEM_HEADS = 4
MEM_WIDTH = MEM_HEADS * HEAD_DIM


def _dot(a, b, precision=None):
    return jnp.dot(a, b, preferred_element_type=F32, precision=precision)


def _dot_nt(a, b, precision=None):
    return lax.dot_general(a, b, (((1,), (1,)), ((), ())), preferred_element_type=F32,
                           precision=precision)


def _sigmoid(x):
    return 1.0 / (1.0 + jnp.exp(-x))


def _silu(x):
    return x * _sigmoid(x)


def _softplus(x):
    return jnp.maximum(x, 0.0) + jnp.log(1.0 + jnp.exp(-jnp.abs(x)))


def _gelu_tanh(x):
    return 0.5 * x * (1.0 + jnp.tanh(math.sqrt(2.0 / math.pi) * (x + 0.044715 * (x * x * x))))


def _rms(x, w, eps=NORM_EPS):
    ms = jnp.mean(x * x, axis=-1, keepdims=True)
    return x * lax.rsqrt(ms + eps) * w


def _params(*sem):
    return pltpu.CompilerParams(dimension_semantics=sem, vmem_limit_bytes=VMEM_LIMIT_BYTES)


def _norm_proj_kernel(x_ref, nw_ref, w_ref, *out_refs, splits):
    xn = _rms(x_ref[...], nw_ref[...]).astype(BF16)
    off = 0
    for o_ref, n in zip(out_refs, splits):
        o_ref[...] = _dot(xn, w_ref[:, off:off + n]).astype(o_ref.dtype)
        off += n


def _norm_proj(x, nw, w, splits, dtypes, tm=256):
    t, d = x.shape
    n = w.shape[1]
    assert sum(splits) == n and t % tm == 0
    return pl.pallas_call(
        functools.partial(_norm_proj_kernel, splits=splits),
        grid=(t // tm,),
        in_specs=[pl.BlockSpec((tm, d), lambda i: (i, 0)),
                  pl.BlockSpec((1, d), lambda i: (0, 0)),
                  pl.BlockSpec((d, n), lambda i: (0, 0))],
        out_specs=[pl.BlockSpec((tm, s), lambda i: (i, 0)) for s in splits],
        out_shape=[jax.ShapeDtypeStruct((t, s), dt) for s, dt in zip(splits, dtypes)],
        compiler_params=_params("arbitrary"),
        name="norm_proj",
    )(x, nw.reshape(1, d), w)


def _mem_kv_kernel(mem_ref, nw_ref, w_ref, k_ref, v_ref):
    mn = _rms(mem_ref[0], nw_ref[...]).astype(BF16)
    kv = _dot(mn, w_ref[0])
    k_ref[0, 0] = kv[:, :MEM_WIDTH].astype(k_ref.dtype)
    v_ref[0, 0] = kv[:, MEM_WIDTH:].astype(v_ref.dtype)


def _mem_kv(mem, nw, w):
    b, m, d = mem.shape
    depth = w.shape[0]
    shp = jax.ShapeDtypeStruct((depth, b, m, MEM_WIDTH), BF16)
    return pl.pallas_call(
        _mem_kv_kernel,
        grid=(depth, b),
        in_specs=[pl.BlockSpec((1, m, d), lambda l, i: (i, 0, 0)),
                  pl.BlockSpec((1, d), lambda l, i: (0, 0)),
                  pl.BlockSpec((1, d, 2 * MEM_WIDTH), lambda l, i: (l, 0, 0))],
        out_specs=[pl.BlockSpec((1, 1, m, MEM_WIDTH), lambda l, i: (l, i, 0, 0))] * 2,
        out_shape=[shp, shp],
        compiler_params=_params("arbitrary", "arbitrary"),
        name="mem_kv",
    )(mem, nw.reshape(1, d), w)


def _mem_attend(q, k, v):
    lane = lax.broadcasted_iota(jnp.int32, (1, MEM_WIDTH), 1)
    qb = (q * (HEAD_DIM ** -0.5)).astype(BF16)
    out = jnp.zeros(q.shape, F32)
    for h in range(MEM_HEADS):
        hm = (lane >> 6) == h
        sc = _dot_nt(qb, jnp.where(hm, k, jnp.zeros_like(k)))
        sc = sc - jnp.max(sc, axis=-1, keepdims=True)
        p = jnp.exp(sc)
        p = p / jnp.sum(p, axis=-1, keepdims=True)
        out = out + _dot(p.astype(BF16), jnp.where(hm, v, jnp.zeros_like(v)))
    return out


def _split3(x):
    hi = x.astype(BF16)
    r1 = x - hi.astype(F32)
    mid = r1.astype(BF16)
    lo = (r1 - mid.astype(F32)).astype(BF16)
    return hi, mid, lo


def _rwkv_kernel(p_ref, mu_ref, vec_ref, wa_ref, out_ref, st_ref, prev_ref, *, L):
    c = pl.program_id(1)
    W = RW_WIDTH

    @pl.when(c == 0)
    def _():
        st_ref[...] = jnp.zeros_like(st_ref)
        prev_ref[...] = jnp.zeros_like(prev_ref)

    x = p_ref[...]
    row_l = lax.broadcasted_iota(jnp.int32, (L, 1), 0)
    xs = jnp.where(row_l == 0, prev_ref[0:1, :], pltpu.roll(x, 1, axis=0))
    prev_ref[0:1, :] = x[L - 1:L, :]
    xm = x + (xs - x) * mu_ref[...]
    r, k, v, lat = xm[:, :W], xm[:, W:2 * W], xm[:, 2 * W:3 * W], xm[:, 3 * W:]

    w0, a0, k_k, k_a, r_k, ln_w, ln_b = [vec_ref[i:i + 1, :] for i in range(7)]
    lane = lax.broadcasted_iota(jnp.int32, (1, LANES), 1)
    lat_t = jnp.where(lane < RW_LORA, jnp.tanh(lat), lat).astype(BF16)
    la = _dot(lat_t, wa_ref[...])
    log_w = -_softplus(-(w0 + la[:, :W])) - 0.5
    lw = -jnp.exp(log_w)
    iclr = _sigmoid(a0 + la[:, W:])
    kk0 = k * k_k
    kp = k * (1.0 + (iclr - 1.0) * k_a)
    rkr = r * kp * r_k

    il = lax.broadcasted_iota(jnp.int32, (L, 3 * L), 0)
    jl = lax.broadcasted_iota(jnp.int32, (L, 3 * L), 1) & (L - 1)
    tri3 = jnp.where(jl <= il, 1.0, 0.0).astype(BF16)
    cum = _dot(tri3, jnp.concatenate(_split3(lw), axis=0))
    cum_l = cum[L - 1:L, :]
    e_neg = jnp.exp(-cum)
    e_exc = jnp.exp(cum - lw)
    e_inc = jnp.exp(cum)
    dend = jnp.exp(cum_l - cum)
    e_tot = jnp.exp(cum_l)

    i2 = lax.broadcasted_iota(jnp.int32, (LANES, LANES), 0)
    j2 = lax.broadcasted_iota(jnp.int32, (LANES, LANES), 1)
    same_head = (i2 >> 6) == (j2 >> 6)
    head_ones = jnp.where(same_head, 1.0, 0.0).astype(BF16)
    strict = same_head & (j2 < i2)
    incl = same_head & (j2 <= i2)
    eye = jnp.where(i2 == j2, 1.0, 0.0)
    levels = [((i2 & -(2 * s)) == (j2 & -(2 * s))) & ((i2 & s) != 0) & ((j2 & s) == 0)
              for s in (1, 2, 4, 8, 16, 32)]
    m0 = lane < HEAD_DIM
    stack = lambda t: jnp.concatenate([jnp.where(m0, t, 0.0), jnp.where(m0, 0.0, t)], axis=0)

    pairs = range(W // LANES)
    sls = [slice(p * LANES, (p + 1) * LANES) for p in pairs]
    each = lambda f: [f(p) for p in pairs]
    sums = each(lambda p: _dot(jnp.concatenate([kk0[:, sls[p]] * kk0[:, sls[p]], rkr[:, sls[p]]],
                                               axis=0).astype(BF16), head_ones))
    kk = each(lambda p: kk0[:, sls[p]] * lax.rsqrt(jnp.maximum(sums[p][:L], 1e-24)))
    bvec = each(lambda p: kk[p] * iclr[:, sls[p]])
    ar = each(lambda p: jnp.concatenate([stack(-kk[p] * e_exc[:, sls[p]]), stack(r[:, sls[p]] * e_inc[:, sls[p]])],
                                        axis=0).astype(BF16))
    b2 = each(lambda p: stack(bvec[p] * e_neg[:, sls[p]]).astype(BF16))
    k2 = each(lambda p: stack(kp[:, sls[p]] * e_neg[:, sls[p]]).astype(BF16))
    v2f = each(lambda p: stack(v[:, sls[p]]))
    v2 = each(lambda p: v2f[p].astype(BF16))
    bk_end = each(lambda p: jnp.concatenate([stack(bvec[p] * dend[:, sls[p]]), stack(kp[:, sls[p]] * dend[:, sls[p]])],
                                            axis=0).astype(BF16))
    sb = each(lambda p: _dot_nt(ar[p], b2[p]))
    sk = each(lambda p: _dot_nt(ar[p], k2[p]))
    aab = each(lambda p: jnp.where(strict, sb[p][:LANES], 0.0))

    inv = each(lambda p: eye + jnp.where(levels[0], aab[p], 0.0))
    for lvl in levels[1:]:
        invb = each(lambda p: inv[p].astype(BF16))
        t1 = each(lambda p: _dot(invb[p], jnp.where(lvl, aab[p], 0.0).astype(BF16)).astype(BF16))
        inv = each(lambda p: inv[p] + _dot(t1[p], invb[p]))

    st = each(lambda p: st_ref[p])
    sta = each(lambda p: _dot_nt(ar[p], st[p].astype(BF16)))
    rhs = each(lambda p: sta[p][:LANES] + _dot(jnp.where(strict, sk[p][:LANES], 0.0).astype(BF16), v2[p]))
    u2 = each(lambda p: _dot(inv[p].astype(BF16), rhs[p].astype(BF16)))
    y2 = each(lambda p: sta[p][LANES:] + _dot(jnp.where(incl, sb[p][LANES:], 0.0).astype(BF16), u2[p].astype(BF16))
              + _dot(jnp.where(incl, sk[p][LANES:], 0.0).astype(BF16), v2[p]))
    y = each(lambda p: y2[p][:L, :] + y2[p][L:, :])
    for p in pairs:
        uv_t = jnp.concatenate([u2[p].T, v2f[p].T], axis=1).astype(BF16)
        st_ref[p] = st[p] * e_tot[:, sls[p]] + _dot(uv_t, bk_end[p])

    mean = each(lambda p: _dot(y[p].astype(BF16), head_ones) * (1.0 / HEAD_DIM))
    dcen = each(lambda p: y[p] - mean[p])
    var = each(lambda p: _dot((dcen[p] * dcen[p]).astype(BF16), head_ones) * (1.0 / HEAD_DIM))
    for p in pairs:
        out_ref[:, sls[p]] = (dcen[p] * lax.rsqrt(var[p] + RW_GN_EPS) * ln_w[:, sls[p]] + ln_b[:, sls[p]]
                              + sums[p][L:] * v[:, sls[p]])


def _rwkv(p, mu, vec, wa, batch, seq):
    L = RW_CHUNK
    t, pw = p.shape
    nc = seq // L
    return pl.pallas_call(
        functools.partial(_rwkv_kernel, L=L),
        grid=(batch, nc),
        in_specs=[pl.BlockSpec((L, pw), lambda b, c: (b * nc + c, 0)),
                  pl.BlockSpec((1, pw), lambda b, c: (0, 0)),
                  pl.BlockSpec(vec.shape, lambda b, c: (0, 0)),
                  pl.BlockSpec(wa.shape, lambda b, c: (0, 0))],
        out_specs=pl.BlockSpec((L, RW_WIDTH), lambda b, c: (b * nc + c, 0)),
        out_shape=jax.ShapeDtypeStruct((t, RW_WIDTH), F32),
        scratch_shapes=[pltpu.VMEM((RW_WIDTH // LANES, LANES, LANES), F32), pltpu.VMEM((SUBLANES, pw), F32)],
        compiler_params=_params("arbitrary", "arbitrary"),
        name="rwkv7",
    )(p, mu.reshape(1, pw), vec, wa)


def _rwkv_param_pack(w0, w2, a0, a2, k_k, k_a, r_k, ln_w, ln_b):
    vec = _pad_rows([w0, a0, k_k, k_a, r_k.reshape(-1), ln_w, ln_b])
    z = jnp.zeros_like(w2)
    wa = jnp.concatenate([jnp.concatenate([w2, z], axis=1), jnp.concatenate([z, a2], axis=1)], axis=0)
    return vec, wa.astype(BF16)


def _s5_kernel(u_ref, tz_ref, gm_ref, cc_ref, pw_ref, y_ref, *, nj):
    u = u_ref[0]
    y = _dot(u, tz_ref[0])
    x = _dot(u, gm_ref[0])
    rows = x.shape[0]
    jrow = lax.broadcasted_iota(jnp.int32, (rows, LANES), 0) & (nj - 1)
    pw = pw_ref[0]
    d, lvl = 1, 0
    while d < nj:
        xs = jnp.where(jrow >= d, pltpu.roll(x, d, axis=0), 0.0)
        xsw = pltpu.roll(xs, S5_STATE, axis=1)
        x = x + xs * pw[2 * lvl:2 * lvl + 1, :] + xsw * pw[2 * lvl + 1:2 * lvl + 2, :]
        d, lvl = 2 * d, lvl + 1
    h_in = jnp.where(jrow >= 1, pltpu.roll(x, 1, axis=0), 0.0)
    y_ref[0] = y + _dot(h_in.astype(BF16), cc_ref[0])


def _s5(u_g, tz, gm, cc, pw, nj):
    g, rows, width = u_g.shape
    return pl.pallas_call(
        functools.partial(_s5_kernel, nj=nj),
        grid=(g,),
        in_specs=[pl.BlockSpec((1, rows, width), lambda i: (i, 0, 0)),
                  pl.BlockSpec((1, width, width), lambda i: (i, 0, 0)),
                  pl.BlockSpec((1, width, 2 * S5_STATE), lambda i: (i, 0, 0)),
                  pl.BlockSpec((1, 2 * S5_STATE, width), lambda i: (i, 0, 0)),
                  pl.BlockSpec((1, 16, 2 * S5_STATE), lambda i: (i, 0, 0))],
        out_specs=pl.BlockSpec((1, rows, width), lambda i: (i, 0, 0)),
        out_shape=jax.ShapeDtypeStruct((g, rows, width), F32),
        compiler_params=_params("arbitrary"),
        name="s5",
    )(u_g, tz, gm, cc, pw)


def _s5_param_pack(lam_re, lam_im, b_re, b_im, c_re, c_im, log_dt, nj):
    ls = S5_SUB
    dt = jnp.exp(log_dt)[:, None]
    zr, zi = lam_re * dt, lam_im * dt

    def powers(steps):
        st = steps[:, None, None]
        mag = jnp.exp(st * zr)
        return mag * jnp.cos(st * zi), mag * jnp.sin(st * zi)

    ab_re, ab_im = powers(jnp.ones((1,), F32))
    ab_re, ab_im = ab_re[0], ab_im[0]
    den = lam_re * lam_re + lam_im * lam_im
    nr, ni = ab_re - 1.0, ab_im
    f_re = (nr * lam_re + ni * lam_im) / den
    f_im = (ni * lam_re - nr * lam_im) / den
    bb_re = f_re[..., None] * b_re - f_im[..., None] * b_im
    bb_im = f_re[..., None] * b_im + f_im[..., None] * b_re
    pr, pi = powers(jnp.arange(0, ls + 1, dtype=F32))
    cl_re = c_re[None] * pr[:, :, None, :] - c_im[None] * pi[:, :, None, :]
    cl_im = c_re[None] * pi[:, :, None, :] + c_im[None] * pr[:, :, None, :]
    kern = (jnp.einsum('dgpn,gnq->dgpq', cl_re[:ls], bb_re, precision=HIGHEST)
            - jnp.einsum('dgpn,gnq->dgpq', cl_im[:ls], bb_im, precision=HIGHEST))
    lag = jnp.arange(ls)[None, :] - jnp.arange(ls)[:, None]
    tz = jnp.where((lag >= 0)[:, :, None, None, None], kern[jnp.clip(lag, 0, ls - 1)], 0.0)
    g = lam_re.shape[0]
    pch = S5_GROUP
    tz = tz.transpose(2, 0, 4, 1, 3).reshape(g, ls * pch, ls * pch)
    rr, ri = pr[ls - 1::-1][:ls], pi[ls - 1::-1][:ls]
    gm_re = rr[..., None] * bb_re[None] - ri[..., None] * bb_im[None]
    gm_im = rr[..., None] * bb_im[None] + ri[..., None] * bb_re[None]
    gm = jnp.concatenate([gm_re, gm_im], axis=2)
    gm = gm.transpose(1, 0, 3, 2).reshape(g, ls * pch, 2 * S5_STATE)
    cc = jnp.concatenate([cl_re[1:], -cl_im[1:]], axis=3)
    cc = cc.transpose(1, 3, 0, 2).reshape(g, 2 * S5_STATE, ls * pch)
    lv = []
    d = 1
    while d < nj:
        lv.append(float(ls * d))
        d *= 2
    qr, qi = powers(jnp.asarray(lv, F32))
    pw = jnp.stack([jnp.concatenate([qr, qr], axis=-1), jnp.concatenate([-qi, qi], axis=-1)], axis=1)
    pw = pw.reshape(2 * len(lv), g, 2 * S5_STATE).transpose(1, 0, 2)
    pw = jnp.concatenate([pw, jnp.zeros((g, 16 - 2 * len(lv), 2 * S5_STATE), F32)], axis=1)
    return tz.astype(BF16), gm.astype(BF16), cc.astype(BF16), pw


def _ssd_kernel(xbc_ref, misc_ref, zc_ref, cw_ref, vec_ref, nw_ref, out_ref,
                st_ref, halo_ref, buf_ref, *, L):
    c = pl.program_id(1)
    width = M2_HEADS * HEAD_DIM
    gs = M2_STATE

    @pl.when(c == 0)
    def _():
        st_ref[...] = jnp.zeros_like(st_ref)
        halo_ref[...] = jnp.zeros_like(halo_ref)

    xbc = xbc_ref[...]
    buf_ref[0:SUBLANES, :] = halo_ref[...]
    buf_ref[SUBLANES:SUBLANES + L, :] = xbc
    halo_ref[...] = xbc[L - SUBLANES:L, :]
    cw = cw_ref[...]
    acc = cw[M2_CONV:M2_CONV + 1, :]
    for tap in range(M2_CONV):
        s0 = SUBLANES - (M2_CONV - 1) + tap
        acc = acc + cw[tap:tap + 1, :] * buf_ref[s0:s0 + L, :]
    xc = _silu(acc)
    xs = xc[:, :width]

    vec = vec_ref[...]
    dt = _softplus(misc_ref[:, 2 * LANES:3 * LANES] + vec[0:1, :])
    a_dt = dt * vec[1:2, :]
    il = lax.broadcasted_iota(jnp.int32, (L, L), 0)
    jl = lax.broadcasted_iota(jnp.int32, (L, L), 1)
    tril = jl <= il
    cum = _dot(tril.astype(F32), a_dt, precision=HIGHEST)
    cum_t = cum.T

    eh = lax.broadcasted_iota(jnp.int32, (LANES, width), 0)
    ec = lax.broadcasted_iota(jnp.int32, (LANES, width), 1)
    expand = ((ec >> 6) == eh).astype(F32)
    dt_x = _dot(dt, expand, precision=HIGHEST)
    cum_x = _dot(cum, expand, precision=HIGHEST)
    cum_lx = cum_x[L - 1:L, :]
    xdt = xs * dt_x
    xdd = xdt * jnp.exp(cum_lx - cum_x)
    ecum = jnp.exp(cum_x)

    rh = lax.broadcasted_iota(jnp.int32, (width, LANES), 0)
    rc = lax.broadcasted_iota(jnp.int32, (width, LANES), 1)
    expand_t = ((rh >> 6) == rc).astype(F32)
    tot_rows = _dot(expand_t, jnp.broadcast_to(cum_t[:, L - 1:L], (LANES, LANES)), precision=HIGHEST)

    lane = lax.broadcasted_iota(jnp.int32, (1, LANES), 1)
    m0 = lane < HEAD_DIM
    ys = []
    for pair in range(M2_HEADS // 2):
        g = pair // (M2_HEADS // 2 // M2_GROUPS)
        bm = xc[:, width + g * gs:width + (g + 1) * gs].astype(BF16)
        cm = xc[:, width + M2_GROUPS * gs + g * gs:width + M2_GROUPS * gs + (g + 1) * gs].astype(BF16)
        cb = _dot_nt(cm, bm)
        sl = slice(pair * LANES, (pair + 1) * LANES)
        xdt_p = xdt[:, sl]
        yp = jnp.zeros((L, LANES), F32)
        for hh in range(2):
            h = 2 * pair + hh
            diff = jnp.where(tril, cum[:, h:h + 1] - cum_t[h:h + 1, :], 0.0)
            lm = jnp.where(tril, jnp.exp(diff), 0.0)
            hm = m0 if hh == 0 else jnp.logical_not(m0)
            yp = yp + _dot((cb * lm).astype(BF16), jnp.where(hm, xdt_p, 0.0).astype(BF16))
        st_p = st_ref[sl, :]
        yp = yp + _dot_nt(cm, st_p.astype(BF16)) * ecum[:, sl]
        st_ref[sl, :] = st_p * jnp.exp(tot_rows[sl, :]) + _dot(xdd[:, sl].T.astype(BF16), bm)
        ys.append(yp)
    y = jnp.concatenate(ys, axis=1) + xs * nw_ref[1:2, :]
    y = y * _silu(zc_ref[...])
    gw = width // M2_GROUPS
    outs = []
    for g in range(M2_GROUPS):
        yg = y[:, g * gw:(g + 1) * gw]
        outs.append(_rms(yg, nw_ref[0:1, g * gw:(g + 1) * gw]))
    out_ref[...] = jnp.concatenate(outs, axis=1)


def _ssd(xbc, misc, z, cw, vec, nw, batch, seq):
    L = M2_CHUNK
    t, cd = xbc.shape
    nc = seq // L
    width = M2_HEADS * HEAD_DIM
    return pl.pallas_call(
        functools.partial(_ssd_kernel, L=L),
        grid=(batch, nc),
        in_specs=[pl.BlockSpec((L, cd), lambda b, c: (b * nc + c, 0)),
                  pl.BlockSpec((L, misc.shape[1]), lambda b, c: (b * nc + c, 0)),
                  pl.BlockSpec((L, width), lambda b, c: (b * nc + c, 0)),
                  pl.BlockSpec((SUBLANES, cd), lambda b, c: (0, 0)),
                  pl.BlockSpec((SUBLANES, LANES), lambda b, c: (0, 0)),
                  pl.BlockSpec((SUBLANES, width), lambda b, c: (0, 0))],
        out_specs=pl.BlockSpec((L, width), lambda b, c: (b * nc + c, 0)),
        out_shape=jax.ShapeDtypeStruct((t, width), F32),
        scratch_shapes=[pltpu.VMEM((width, M2_STATE), F32), pltpu.VMEM((SUBLANES, cd), F32),
                        pltpu.VMEM((L + SUBLANES, cd), F32)],
        compiler_params=_params("arbitrary", "arbitrary"),
        name="ssd",
    )(xbc, misc, z, cw, vec, nw)


def _mla_prep_kernel(cq_ref, ckv_ref, misc_ref, cos_ref, sin_ref, qnw_ref, kvnw_ref, wq_ref, wkv_ref,
                     q_ref, k_ref, v_ref):
    nh = MLA_HEADS
    cos, sin = cos_ref[...], sin_ref[...]
    q = _dot(_rms(cq_ref[...], qnw_ref[...]).astype(BF16), wq_ref[...])
    scale = (MLA_NOPE + MLA_ROPE) ** -0.5 * math.log2(math.e)
    rope0 = nh * MLA_NOPE
    rot0 = rope0 + nh * MLA_ROPE
    for pair in range(nh // 2):
        sl = slice(pair * LANES, (pair + 1) * LANES)
        qr = (q[:, rope0:rot0][:, sl] * cos + q[:, rot0:][:, sl] * sin) * scale
        for hh in range(2):
            h = 2 * pair + hh
            q_ref[:, h * 2 * LANES:h * 2 * LANES + LANES] = (
                q[:, h * MLA_NOPE:(h + 1) * MLA_NOPE] * scale).astype(q_ref.dtype)
            q_ref[:, h * 2 * LANES + LANES:(h + 1) * 2 * LANES] = qr.astype(q_ref.dtype)
    kv = _dot(_rms(ckv_ref[...], kvnw_ref[...]).astype(BF16), wkv_ref[...])
    kpe = misc_ref[:, :LANES] * cos + misc_ref[:, LANES:2 * LANES] * sin
    lane = lax.broadcasted_iota(jnp.int32, (1, LANES), 1)
    slots = [jnp.where(lane < MLA_ROPE, kpe, 0.0), jnp.where(lane < MLA_ROPE, 0.0, kpe)]
    for h in range(nh):
        k_ref[:, h * 2 * LANES:h * 2 * LANES + LANES] = kv[:, h * MLA_NOPE:(h + 1) * MLA_NOPE].astype(k_ref.dtype)
        k_ref[:, h * 2 * LANES + LANES:(h + 1) * 2 * LANES] = slots[h % 2].astype(k_ref.dtype)
    v_ref[...] = kv[:, nh * MLA_NOPE:].astype(v_ref.dtype)


def _mla_prep(cq, ckv, misc, cos, sin, qnw, kvnw, wq, wkv, tm=256):
    t = cq.shape[0]
    nh = MLA_HEADS
    row = lambda n: pl.BlockSpec((tm, n), lambda i: (i, 0))
    full = lambda a: pl.BlockSpec(a.shape, lambda i: (0, 0))
    qnw, kvnw = qnw.reshape(1, -1), kvnw.reshape(1, -1)
    return pl.pallas_call(
        _mla_prep_kernel,
        grid=(t // tm,),
        in_specs=[row(cq.shape[1]), row(ckv.shape[1]), row(misc.shape[1]), row(LANES), row(LANES),
                  full(qnw), full(kvnw), full(wq), full(wkv)],
        out_specs=[row(nh * 2 * LANES), row(nh * 2 * LANES), row(nh * MLA_V)],
        out_shape=[jax.ShapeDtypeStruct((t, nh * 2 * LANES), BF16),
                   jax.ShapeDtypeStruct((t, nh * 2 * LANES), BF16),
                   jax.ShapeDtypeStruct((t, nh * MLA_V), BF16)],
        compiler_params=_params("arbitrary"),
        name="mla_prep",
    )(cq, ckv, misc, cos, sin, qnw, kvnw, wq, wkv)


def _flash_kernel(q_ref, k_ref, v_ref, o_ref, *, tq, hb):
    qi = pl.program_id(2)
    qw, vw = 2 * LANES, MLA_V
    qs = [q_ref[:, h * qw:(h + 1) * qw] for h in range(hb)]

    def update(h, carry, kb, vb, mask):
        m_prev, l_prev, acc = carry
        s = _dot_nt(qs[h], kb)
        if mask is not None:
            s = jnp.where(mask, s, -jnp.inf)
        m_new = jnp.maximum(m_prev, jnp.max(s, axis=-1, keepdims=True))
        alpha = jnp.exp2(m_prev - m_new)
        p = jnp.exp2(s - m_new)
        l_new = alpha * l_prev + jnp.sum(p, axis=-1, keepdims=True)
        return m_new, l_new, alpha * acc + _dot(p.astype(BF16), vb)

    def block(j, carries, mask):
        off = pl.multiple_of(j * tq, tq)
        return tuple(update(h, carries[h], k_ref[pl.ds(off, tq), h * qw:(h + 1) * qw],
                            v_ref[pl.ds(off, tq), h * vw:(h + 1) * vw], mask) for h in range(hb))

    init = tuple((jnp.full((tq, 1), -jnp.inf, F32), jnp.zeros((tq, 1), F32), jnp.zeros((tq, vw), F32))
                 for _ in range(hb))
    carries = lax.fori_loop(0, qi, lambda j, c: block(j, c, None), init)
    ri = lax.broadcasted_iota(jnp.int32, (tq, tq), 0)
    ci = lax.broadcasted_iota(jnp.int32, (tq, tq), 1)
    carries = block(qi, carries, ci <= ri)
    for h in range(hb):
        _, l_fin, acc = carries[h]
        o_ref[:, h * vw:(h + 1) * vw] = (acc / l_fin).astype(o_ref.dtype)


def _flash(q, k, v, batch, seq, tq=512, hb=2):
    t = q.shape[0]
    tq = min(tq, seq)
    nq = seq // tq
    ng = MLA_HEADS // hb
    return pl.pallas_call(
        functools.partial(_flash_kernel, tq=tq, hb=hb),
        grid=(batch, ng, nq),
        in_specs=[pl.BlockSpec((tq, hb * 2 * LANES), lambda b, g, i: (b * nq + i, g)),
                  pl.BlockSpec((seq, hb * 2 * LANES), lambda b, g, i: (b, g)),
                  pl.BlockSpec((seq, hb * MLA_V), lambda b, g, i: (b, g))],
        out_specs=pl.BlockSpec((tq, hb * MLA_V), lambda b, g, i: (b * nq + i, g)),
        out_shape=jax.ShapeDtypeStruct((t, MLA_HEADS * MLA_V), F32),
        compiler_params=_params("arbitrary", "arbitrary", "arbitrary"),
        name="mla_flash",
    )(q, k, v)


def _finish(h_ref, acc, fnw_ref, o_ref, final):
    hn = h_ref[...] + acc
    o_ref[...] = _rms(hn, fnw_ref[...]) if final else hn


def _even_tail_kernel(a_ref, ys_ref, u_ref, q_ref, z_ref, h_ref, mk_ref, mv_ref, vec_ref, gw_ref, ow_ref,
                      fnw_ref, o_ref, *, final):
    w = RW_WIDTH
    z = z_ref[...]
    ga = (a_ref[...] * _silu(z[:, :w])).astype(BF16)
    yb = _gelu_tanh(ys_ref[...] + vec_ref[0:1, :] * u_ref[...])
    gate = _sigmoid(_dot(yb.astype(BF16), gw_ref[...]) + vec_ref[1:2, :])
    gb = (yb * gate * _silu(z[:, w:2 * w])).astype(BF16)
    gm = (_mem_attend(q_ref[...], mk_ref[0], mv_ref[0]) * _silu(z[:, 2 * w:])).astype(BF16)
    acc = _dot(ga, ow_ref[0:w, :]) + _dot(gb, ow_ref[w:2 * w, :]) + _dot(gm, ow_ref[2 * w:, :])
    _finish(h_ref, acc, fnw_ref, o_ref, final)


def _odd_tail_kernel(c_ref, d_ref, q_ref, z_ref, h_ref, mk_ref, mv_ref, ow_ref, fnw_ref, o_ref, *, final):
    w = D_MODEL
    z = z_ref[...]
    gc = c_ref[...].astype(BF16)
    gd = (d_ref[...] * _silu(z[:, :w])).astype(BF16)
    gm = (_mem_attend(q_ref[...], mk_ref[0], mv_ref[0]) * _silu(z[:, w:])).astype(BF16)
    acc = _dot(gc, ow_ref[0:w, :]) + _dot(gd, ow_ref[w:2 * w, :]) + _dot(gm, ow_ref[2 * w:, :])
    _finish(h_ref, acc, fnw_ref, o_ref, final)


def _tail_call(kernel, rows, consts, h, mk, mv, fnw, batch, seq, final, name, tm=256):
    t, d = h.shape
    nt = seq // tm
    row = lambda a: pl.BlockSpec((tm, a.shape[1]), lambda b, i: (b * nt + i, 0))
    full = lambda a: pl.BlockSpec(a.shape, lambda b, i: (0,) * a.ndim)
    memspec = pl.BlockSpec((1,) + mk.shape[1:], lambda b, i: (b, 0, 0))
    fnw = fnw.reshape(1, d)
    return pl.pallas_call(
        functools.partial(kernel, final=final),
        grid=(batch, nt),
        in_specs=[row(a) for a in rows] + [row(h), memspec, memspec] + [full(a) for a in consts] + [full(fnw)],
        out_specs=row(h),
        out_shape=jax.ShapeDtypeStruct((t, d), F32),
        compiler_params=_params("arbitrary", "arbitrary"),
        name=name,
    )(*rows, h, mk, mv, *consts, fnw)


def _pad_rows(rows, n=SUBLANES):
    width = rows[0].shape[-1]
    return jnp.concatenate([r.reshape(1, width) for r in rows] + [jnp.zeros((n - len(rows), width), F32)], axis=0)


def _even_layer(h, batch, seq, mk, mv, nw, in_w, out_w, fnw, final, mu, w0, w2, a0, a2, k_k, k_a, r_k,
                ln_w, ln_b, lam_re, lam_im, b_re, b_im, c_re, c_im, s5_d, log_dt, glu_w, glu_b):
    rw_proj = 3 * RW_WIDTH + 2 * RW_LORA
    ev_width = 2 * RW_WIDTH + MEM_WIDTH
    splits = (rw_proj, RW_WIDTH, MEM_WIDTH, ev_width)
    p, u, q_mem, z = _norm_proj(h, nw, in_w.astype(BF16), splits, (F32, F32, F32, F32))
    rvec, wa = _rwkv_param_pack(w0, w2, a0, a2, k_k, k_a, r_k, ln_w, ln_b)
    a_out = _rwkv(p, mu, rvec, wa, batch, seq)

    nj = seq // S5_SUB
    groups = RW_WIDTH // S5_GROUP
    tz, gm, cc, pw = _s5_param_pack(lam_re, lam_im, b_re, b_im, c_re, c_im, log_dt, nj)
    u_g = u.astype(BF16).reshape(batch * nj, S5_SUB, groups, S5_GROUP).transpose(2, 0, 1, 3)
    u_g = u_g.reshape(groups, batch * nj, S5_SUB * S5_GROUP)
    y_g = _s5(u_g, tz, gm, cc, pw, nj)
    ys = y_g.reshape(groups, batch * nj, S5_SUB, S5_GROUP).transpose(1, 2, 0, 3).reshape(batch * seq, RW_WIDTH)

    vec = _pad_rows([s5_d, glu_b])
    return _tail_call(_even_tail_kernel, [a_out, ys, u, q_mem, z], [vec, glu_w.astype(BF16), out_w.astype(BF16)],
                      h, mk, mv, fnw, batch, seq, final, "even_tail")


def _odd_layer(h, batch, seq, mk, mv, cos, sin, nw, in_w, out_w, fnw, final, conv_w, conv_b, dt_bias, a_log,
               m2_d, m2_norm_w, q_norm_w, wq_up, kv_norm_w, wkv_up):
    width = M2_HEADS * HEAD_DIM
    conv_dim = width + 2 * M2_GROUPS * M2_STATE
    nh = MLA_HEADS
    o_dt = conv_dim
    o_cq = o_dt + M2_HEADS
    o_ckv = o_cq + MLA_Q_RANK
    o_kr = o_ckv + MLA_KV_RANK
    o_qm = o_kr + MLA_ROPE
    o_z = o_qm + MEM_WIDTH
    half = MLA_ROPE // 2
    w_kr = in_w[:, o_kr:o_qm]
    w_kr_sw = jnp.concatenate([w_kr[:, half:], w_kr[:, :half]], axis=1)
    w_misc = jnp.concatenate([w_kr, w_kr, w_kr_sw, w_kr_sw, in_w[:, o_dt:o_cq],
                              jnp.zeros((in_w.shape[0], LANES - M2_HEADS), in_w.dtype)], axis=1)
    w_all = jnp.concatenate([in_w[:, :o_dt], in_w[:, o_cq:o_ckv], in_w[:, o_ckv:o_kr], in_w[:, o_qm:o_z],
                             in_w[:, o_z:], w_misc], axis=1).astype(BF16)
    splits = (conv_dim, MLA_Q_RANK, MLA_KV_RANK, MEM_WIDTH, 2 * D_MODEL + MEM_WIDTH, 3 * LANES)
    xbc, cq, ckv, q_mem, z, misc = _norm_proj(h, nw, w_all, splits, (F32,) * 6)

    cw = _pad_rows(list(conv_w) + [conv_b])
    pad16 = lambda x: jnp.concatenate([x, jnp.zeros((LANES - M2_HEADS,), F32)])
    vec = _pad_rows([pad16(dt_bias), pad16(-jnp.exp(a_log))])
    nwd = _pad_rows([m2_norm_w, jnp.repeat(m2_d, HEAD_DIM)])
    c_out = _ssd(xbc, misc, z[:, :width], cw, vec, nwd, batch, seq)

    wq = wq_up.reshape(MLA_Q_RANK, nh, MLA_NOPE + MLA_ROPE)
    wq_r = wq[:, :, MLA_NOPE:]
    wq_sw = jnp.concatenate([wq_r[:, :, half:], wq_r[:, :, :half]], axis=2)
    wq_all = jnp.concatenate([wq[:, :, :MLA_NOPE].reshape(MLA_Q_RANK, -1), wq_r.reshape(MLA_Q_RANK, -1),
                              wq_sw.reshape(MLA_Q_RANK, -1)], axis=1).astype(BF16)
    wkv = wkv_up.reshape(MLA_KV_RANK, nh, MLA_NOPE + MLA_V)
    wkv_all = jnp.concatenate([wkv[:, :, :MLA_NOPE].reshape(MLA_KV_RANK, -1),
                               wkv[:, :, MLA_NOPE:].reshape(MLA_KV_RANK, -1)], axis=1).astype(BF16)
    qc, kc, vv = _mla_prep(cq, ckv, misc, cos, sin, q_norm_w, kv_norm_w, wq_all, wkv_all)
    d_attn = _flash(qc, kc, vv, batch, seq)

    return _tail_call(_odd_tail_kernel, [c_out, d_attn, q_mem, z[:, width:]], [out_w.astype(BF16)],
                      h, mk, mv, fnw, batch, seq, final, "odd_tail")


def _rope_tables(positions):
    inv = 1.0 / (ROPE_THETA ** (jnp.arange(0, MLA_ROPE, 2, dtype=F32) / MLA_ROPE))
    ang = positions.astype(F32).reshape(-1, 1) * inv
    cos, sin = jnp.cos(ang), jnp.sin(ang)
    return jnp.tile(cos, (1, 4)), jnp.tile(jnp.concatenate([-sin, sin], axis=1), (1, 2))


def kernel(x, mem, positions, norm_w, mem_norm_w, final_norm_w, mem_kv_w, ev_in_w, ev_out_w, rw_mu, rw_w0, rw_w2, rw_a0, rw_a2, rw_k_k, rw_k_a, rw_r_k, rw_ln_w, rw_ln_b, s5_lambda_re, s5_lambda_im, s5_b_re, s5_b_im, s5_c_re, s5_c_im, s5_d, s5_log_dt, s5_glu_w, s5_glu_b, od_in_w, od_out_w, m2_conv_w, m2_conv_b, m2_dt_bias, m2_a_log, m2_d, m2_norm_w, mla_q_norm_w, mla_wq_up, mla_kv_norm_w, mla_wkv_up):
    batch, seq, d = x.shape
    depth = norm_w.shape[0]
    mk, mv = _mem_kv(mem, mem_norm_w, mem_kv_w.astype(BF16))
    cos, sin = _rope_tables(positions)
    h = x.reshape(batch * seq, d)
    for layer in range(depth):
        i = layer // 2
        final = layer == depth - 1
        if layer % 2 == 0:
            h = _even_layer(h, batch, seq, mk[layer], mv[layer], norm_w[layer], ev_in_w[i], ev_out_w[i],
                            final_norm_w, final, rw_mu[i], rw_w0[i], rw_w2[i], rw_a0[i], rw_a2[i], rw_k_k[i],
                            rw_k_a[i], rw_r_k[i], rw_ln_w[i], rw_ln_b[i], s5_lambda_re[i], s5_lambda_im[i],
                            s5_b_re[i], s5_b_im[i], s5_c_re[i], s5_c_im[i], s5_d[i], s5_log_dt[i],
                            s5_glu_w[i], s5_glu_b[i])
        else:
            h = _odd_layer(h, batch, seq, mk[layer], mv[layer], cos, sin, norm_w[layer], od_in_w[i], od_out_w[i],
                           final_norm_w, final, m2_conv_w[i], m2_conv_b[i], m2_dt_bias[i], m2_a_log[i], m2_d[i],
                           m2_norm_w[i], mla_q_norm_w[i], mla_wq_up[i], mla_kv_norm_w[i], mla_wkv_up[i])
    return h.reshape(batch, seq, d)
```

```python
import functools
import math

import jax
import jax.numpy as jnp
from jax import lax
from jax.experimental import pallas as pl
from jax.experimental.pallas import tpu as pltpu

F32 = jnp.float32
BF16 = jnp.bfloat16
HIGHEST = lax.Precision.HIGHEST

LANES = 128
SUBLANES = 8
VMEM_LIMIT_BYTES = 56 * 1024 * 1024

D_MODEL = 1024
HEAD_DIM = 64
NORM_EPS = 1e-6
RW_WIDTH = 1024
RW_LORA = 64
RW_GN_EPS = 64e-5
RW_CHUNK = 64
S5_GROUP = 16
S5_STATE = 64
S5_SUB = 16
M2_HEADS = 16
M2_GROUPS = 2
M2_STATE = 128
M2_CONV = 4
M2_CHUNK = 128
MLA_HEADS = 8
MLA_NOPE = 128
MLA_ROPE = 64
MLA_V = 128
MLA_Q_RANK = 384
MLA_KV_RANK = 256
ROPE_THETA = 10000.0
MEM_HEADS = 4
MEM_WIDTH = MEM_HEADS * HEAD_DIM


def _dot(a, b, precision=None):
    return jnp.dot(a, b, preferred_element_type=F32, precision=precision)


def _dot_nt(a, b, precision=None):
    return lax.dot_general(a, b, (((1,), (1,)), ((), ())), preferred_element_type=F32,
                           precision=precision)


def _sigmoid(x):
    return 1.0 / (1.0 + jnp.exp(-x))


def _silu(x):
    return x * _sigmoid(x)


def _softplus(x):
    return jnp.maximum(x, 0.0) + jnp.log(1.0 + jnp.exp(-jnp.abs(x)))


def _gelu_tanh(x):
    return 0.5 * x * (1.0 + jnp.tanh(math.sqrt(2.0 / math.pi) * (x + 0.044715 * (x * x * x))))


def _rms(x, w, eps=NORM_EPS):
    ms = jnp.mean(x * x, axis=-1, keepdims=True)
    return x * lax.rsqrt(ms + eps) * w


def _params(*sem):
    return pltpu.CompilerParams(dimension_semantics=sem, vmem_limit_bytes=VMEM_LIMIT_BYTES)


def _norm_proj_kernel(x_ref, nw_ref, w_ref, *out_refs, splits):
    xn = _rms(x_ref[...], nw_ref[...]).astype(BF16)
    off = 0
    for o_ref, n in zip(out_refs, splits):
        o_ref[...] = _dot(xn, w_ref[:, off:off + n]).astype(o_ref.dtype)
        off += n


def _norm_proj(x, nw, w, splits, dtypes, tm=256):
    t, d = x.shape
    n = w.shape[1]
    assert sum(splits) == n and t % tm == 0
    return pl.pallas_call(
        functools.partial(_norm_proj_kernel, splits=splits),
        grid=(t // tm,),
        in_specs=[pl.BlockSpec((tm, d), lambda i: (i, 0)),
                  pl.BlockSpec((1, d), lambda i: (0, 0)),
                  pl.BlockSpec((d, n), lambda i: (0, 0))],
        out_specs=[pl.BlockSpec((tm, s), lambda i: (i, 0)) for s in splits],
        out_shape=[jax.ShapeDtypeStruct((t, s), dt) for s, dt in zip(splits, dtypes)],
        compiler_params=_params("arbitrary"),
        name="norm_proj",
    )(x, nw.reshape(1, d), w)


def _mem_kv_kernel(mem_ref, nw_ref, w_ref, k_ref, v_ref):
    mn = _rms(mem_ref[0], nw_ref[...]).astype(BF16)
    kv = _dot(mn, w_ref[0])
    k_ref[0, 0] = kv[:, :MEM_WIDTH].astype(k_ref.dtype)
    v_ref[0, 0] = kv[:, MEM_WIDTH:].astype(v_ref.dtype)


def _mem_kv(mem, nw, w):
    b, m, d = mem.shape
    depth = w.shape[0]
    shp = jax.ShapeDtypeStruct((depth, b, m, MEM_WIDTH), BF16)
    return pl.pallas_call(
        _mem_kv_kernel,
        grid=(depth, b),
        in_specs=[pl.BlockSpec((1, m, d), lambda l, i: (i, 0, 0)),
                  pl.BlockSpec((1, d), lambda l, i: (0, 0)),
                  pl.BlockSpec((1, d, 2 * MEM_WIDTH), lambda l, i: (l, 0, 0))],
        out_specs=[pl.BlockSpec((1, 1, m, MEM_WIDTH), lambda l, i: (l, i, 0, 0))] * 2,
        out_shape=[shp, shp],
        compiler_params=_params("arbitrary", "arbitrary"),
        name="mem_kv",
    )(mem, nw.reshape(1, d), w)


def _mem_attend(q, k, v):
    lane = lax.broadcasted_iota(jnp.int32, (1, MEM_WIDTH), 1)
    qb = (q * (HEAD_DIM ** -0.5)).astype(BF16)
    out = jnp.zeros(q.shape, F32)
    for h in range(MEM_HEADS):
        hm = (lane >> 6) == h
        sc = _dot_nt(qb, jnp.where(hm, k, jnp.zeros_like(k)))
        sc = sc - jnp.max(sc, axis=-1, keepdims=True)
        p = jnp.exp(sc)
        p = p / jnp.sum(p, axis=-1, keepdims=True)
        out = out + _dot(p.astype(BF16), jnp.where(hm, v, jnp.zeros_like(v)))
    return out


def _split3(x):
    hi = x.astype(BF16)
    r1 = x - hi.astype(F32)
    mid = r1.astype(BF16)
    lo = (r1 - mid.astype(F32)).astype(BF16)
    return hi, mid, lo


def _rwkv_kernel(p_ref, mu_ref, vec_ref, wa_ref, out_ref, st_ref, prev_ref, *, L):
    c = pl.program_id(1)
    W = RW_WIDTH

    @pl.when(c == 0)
    def _():
        st_ref[...] = jnp.zeros_like(st_ref)
        prev_ref[...] = jnp.zeros_like(prev_ref)

    x = p_ref[...]
    row_l = lax.broadcasted_iota(jnp.int32, (L, 1), 0)
    xs = jnp.where(row_l == 0, prev_ref[0:1, :], pltpu.roll(x, 1, axis=0))
    prev_ref[0:1, :] = x[L - 1:L, :]
    xm = x + (xs - x) * mu_ref[...]
    r, k, v, lat = xm[:, :W], xm[:, W:2 * W], xm[:, 2 * W:3 * W], xm[:, 3 * W:]

    w0, a0, k_k, k_a, r_k, ln_w, ln_b = [vec_ref[i:i + 1, :] for i in range(7)]
    lane = lax.broadcasted_iota(jnp.int32, (1, LANES), 1)
    lat_t = jnp.where(lane < RW_LORA, jnp.tanh(lat), lat).astype(BF16)
    la = _dot(lat_t, wa_ref[...])
    log_w = -_softplus(-(w0 + la[:, :W])) - 0.5
    lw = -jnp.exp(log_w)
    iclr = _sigmoid(a0 + la[:, W:])
    kk0 = k * k_k
    kp = k * (1.0 + (iclr - 1.0) * k_a)
    rkr = r * kp * r_k

    il = lax.broadcasted_iota(jnp.int32, (L, 3 * L), 0)
    jl = lax.broadcasted_iota(jnp.int32, (L, 3 * L), 1) & (L - 1)
    tri3 = jnp.where(jl <= il, 1.0, 0.0).astype(BF16)
    cum = _dot(tri3, jnp.concatenate(_split3(lw), axis=0))
    cum_l = cum[L - 1:L, :]
    e_neg = jnp.exp(-cum)
    e_exc = jnp.exp(cum - lw)
    e_inc = jnp.exp(cum)
    dend = jnp.exp(cum_l - cum)
    e_tot = jnp.exp(cum_l)

    i2 = lax.broadcasted_iota(jnp.int32, (LANES, LANES), 0)
    j2 = lax.broadcasted_iota(jnp.int32, (LANES, LANES), 1)
    same_head = (i2 >> 6) == (j2 >> 6)
    head_ones = jnp.where(same_head, 1.0, 0.0).astype(BF16)
    strict = same_head & (j2 < i2)
    incl = same_head & (j2 <= i2)
    eye = jnp.where(i2 == j2, 1.0, 0.0)
    levels = [((i2 & -(2 * s)) == (j2 & -(2 * s))) & ((i2 & s) != 0) & ((j2 & s) == 0)
              for s in (1, 2, 4, 8, 16, 32)]
    m0 = lane < HEAD_DIM
    stack = lambda t: jnp.concatenate([jnp.where(m0, t, 0.0), jnp.where(m0, 0.0, t)], axis=0)

    pairs = range(W // LANES)
    sls = [slice(p * LANES, (p + 1) * LANES) for p in pairs]
    each = lambda f: [f(p) for p in pairs]
    sums = each(lambda p: _dot(jnp.concatenate([kk0[:, sls[p]] * kk0[:, sls[p]], rkr[:, sls[p]]],
                                               axis=0).astype(BF16), head_ones))
    kk = each(lambda p: kk0[:, sls[p]] * lax.rsqrt(jnp.maximum(sums[p][:L], 1e-24)))
    bvec = each(lambda p: kk[p] * iclr[:, sls[p]])
    ar = each(lambda p: jnp.concatenate([stack(-kk[p] * e_exc[:, sls[p]]), stack(r[:, sls[p]] * e_inc[:, sls[p]])],
                                        axis=0).astype(BF16))
    b2 = each(lambda p: stack(bvec[p] * e_neg[:, sls[p]]).astype(BF16))
    k2 = each(lambda p: stack(kp[:, sls[p]] * e_neg[:, sls[p]]).astype(BF16))
    v2f = each(lambda p: stack(v[:, sls[p]]))
    v2 = each(lambda p: v2f[p].astype(BF16))
    bk_end = each(lambda p: jnp.concatenate([stack(bvec[p] * dend[:, sls[p]]), stack(kp[:, sls[p]] * dend[:, sls[p]])],
                                            axis=0).astype(BF16))
    sb = each(lambda p: _dot_nt(ar[p], b2[p]))
    sk = each(lambda p: _dot_nt(ar[p], k2[p]))
    aab = each(lambda p: jnp.where(strict, sb[p][:LANES], 0.0))

    inv = each(lambda p: eye + jnp.where(levels[0], aab[p], 0.0))
    for lvl in levels[1:]:
        invb = each(lambda p: inv[p].astype(BF16))
        t1 = each(lambda p: _dot(invb[p], jnp.where(lvl, aab[p], 0.0).astype(BF16)).astype(BF16))
        inv = each(lambda p: inv[p] + _dot(t1[p], invb[p]))

    st = each(lambda p: st_ref[p])
    sta = each(lambda p: _dot_nt(ar[p], st[p].astype(BF16)))
    rhs = each(lambda p: sta[p][:LANES] + _dot(jnp.where(strict, sk[p][:LANES], 0.0).astype(BF16), v2[p]))
    u2 = each(lambda p: _dot(inv[p].astype(BF16), rhs[p].astype(BF16)))
    y2 = each(lambda p: sta[p][LANES:] + _dot(jnp.where(incl, sb[p][LANES:], 0.0).astype(BF16), u2[p].astype(BF16))
              + _dot(jnp.where(incl, sk[p][LANES:], 0.0).astype(BF16), v2[p]))
    y = each(lambda p: y2[p][:L, :] + y2[p][L:, :])
    for p in pairs:
        uv_t = jnp.concatenate([u2[p].T, v2f[p].T], axis=1).astype(BF16)
        st_ref[p] = st[p] * e_tot[:, sls[p]] + _dot(uv_t, bk_end[p])

    mean = each(lambda p: _dot(y[p].astype(BF16), head_ones) * (1.0 / HEAD_DIM))
    dcen = each(lambda p: y[p] - mean[p])
    var = each(lambda p: _dot((dcen[p] * dcen[p]).astype(BF16), head_ones) * (1.0 / HEAD_DIM))
    for p in pairs:
        out_ref[:, sls[p]] = (dcen[p] * lax.rsqrt(var[p] + RW_GN_EPS) * ln_w[:, sls[p]] + ln_b[:, sls[p]]
                              + sums[p][L:] * v[:, sls[p]])


def _rwkv(p, mu, vec, wa, batch, seq):
    L = RW_CHUNK
    t, pw = p.shape
    nc = seq // L
    return pl.pallas_call(
        functools.partial(_rwkv_kernel, L=L),
        grid=(batch, nc),
        in_specs=[pl.BlockSpec((L, pw), lambda b, c: (b * nc + c, 0)),
                  pl.BlockSpec((1, pw), lambda b, c: (0, 0)),
                  pl.BlockSpec(vec.shape, lambda b, c: (0, 0)),
                  pl.BlockSpec(wa.shape, lambda b, c: (0, 0))],
        out_specs=pl.BlockSpec((L, RW_WIDTH), lambda b, c: (b * nc + c, 0)),
        out_shape=jax.ShapeDtypeStruct((t, RW_WIDTH), F32),
        scratch_shapes=[pltpu.VMEM((RW_WIDTH // LANES, LANES, LANES), F32), pltpu.VMEM((SUBLANES, pw), F32)],
        compiler_params=_params("arbitrary", "arbitrary"),
        name="rwkv7",
    )(p, mu.reshape(1, pw), vec, wa)


def _rwkv_param_pack(w0, w2, a0, a2, k_k, k_a, r_k, ln_w, ln_b):
    vec = _pad_rows([w0, a0, k_k, k_a, r_k.reshape(-1), ln_w, ln_b])
    z = jnp.zeros_like(w2)
    wa = jnp.concatenate([jnp.concatenate([w2, z], axis=1), jnp.concatenate([z, a2], axis=1)], axis=0)
    return vec, wa.astype(BF16)


def _s5_kernel(u_ref, tz_ref, gm_ref, cc_ref, pw_ref, y_ref, *, nj):
    ls = S5_SUB
    xcat = jnp.concatenate([u_ref[pl.ds(l, nj, stride=ls), :] for l in range(ls)], axis=1).astype(BF16)
    y = _dot(xcat, tz_ref[0])
    x = _dot(xcat, gm_ref[0])
    half = x.shape[1] // 2
    jrow = lax.broadcasted_iota(jnp.int32, (nj, 1), 0)
    pw = pw_ref[0]
    d, lvl = 1, 0
    while d < nj:
        xs = jnp.where(jrow >= d, pltpu.roll(x, d, axis=0), 0.0)
        xsw = jnp.concatenate([xs[:, half:], xs[:, :half]], axis=1)
        x = x + xs * pw[2 * lvl:2 * lvl + 1, :] + xsw * pw[2 * lvl + 1:2 * lvl + 2, :]
        d, lvl = 2 * d, lvl + 1
    h_in = jnp.where(jrow >= 1, pltpu.roll(x, 1, axis=0), 0.0)
    y = y + _dot(h_in.astype(BF16), cc_ref[0])
    for l in range(ls):
        y_ref[pl.ds(l, nj, stride=ls), :] = y[:, l * LANES:(l + 1) * LANES]


def _s5(u, tz, gm, cc, pw, batch, seq):
    t, width = u.shape
    nslab = width // LANES
    nj = seq // S5_SUB
    slab_spec = lambda a: pl.BlockSpec((1,) + a.shape[1:], lambda s, b: (s, 0, 0))
    return pl.pallas_call(
        functools.partial(_s5_kernel, nj=nj),
        grid=(nslab, batch),
        in_specs=[pl.BlockSpec((seq, LANES), lambda s, b: (b, s)),
                  slab_spec(tz), slab_spec(gm), slab_spec(cc), slab_spec(pw)],
        out_specs=pl.BlockSpec((seq, LANES), lambda s, b: (b, s)),
        out_shape=jax.ShapeDtypeStruct((t, width), F32),
        compiler_params=_params("arbitrary", "arbitrary"),
        name="s5",
    )(u, tz, gm, cc, pw)


def _s5_slab_pack(tz, gm, cc, pw):
    g = tz.shape[0]
    gs = LANES // S5_GROUP
    ns, ls, pch, n = g // gs, S5_SUB, S5_GROUP, S5_STATE
    eye = jnp.eye(gs, dtype=tz.dtype)
    tzd = jnp.einsum('sgaqbp,gh->sagqbhp', tz.reshape(ns, gs, ls, pch, ls, pch), eye)
    tzd = tzd.reshape(ns, ls * LANES, ls * LANES)
    gmd = jnp.einsum('sglqcn,gh->slgqchn', gm.reshape(ns, gs, ls, pch, 2, n), eye)
    gmd = gmd.reshape(ns, ls * LANES, 2 * gs * n)
    ccd = jnp.einsum('sgcnlp,gh->scgnlhp', cc.reshape(ns, gs, 2, n, ls, pch), eye)
    ccd = ccd.reshape(ns, 2 * gs * n, ls * LANES)
    pwd = pw.reshape(ns, gs, pw.shape[1], 2, n).transpose(0, 2, 3, 1, 4).reshape(ns, pw.shape[1], 2 * gs * n)
    return tzd, gmd, ccd, pwd


def _s5_param_pack(lam_re, lam_im, b_re, b_im, c_re, c_im, log_dt, nj):
    ls = S5_SUB
    dt = jnp.exp(log_dt)[:, None]
    zr, zi = lam_re * dt, lam_im * dt

    def powers(steps):
        st = steps[:, None, None]
        mag = jnp.exp(st * zr)
        return mag * jnp.cos(st * zi), mag * jnp.sin(st * zi)

    ab_re, ab_im = powers(jnp.ones((1,), F32))
    ab_re, ab_im = ab_re[0], ab_im[0]
    den = lam_re * lam_re + lam_im * lam_im
    nr, ni = ab_re - 1.0, ab_im
    f_re = (nr * lam_re + ni * lam_im) / den
    f_im = (ni * lam_re - nr * lam_im) / den
    bb_re = f_re[..., None] * b_re - f_im[..., None] * b_im
    bb_im = f_re[..., None] * b_im + f_im[..., None] * b_re
    pr, pi = powers(jnp.arange(0, ls + 1, dtype=F32))
    cl_re = c_re[None] * pr[:, :, None, :] - c_im[None] * pi[:, :, None, :]
    cl_im = c_re[None] * pi[:, :, None, :] + c_im[None] * pr[:, :, None, :]
    kern = (jnp.einsum('dgpn,gnq->dgpq', cl_re[:ls], bb_re, precision=HIGHEST)
            - jnp.einsum('dgpn,gnq->dgpq', cl_im[:ls], bb_im, precision=HIGHEST))
    lag = jnp.arange(ls)[None, :] - jnp.arange(ls)[:, None]
    tz = jnp.where((lag >= 0)[:, :, None, None, None], kern[jnp.clip(lag, 0, ls - 1)], 0.0)
    g = lam_re.shape[0]
    pch = S5_GROUP
    tz = tz.transpose(2, 0, 4, 1, 3).reshape(g, ls * pch, ls * pch)
    rr, ri = pr[ls - 1::-1][:ls], pi[ls - 1::-1][:ls]
    gm_re = rr[..., None] * bb_re[None] - ri[..., None] * bb_im[None]
    gm_im = rr[..., None] * bb_im[None] + ri[..., None] * bb_re[None]
    gm = jnp.concatenate([gm_re, gm_im], axis=2)
    gm = gm.transpose(1, 0, 3, 2).reshape(g, ls * pch, 2 * S5_STATE)
    cc = jnp.concatenate([cl_re[1:], -cl_im[1:]], axis=3)
    cc = cc.transpose(1, 3, 0, 2).reshape(g, 2 * S5_STATE, ls * pch)
    lv = []
    d = 1
    while d < nj:
        lv.append(float(ls * d))
        d *= 2
    qr, qi = powers(jnp.asarray(lv, F32))
    pw = jnp.stack([jnp.concatenate([qr, qr], axis=-1), jnp.concatenate([-qi, qi], axis=-1)], axis=1)
    pw = pw.reshape(2 * len(lv), g, 2 * S5_STATE).transpose(1, 0, 2)
    pw = jnp.concatenate([pw, jnp.zeros((g, 16 - 2 * len(lv), 2 * S5_STATE), F32)], axis=1)
    return tz.astype(BF16), gm.astype(BF16), cc.astype(BF16), pw


def _ssd_kernel(xbc_ref, misc_ref, zc_ref, cw_ref, vec_ref, nw_ref, out_ref,
                st_ref, halo_ref, buf_ref, *, L):
    c = pl.program_id(1)
    width = M2_HEADS * HEAD_DIM
    gs = M2_STATE

    @pl.when(c == 0)
    def _():
        st_ref[...] = jnp.zeros_like(st_ref)
        halo_ref[...] = jnp.zeros_like(halo_ref)

    xbc = xbc_ref[...]
    buf_ref[0:SUBLANES, :] = halo_ref[...]
    buf_ref[SUBLANES:SUBLANES + L, :] = xbc
    halo_ref[...] = xbc[L - SUBLANES:L, :]
    cw = cw_ref[...]
    acc = cw[M2_CONV:M2_CONV + 1, :]
    for tap in range(M2_CONV):
        s0 = SUBLANES - (M2_CONV - 1) + tap
        acc = acc + cw[tap:tap + 1, :] * buf_ref[s0:s0 + L, :]
    xc = _silu(acc)
    xs = xc[:, :width]

    vec = vec_ref[...]
    dt = _softplus(misc_ref[:, 2 * LANES:3 * LANES] + vec[0:1, :])
    a_dt = dt * vec[1:2, :]
    il = lax.broadcasted_iota(jnp.int32, (L, 3 * L), 0)
    jl = lax.broadcasted_iota(jnp.int32, (L, 3 * L), 1) & (L - 1)
    tri3 = jnp.where(jl <= il, 1.0, 0.0).astype(BF16)
    cum = _dot(tri3, jnp.concatenate(_split3(a_dt), axis=0))
    cum_t = cum.T

    eh = lax.broadcasted_iota(jnp.int32, (3 * LANES, width), 0) & (LANES - 1)
    ec = lax.broadcasted_iota(jnp.int32, (3 * LANES, width), 1)
    expand3 = jnp.where((ec >> 6) == eh, 1.0, 0.0).astype(BF16)
    dt_x = _dot(jnp.concatenate(_split3(dt), axis=1), expand3)
    cum_x = _dot(jnp.concatenate(_split3(cum), axis=1), expand3)
    cum_lx = cum_x[L - 1:L, :]
    xdt = xs * dt_x
    xdd = (xdt * jnp.exp(cum_lx - cum_x)).astype(BF16)
    xdt_b = xdt.astype(BF16)
    ecum = jnp.exp(cum_x)
    e_tot = jnp.exp(cum_lx)

    tril = (lax.broadcasted_iota(jnp.int32, (L, L), 1) <= lax.broadcasted_iota(jnp.int32, (L, L), 0))
    lane = lax.broadcasted_iota(jnp.int32, (1, LANES), 1)
    m0 = lane < HEAD_DIM
    hpg = M2_HEADS // M2_GROUPS
    gw = hpg * HEAD_DIM
    groups = range(M2_GROUPS)
    heads = range(M2_HEADS)
    bm_f = [xc[:, width + g * gs:width + (g + 1) * gs] for g in groups]
    bm = [b.astype(BF16) for b in bm_f]
    cm = [xc[:, width + (M2_GROUPS + g) * gs:width + (M2_GROUPS + g + 1) * gs].astype(BF16) for g in groups]
    cbm = [jnp.where(tril, _dot_nt(cm[g], bm[g]), 0.0) for g in groups]
    st = [st_ref[:, g * gw:(g + 1) * gw] for g in groups]
    y_off = [_dot(cm[g], st[g].astype(BF16)) for g in groups]
    for g in groups:
        st_ref[:, g * gw:(g + 1) * gw] = (st[g] * e_tot[:, g * gw:(g + 1) * gw]
                                          + _dot(bm_f[g].T.astype(BF16), xdd[:, g * gw:(g + 1) * gw]))
    mh = [(cbm[h // hpg] * jnp.exp(jnp.minimum(cum[:, h:h + 1] - cum_t[h:h + 1, :], 0.0))).astype(BF16)
          for h in heads]
    zero = jnp.zeros((), BF16)
    yd = [_dot(mh[h], jnp.where(m0 if h % 2 == 0 else jnp.logical_not(m0),
                                xdt_b[:, (h // 2) * LANES:(h // 2 + 1) * LANES], zero)) for h in heads]
    y_diag = jnp.concatenate([yd[2 * p] + yd[2 * p + 1] for p in range(M2_HEADS // 2)], axis=1)
    y = y_diag + jnp.concatenate(y_off, axis=1) * ecum + xs * nw_ref[1:2, :]
    y = y * _silu(zc_ref[...])
    gw = width // M2_GROUPS
    outs = []
    for g in range(M2_GROUPS):
        yg = y[:, g * gw:(g + 1) * gw]
        outs.append(_rms(yg, nw_ref[0:1, g * gw:(g + 1) * gw]))
    out_ref[...] = jnp.concatenate(outs, axis=1)


def _ssd(xbc, misc, z, cw, vec, nw, batch, seq):
    L = M2_CHUNK
    t, cd = xbc.shape
    nc = seq // L
    width = M2_HEADS * HEAD_DIM
    return pl.pallas_call(
        functools.partial(_ssd_kernel, L=L),
        grid=(batch, nc),
        in_specs=[pl.BlockSpec((L, cd), lambda b, c: (b * nc + c, 0)),
                  pl.BlockSpec((L, misc.shape[1]), lambda b, c: (b * nc + c, 0)),
                  pl.BlockSpec((L, width), lambda b, c: (b * nc + c, 0)),
                  pl.BlockSpec((SUBLANES, cd), lambda b, c: (0, 0)),
                  pl.BlockSpec((SUBLANES, LANES), lambda b, c: (0, 0)),
                  pl.BlockSpec((SUBLANES, width), lambda b, c: (0, 0))],
        out_specs=pl.BlockSpec((L, width), lambda b, c: (b * nc + c, 0)),
        out_shape=jax.ShapeDtypeStruct((t, width), F32),
        scratch_shapes=[pltpu.VMEM((M2_STATE, width), F32), pltpu.VMEM((SUBLANES, cd), F32),
                        pltpu.VMEM((L + SUBLANES, cd), F32)],
        compiler_params=_params("arbitrary", "arbitrary"),
        name="ssd",
    )(xbc, misc, z, cw, vec, nw)


def _mla_prep_kernel(cq_ref, ckv_ref, misc_ref, cos_ref, sin_ref, qnw_ref, kvnw_ref, wq_ref, wkv_ref,
                     q_ref, k_ref, v_ref):
    nh = MLA_HEADS
    cos, sin = cos_ref[...], sin_ref[...]
    q = _dot(_rms(cq_ref[...], qnw_ref[...]).astype(BF16), wq_ref[...])
    scale = (MLA_NOPE + MLA_ROPE) ** -0.5 * math.log2(math.e)
    rope0 = nh * MLA_NOPE
    rot0 = rope0 + nh * MLA_ROPE
    for pair in range(nh // 2):
        sl = slice(pair * LANES, (pair + 1) * LANES)
        qr = (q[:, rope0:rot0][:, sl] * cos + q[:, rot0:][:, sl] * sin) * scale
        for hh in range(2):
            h = 2 * pair + hh
            q_ref[:, h * 2 * LANES:h * 2 * LANES + LANES] = (
                q[:, h * MLA_NOPE:(h + 1) * MLA_NOPE] * scale).astype(q_ref.dtype)
            q_ref[:, h * 2 * LANES + LANES:(h + 1) * 2 * LANES] = qr.astype(q_ref.dtype)
    kv = _dot(_rms(ckv_ref[...], kvnw_ref[...]).astype(BF16), wkv_ref[...])
    kpe = misc_ref[:, :LANES] * cos + misc_ref[:, LANES:2 * LANES] * sin
    lane = lax.broadcasted_iota(jnp.int32, (1, LANES), 1)
    slots = [jnp.where(lane < MLA_ROPE, kpe, 0.0), jnp.where(lane < MLA_ROPE, 0.0, kpe)]
    for h in range(nh):
        k_ref[:, h * 2 * LANES:h * 2 * LANES + LANES] = kv[:, h * MLA_NOPE:(h + 1) * MLA_NOPE].astype(k_ref.dtype)
        k_ref[:, h * 2 * LANES + LANES:(h + 1) * 2 * LANES] = slots[h % 2].astype(k_ref.dtype)
    v_ref[...] = kv[:, nh * MLA_NOPE:].astype(v_ref.dtype)


def _mla_prep(cq, ckv, misc, cos, sin, qnw, kvnw, wq, wkv, tm=256):
    t = cq.shape[0]
    nh = MLA_HEADS
    row = lambda n: pl.BlockSpec((tm, n), lambda i: (i, 0))
    full = lambda a: pl.BlockSpec(a.shape, lambda i: (0, 0))
    qnw, kvnw = qnw.reshape(1, -1), kvnw.reshape(1, -1)
    return pl.pallas_call(
        _mla_prep_kernel,
        grid=(t // tm,),
        in_specs=[row(cq.shape[1]), row(ckv.shape[1]), row(misc.shape[1]), row(LANES), row(LANES),
                  full(qnw), full(kvnw), full(wq), full(wkv)],
        out_specs=[row(nh * 2 * LANES), row(nh * 2 * LANES), row(nh * MLA_V)],
        out_shape=[jax.ShapeDtypeStruct((t, nh * 2 * LANES), BF16),
                   jax.ShapeDtypeStruct((t, nh * 2 * LANES), BF16),
                   jax.ShapeDtypeStruct((t, nh * MLA_V), BF16)],
        compiler_params=_params("arbitrary"),
        name="mla_prep",
    )(cq, ckv, misc, cos, sin, qnw, kvnw, wq, wkv)


def _flash_kernel(q_ref, k_ref, v_ref, o_ref, *, tq, hb):
    qi = pl.program_id(2)
    qw, vw = 2 * LANES, MLA_V
    qs = [q_ref[:, h * qw:(h + 1) * qw] for h in range(hb)]

    def update(h, carry, kb, vb, mask):
        m_prev, l_prev, acc = carry
        s = _dot_nt(qs[h], kb)
        if mask is not None:
            s = jnp.where(mask, s, -jnp.inf)
        m_new = jnp.maximum(m_prev, jnp.max(s, axis=-1, keepdims=True))
        alpha = jnp.exp2(m_prev - m_new)
        p = jnp.exp2(s - m_new)
        l_new = alpha * l_prev + jnp.sum(p, axis=-1, keepdims=True)
        return m_new, l_new, alpha * acc + _dot(p.astype(BF16), vb)

    def block(j, carries, mask):
        off = pl.multiple_of(j * tq, tq)
        return tuple(update(h, carries[h], k_ref[pl.ds(off, tq), h * qw:(h + 1) * qw],
                            v_ref[pl.ds(off, tq), h * vw:(h + 1) * vw], mask) for h in range(hb))

    init = tuple((jnp.full((tq, 1), -jnp.inf, F32), jnp.zeros((tq, 1), F32), jnp.zeros((tq, vw), F32))
                 for _ in range(hb))
    carries = lax.fori_loop(0, qi, lambda j, c: block(j, c, None), init)
    ri = lax.broadcasted_iota(jnp.int32, (tq, tq), 0)
    ci = lax.broadcasted_iota(jnp.int32, (tq, tq), 1)
    carries = block(qi, carries, ci <= ri)
    for h in range(hb):
        _, l_fin, acc = carries[h]
        o_ref[:, h * vw:(h + 1) * vw] = (acc / l_fin).astype(o_ref.dtype)


def _flash(q, k, v, batch, seq, tq=512, hb=2):
    t = q.shape[0]
    tq = min(tq, seq)
    nq = seq // tq
    ng = MLA_HEADS // hb
    return pl.pallas_call(
        functools.partial(_flash_kernel, tq=tq, hb=hb),
        grid=(batch, ng, nq),
        in_specs=[pl.BlockSpec((tq, hb * 2 * LANES), lambda b, g, i: (b * nq + i, g)),
                  pl.BlockSpec((seq, hb * 2 * LANES), lambda b, g, i: (b, g)),
                  pl.BlockSpec((seq, hb * MLA_V), lambda b, g, i: (b, g))],
        out_specs=pl.BlockSpec((tq, hb * MLA_V), lambda b, g, i: (b * nq + i, g)),
        out_shape=jax.ShapeDtypeStruct((t, MLA_HEADS * MLA_V), F32),
        compiler_params=_params("arbitrary", "arbitrary", "arbitrary"),
        name="mla_flash",
    )(q, k, v)


def _finish(h_ref, acc, fnw_ref, o_ref, final):
    hn = h_ref[...] + acc
    o_ref[...] = _rms(hn, fnw_ref[...]) if final else hn


def _even_tail_kernel(a_ref, ys_ref, u_ref, q_ref, z_ref, h_ref, mk_ref, mv_ref, vec_ref, gw_ref, ow_ref,
                      fnw_ref, o_ref, *, final):
    w = RW_WIDTH
    z = z_ref[...]
    ga = (a_ref[...] * _silu(z[:, :w])).astype(BF16)
    yb = _gelu_tanh(ys_ref[...] + vec_ref[0:1, :] * u_ref[...])
    gate = _sigmoid(_dot(yb.astype(BF16), gw_ref[...]) + vec_ref[1:2, :])
    gb = (yb * gate * _silu(z[:, w:2 * w])).astype(BF16)
    gm = (_mem_attend(q_ref[...], mk_ref[0], mv_ref[0]) * _silu(z[:, 2 * w:])).astype(BF16)
    acc = _dot(ga, ow_ref[0:w, :]) + _dot(gb, ow_ref[w:2 * w, :]) + _dot(gm, ow_ref[2 * w:, :])
    _finish(h_ref, acc, fnw_ref, o_ref, final)


def _odd_tail_kernel(c_ref, d_ref, q_ref, z_ref, h_ref, mk_ref, mv_ref, ow_ref, fnw_ref, o_ref, *, final):
    w = D_MODEL
    z = z_ref[...]
    gc = c_ref[...].astype(BF16)
    gd = (d_ref[...] * _silu(z[:, :w])).astype(BF16)
    gm = (_mem_attend(q_ref[...], mk_ref[0], mv_ref[0]) * _silu(z[:, w:])).astype(BF16)
    acc = _dot(gc, ow_ref[0:w, :]) + _dot(gd, ow_ref[w:2 * w, :]) + _dot(gm, ow_ref[2 * w:, :])
    _finish(h_ref, acc, fnw_ref, o_ref, final)


def _tail_call(kernel, rows, consts, h, mk, mv, fnw, batch, seq, final, name, tm=256):
    t, d = h.shape
    nt = seq // tm
    row = lambda a: pl.BlockSpec((tm, a.shape[1]), lambda b, i: (b * nt + i, 0))
    full = lambda a: pl.BlockSpec(a.shape, lambda b, i: (0,) * a.ndim)
    memspec = pl.BlockSpec((1,) + mk.shape[1:], lambda b, i: (b, 0, 0))
    fnw = fnw.reshape(1, d)
    return pl.pallas_call(
        functools.partial(kernel, final=final),
        grid=(batch, nt),
        in_specs=[row(a) for a in rows] + [row(h), memspec, memspec] + [full(a) for a in consts] + [full(fnw)],
        out_specs=row(h),
        out_shape=jax.ShapeDtypeStruct((t, d), F32),
        compiler_params=_params("arbitrary", "arbitrary"),
        name=name,
    )(*rows, h, mk, mv, *consts, fnw)


def _pad_rows(rows, n=SUBLANES):
    width = rows[0].shape[-1]
    return jnp.concatenate([r.reshape(1, width) for r in rows] + [jnp.zeros((n - len(rows), width), F32)], axis=0)


def _even_layer(h, batch, seq, mk, mv, nw, in_w, out_w, fnw, final, mu, w0, w2, a0, a2, k_k, k_a, r_k,
                ln_w, ln_b, lam_re, lam_im, b_re, b_im, c_re, c_im, s5_d, log_dt, glu_w, glu_b):
    rw_proj = 3 * RW_WIDTH + 2 * RW_LORA
    ev_width = 2 * RW_WIDTH + MEM_WIDTH
    splits = (rw_proj, RW_WIDTH, MEM_WIDTH, ev_width)
    p, u, q_mem, z = _norm_proj(h, nw, in_w.astype(BF16), splits, (F32, F32, F32, F32))
    rvec, wa = _rwkv_param_pack(w0, w2, a0, a2, k_k, k_a, r_k, ln_w, ln_b)
    a_out = _rwkv(p, mu, rvec, wa, batch, seq)

    nj = seq // S5_SUB
    groups = RW_WIDTH // S5_GROUP
    tz, gm, cc, pw = _s5_param_pack(lam_re, lam_im, b_re, b_im, c_re, c_im, log_dt, nj)
    ys = _s5(u, *_s5_slab_pack(tz, gm, cc, pw), batch, seq)

    vec = _pad_rows([s5_d, glu_b])
    return _tail_call(_even_tail_kernel, [a_out, ys, u, q_mem, z], [vec, glu_w.astype(BF16), out_w.astype(BF16)],
                      h, mk, mv, fnw, batch, seq, final, "even_tail")


def _odd_layer(h, batch, seq, mk, mv, cos, sin, nw, in_w, out_w, fnw, final, conv_w, conv_b, dt_bias, a_log,
               m2_d, m2_norm_w, q_norm_w, wq_up, kv_norm_w, wkv_up):
    width = M2_HEADS * HEAD_DIM
    conv_dim = width + 2 * M2_GROUPS * M2_STATE
    nh = MLA_HEADS
    o_dt = conv_dim
    o_cq = o_dt + M2_HEADS
    o_ckv = o_cq + MLA_Q_RANK
    o_kr = o_ckv + MLA_KV_RANK
    o_qm = o_kr + MLA_ROPE
    o_z = o_qm + MEM_WIDTH
    half = MLA_ROPE // 2
    w_kr = in_w[:, o_kr:o_qm]
    w_kr_sw = jnp.concatenate([w_kr[:, half:], w_kr[:, :half]], axis=1)
    w_misc = jnp.concatenate([w_kr, w_kr, w_kr_sw, w_kr_sw, in_w[:, o_dt:o_cq],
                              jnp.zeros((in_w.shape[0], LANES - M2_HEADS), in_w.dtype)], axis=1)
    w_all = jnp.concatenate([in_w[:, :o_dt], in_w[:, o_cq:o_ckv], in_w[:, o_ckv:o_kr], in_w[:, o_qm:o_z],
                             in_w[:, o_z:], w_misc], axis=1).astype(BF16)
    splits = (conv_dim, MLA_Q_RANK, MLA_KV_RANK, MEM_WIDTH, width, D_MODEL + MEM_WIDTH, 3 * LANES)
    xbc, cq, ckv, q_mem, z_c, z_dm, misc = _norm_proj(h, nw, w_all, splits, (F32,) * 7)

    cw = _pad_rows(list(conv_w) + [conv_b])
    pad16 = lambda x: jnp.concatenate([x, jnp.zeros((LANES - M2_HEADS,), F32)])
    vec = _pad_rows([pad16(dt_bias), pad16(-jnp.exp(a_log))])
    nwd = _pad_rows([m2_norm_w, jnp.repeat(m2_d, HEAD_DIM)])
    c_out = _ssd(xbc, misc, z_c, cw, vec, nwd, batch, seq)

    wq = wq_up.reshape(MLA_Q_RANK, nh, MLA_NOPE + MLA_ROPE)
    wq_r = wq[:, :, MLA_NOPE:]
    wq_sw = jnp.concatenate([wq_r[:, :, half:], wq_r[:, :, :half]], axis=2)
    wq_all = jnp.concatenate([wq[:, :, :MLA_NOPE].reshape(MLA_Q_RANK, -1), wq_r.reshape(MLA_Q_RANK, -1),
                              wq_sw.reshape(MLA_Q_RANK, -1)], axis=1).astype(BF16)
    wkv = wkv_up.reshape(MLA_KV_RANK, nh, MLA_NOPE + MLA_V)
    wkv_all = jnp.concatenate([wkv[:, :, :MLA_NOPE].reshape(MLA_KV_RANK, -1),
                               wkv[:, :, MLA_NOPE:].reshape(MLA_KV_RANK, -1)], axis=1).astype(BF16)
    qc, kc, vv = _mla_prep(cq, ckv, misc, cos, sin, q_norm_w, kv_norm_w, wq_all, wkv_all)
    d_attn = _flash(qc, kc, vv, batch, seq)

    return _tail_call(_odd_tail_kernel, [c_out, d_attn, q_mem, z_dm], [out_w.astype(BF16)],
                      h, mk, mv, fnw, batch, seq, final, "odd_tail")


def _rope_tables(positions):
    inv = 1.0 / (ROPE_THETA ** (jnp.arange(0, MLA_ROPE, 2, dtype=F32) / MLA_ROPE))
    ang = positions.astype(F32).reshape(-1, 1) * inv
    cos, sin = jnp.cos(ang), jnp.sin(ang)
    return jnp.tile(cos, (1, 4)), jnp.tile(jnp.concatenate([-sin, sin], axis=1), (1, 2))


def kernel(x, mem, positions, norm_w, mem_norm_w, final_norm_w, mem_kv_w, ev_in_w, ev_out_w, rw_mu, rw_w0, rw_w2, rw_a0, rw_a2, rw_k_k, rw_k_a, rw_r_k, rw_ln_w, rw_ln_b, s5_lambda_re, s5_lambda_im, s5_b_re, s5_b_im, s5_c_re, s5_c_im, s5_d, s5_log_dt, s5_glu_w, s5_glu_b, od_in_w, od_out_w, m2_conv_w, m2_conv_b, m2_dt_bias, m2_a_log, m2_d, m2_norm_w, mla_q_norm_w, mla_wq_up, mla_kv_norm_w, mla_wkv_up):
    batch, seq, d = x.shape
    depth = norm_w.shape[0]
    mk, mv = _mem_kv(mem, mem_norm_w, mem_kv_w.astype(BF16))
    cos, sin = _rope_tables(positions)
    h = x.reshape(batch * seq, d)
    for layer in range(depth):
        i = layer // 2
        final = layer == depth - 1
        if layer % 2 == 0:
            h = _even_layer(h, batch, seq, mk[layer], mv[layer], norm_w[layer], ev_in_w[i], ev_out_w[i],
                            final_norm_w, final, rw_mu[i], rw_w0[i], rw_w2[i], rw_a0[i], rw_a2[i], rw_k_k[i],
                            rw_k_a[i], rw_r_k[i], rw_ln_w[i], rw_ln_b[i], s5_lambda_re[i], s5_lambda_im[i],
                            s5_b_re[i], s5_b_im[i], s5_c_re[i], s5_c_im[i], s5_d[i], s5_log_dt[i],
                            s5_glu_w[i], s5_glu_b[i])
        else:
            h = _odd_layer(h, batch, seq, mk[layer], mv[layer], cos, sin, norm_w[layer], od_in_w[i], od_out_w[i],
                           final_norm_w, final, m2_conv_w[i], m2_conv_b[i], m2_dt_bias[i], m2_a_log[i], m2_d[i],
                           m2_norm_w[i], mla_q_norm_w[i], mla_wq_up[i], mla_kv_norm_w[i], mla_wkv_up[i])
    return h.reshape(batch, seq, d)
```

```python
import functools
import math

import jax
import jax.numpy as jnp
from jax import lax
from jax.experimental import pallas as pl
from jax.experimental.pallas import tpu as pltpu

F32 = jnp.float32
BF16 = jnp.bfloat16
HIGHEST = lax.Precision.HIGHEST

LANES = 128
SUBLANES = 8
VMEM_LIMIT_BYTES = 56 * 1024 * 1024

D_MODEL = 1024
HEAD_DIM = 64
NORM_EPS = 1e-6
RW_WIDTH = 1024
RW_LORA = 64
RW_GN_EPS = 64e-5
RW_CHUNK = 64
S5_GROUP = 16
S5_STATE = 64
S5_SUB = 16
M2_HEADS = 16
M2_GROUPS = 2
M2_STATE = 128
M2_CONV = 4
M2_CHUNK = 128
MLA_HEADS = 8
MLA_NOPE = 128
MLA_ROPE = 64
MLA_V = 128
MLA_Q_RANK = 384
MLA_KV_RANK = 256
ROPE_THETA = 10000.0
MEM_HEADS = 4
MEM_WIDTH = MEM_HEADS * HEAD_DIM


def _dot(a, b, precision=None):
    return jnp.dot(a, b, preferred_element_type=F32, precision=precision)


def _dot_nt(a, b, precision=None):
    return lax.dot_general(a, b, (((1,), (1,)), ((), ())), preferred_element_type=F32,
                           precision=precision)


def _sigmoid(x):
    return 1.0 / (1.0 + jnp.exp(-x))


def _silu(x):
    return x * _sigmoid(x)


def _softplus(x):
    return jnp.maximum(x, 0.0) + jnp.log(1.0 + jnp.exp(-jnp.abs(x)))


def _gelu_tanh(x):
    return 0.5 * x * (1.0 + jnp.tanh(math.sqrt(2.0 / math.pi) * (x + 0.044715 * (x * x * x))))


def _rms(x, w, eps=NORM_EPS):
    ms = jnp.mean(x * x, axis=-1, keepdims=True)
    return x * lax.rsqrt(ms + eps) * w


def _params(*sem):
    return pltpu.CompilerParams(dimension_semantics=sem, vmem_limit_bytes=VMEM_LIMIT_BYTES)


def _norm_proj_kernel(x_ref, nw_ref, w_ref, *out_refs, splits):
    xn = _rms(x_ref[...], nw_ref[...]).astype(BF16)
    off = 0
    for o_ref, n in zip(out_refs, splits):
        o_ref[...] = _dot(xn, w_ref[:, off:off + n]).astype(o_ref.dtype)
        off += n


def _norm_proj(x, nw, w, splits, dtypes, tm=256):
    t, d = x.shape
    n = w.shape[1]
    assert sum(splits) == n and t % tm == 0
    return pl.pallas_call(
        functools.partial(_norm_proj_kernel, splits=splits),
        grid=(t // tm,),
        in_specs=[pl.BlockSpec((tm, d), lambda i: (i, 0)),
                  pl.BlockSpec((1, d), lambda i: (0, 0)),
                  pl.BlockSpec((d, n), lambda i: (0, 0))],
        out_specs=[pl.BlockSpec((tm, s), lambda i: (i, 0)) for s in splits],
        out_shape=[jax.ShapeDtypeStruct((t, s), dt) for s, dt in zip(splits, dtypes)],
        compiler_params=_params("arbitrary"),
        name="norm_proj",
    )(x, nw.reshape(1, d), w)


def _mem_kv_kernel(mem_ref, nw_ref, w_ref, k_ref, v_ref):
    mn = _rms(mem_ref[0], nw_ref[...]).astype(BF16)
    kv = _dot(mn, w_ref[0])
    k_ref[0, 0] = kv[:, :MEM_WIDTH].astype(k_ref.dtype)
    v_ref[0, 0] = kv[:, MEM_WIDTH:].astype(v_ref.dtype)


def _mem_kv(mem, nw, w):
    b, m, d = mem.shape
    depth = w.shape[0]
    shp = jax.ShapeDtypeStruct((depth, b, m, MEM_WIDTH), BF16)
    return pl.pallas_call(
        _mem_kv_kernel,
        grid=(depth, b),
        in_specs=[pl.BlockSpec((1, m, d), lambda l, i: (i, 0, 0)),
                  pl.BlockSpec((1, d), lambda l, i: (0, 0)),
                  pl.BlockSpec((1, d, 2 * MEM_WIDTH), lambda l, i: (l, 0, 0))],
        out_specs=[pl.BlockSpec((1, 1, m, MEM_WIDTH), lambda l, i: (l, i, 0, 0))] * 2,
        out_shape=[shp, shp],
        compiler_params=_params("arbitrary", "arbitrary"),
        name="mem_kv",
    )(mem, nw.reshape(1, d), w)


def _mem_attend(q, k, v):
    lane = lax.broadcasted_iota(jnp.int32, (1, MEM_WIDTH), 1)
    qb = (q * (HEAD_DIM ** -0.5)).astype(BF16)
    out = jnp.zeros(q.shape, F32)
    for h in range(MEM_HEADS):
        hm = (lane >> 6) == h
        sc = _dot_nt(qb, jnp.where(hm, k, jnp.zeros_like(k)))
        sc = sc - jnp.max(sc, axis=-1, keepdims=True)
        p = jnp.exp(sc)
        p = p / jnp.sum(p, axis=-1, keepdims=True)
        out = out + _dot(p.astype(BF16), jnp.where(hm, v, jnp.zeros_like(v)))
    return out


def _split3(x):
    hi = x.astype(BF16)
    r1 = x - hi.astype(F32)
    mid = r1.astype(BF16)
    lo = (r1 - mid.astype(F32)).astype(BF16)
    return hi, mid, lo


def _rwkv_kernel(p_ref, mu_ref, vec_ref, wa_ref, out_ref, st_ref, prev_ref, *, L):
    c = pl.program_id(1)
    W = RW_WIDTH

    @pl.when(c == 0)
    def _():
        st_ref[...] = jnp.zeros_like(st_ref)
        prev_ref[...] = jnp.zeros_like(prev_ref)

    x = p_ref[...]
    row_l = lax.broadcasted_iota(jnp.int32, (L, 1), 0)
    xs = jnp.where(row_l == 0, prev_ref[0:1, :], pltpu.roll(x, 1, axis=0))
    prev_ref[0:1, :] = x[L - 1:L, :]
    xm = x + (xs - x) * mu_ref[...]
    r, k, v, lat = xm[:, :W], xm[:, W:2 * W], xm[:, 2 * W:3 * W], xm[:, 3 * W:]

    w0, a0, k_k, k_a, r_k, ln_w, ln_b = [vec_ref[i:i + 1, :] for i in range(7)]
    lane = lax.broadcasted_iota(jnp.int32, (1, LANES), 1)
    lat_t = jnp.where(lane < RW_LORA, jnp.tanh(lat), lat).astype(BF16)
    la = _dot(lat_t, wa_ref[...])
    log_w = -_softplus(-(w0 + la[:, :W])) - 0.5
    lw = -jnp.exp(log_w)
    iclr = _sigmoid(a0 + la[:, W:])
    kk0 = k * k_k
    kp = k * (1.0 + (iclr - 1.0) * k_a)
    rkr = r * kp * r_k

    il = lax.broadcasted_iota(jnp.int32, (L, 3 * L), 0)
    jl = lax.broadcasted_iota(jnp.int32, (L, 3 * L), 1) & (L - 1)
    tri3 = jnp.where(jl <= il, 1.0, 0.0).astype(BF16)
    cum = _dot(tri3, jnp.concatenate(_split3(lw), axis=0))
    cum_l = cum[L - 1:L, :]
    e_neg = jnp.exp(-cum)
    e_exc = jnp.exp(cum - lw)
    e_inc = jnp.exp(cum)
    dend = jnp.exp(cum_l - cum)
    e_tot = jnp.exp(cum_l)

    i2 = lax.broadcasted_iota(jnp.int32, (LANES, LANES), 0)
    j2 = lax.broadcasted_iota(jnp.int32, (LANES, LANES), 1)
    same_head = (i2 >> 6) == (j2 >> 6)
    head_ones = jnp.where(same_head, 1.0, 0.0).astype(BF16)
    strict = same_head & (j2 < i2)
    incl = same_head & (j2 <= i2)
    eye = jnp.where(i2 == j2, 1.0, 0.0)
    levels = [((i2 & -(2 * s)) == (j2 & -(2 * s))) & ((i2 & s) != 0) & ((j2 & s) == 0)
              for s in (1, 2, 4, 8, 16, 32)]
    m0 = lane < HEAD_DIM
    stack = lambda t: jnp.concatenate([jnp.where(m0, t, 0.0), jnp.where(m0, 0.0, t)], axis=0)

    pairs = range(W // LANES)
    sls = [slice(p * LANES, (p + 1) * LANES) for p in pairs]
    each = lambda f: [f(p) for p in pairs]
    sums = each(lambda p: _dot(jnp.concatenate([kk0[:, sls[p]] * kk0[:, sls[p]], rkr[:, sls[p]]],
                                               axis=0).astype(BF16), head_ones))
    kk = each(lambda p: kk0[:, sls[p]] * lax.rsqrt(jnp.maximum(sums[p][:L], 1e-24)))
    bvec = each(lambda p: kk[p] * iclr[:, sls[p]])
    ar = each(lambda p: jnp.concatenate([stack(-kk[p] * e_exc[:, sls[p]]), stack(r[:, sls[p]] * e_inc[:, sls[p]])],
                                        axis=0).astype(BF16))
    b2 = each(lambda p: stack(bvec[p] * e_neg[:, sls[p]]).astype(BF16))
    k2 = each(lambda p: stack(kp[:, sls[p]] * e_neg[:, sls[p]]).astype(BF16))
    v2f = each(lambda p: stack(v[:, sls[p]]))
    v2 = each(lambda p: v2f[p].astype(BF16))
    bk_end = each(lambda p: jnp.concatenate([stack(bvec[p] * dend[:, sls[p]]), stack(kp[:, sls[p]] * dend[:, sls[p]])],
                                            axis=0).astype(BF16))
    sb = each(lambda p: _dot_nt(ar[p], b2[p]))
    sk = each(lambda p: _dot_nt(ar[p], k2[p]))
    aab = each(lambda p: jnp.where(strict, sb[p][:LANES], 0.0))

    inv = each(lambda p: eye + jnp.where(levels[0], aab[p], 0.0))
    for lvl in levels[1:]:
        invb = each(lambda p: inv[p].astype(BF16))
        t1 = each(lambda p: _dot(invb[p], jnp.where(lvl, aab[p], 0.0).astype(BF16)).astype(BF16))
        inv = each(lambda p: inv[p] + _dot(t1[p], invb[p]))

    st = each(lambda p: st_ref[p])
    sta = each(lambda p: _dot_nt(ar[p], st[p].astype(BF16)))
    rhs = each(lambda p: sta[p][:LANES] + _dot(jnp.where(strict, sk[p][:LANES], 0.0).astype(BF16), v2[p]))
    u2 = each(lambda p: _dot(inv[p].astype(BF16), rhs[p].astype(BF16)))
    y2 = each(lambda p: sta[p][LANES:] + _dot(jnp.where(incl, sb[p][LANES:], 0.0).astype(BF16), u2[p].astype(BF16))
              + _dot(jnp.where(incl, sk[p][LANES:], 0.0).astype(BF16), v2[p]))
    y = each(lambda p: y2[p][:L, :] + y2[p][L:, :])
    for p in pairs:
        uv_t = jnp.concatenate([u2[p].T, v2f[p].T], axis=1).astype(BF16)
        st_ref[p] = st[p] * e_tot[:, sls[p]] + _dot(uv_t, bk_end[p])

    mean = each(lambda p: _dot(y[p].astype(BF16), head_ones) * (1.0 / HEAD_DIM))
    dcen = each(lambda p: y[p] - mean[p])
    var = each(lambda p: _dot((dcen[p] * dcen[p]).astype(BF16), head_ones) * (1.0 / HEAD_DIM))
    for p in pairs:
        out_ref[:, sls[p]] = (dcen[p] * lax.rsqrt(var[p] + RW_GN_EPS) * ln_w[:, sls[p]] + ln_b[:, sls[p]]
                              + sums[p][L:] * v[:, sls[p]])


def _rwkv(p, mu, vec, wa, batch, seq):
    L = RW_CHUNK
    t, pw = p.shape
    nc = seq // L
    return pl.pallas_call(
        functools.partial(_rwkv_kernel, L=L),
        grid=(batch, nc),
        in_specs=[pl.BlockSpec((L, pw), lambda b, c: (b * nc + c, 0)),
                  pl.BlockSpec((1, pw), lambda b, c: (0, 0)),
                  pl.BlockSpec(vec.shape, lambda b, c: (0, 0)),
                  pl.BlockSpec(wa.shape, lambda b, c: (0, 0))],
        out_specs=pl.BlockSpec((L, RW_WIDTH), lambda b, c: (b * nc + c, 0)),
        out_shape=jax.ShapeDtypeStruct((t, RW_WIDTH), F32),
        scratch_shapes=[pltpu.VMEM((RW_WIDTH // LANES, LANES, LANES), F32), pltpu.VMEM((SUBLANES, pw), F32)],
        compiler_params=_params("arbitrary", "arbitrary"),
        name="rwkv7",
    )(p, mu.reshape(1, pw), vec, wa)


def _rwkv_param_pack(w0, w2, a0, a2, k_k, k_a, r_k, ln_w, ln_b):
    vec = _pad_rows([w0, a0, k_k, k_a, r_k.reshape(-1), ln_w, ln_b])
    z = jnp.zeros_like(w2)
    wa = jnp.concatenate([jnp.concatenate([w2, z], axis=1), jnp.concatenate([z, a2], axis=1)], axis=0)
    return vec, wa.astype(BF16)


def _s5_kernel(u_ref, tz_ref, gm_ref, cc_ref, pw_ref, y_ref, tzd_ref, gmd_ref, ccd_ref, *, nj):
    ls, pch, n = S5_SUB, S5_GROUP, S5_STATE
    gs = LANES // pch

    @pl.when(pl.program_id(1) == 0)
    def _():
        tzd_ref[...] = jnp.zeros_like(tzd_ref)
        gmd_ref[...] = jnp.zeros_like(gmd_ref)
        ccd_ref[...] = jnp.zeros_like(ccd_ref)
        for g in range(gs):
            for a in range(ls):
                r0 = a * LANES + g * pch
                for b in range(a, ls):
                    c0 = b * LANES + g * pch
                    tzd_ref[r0:r0 + pch, c0:c0 + pch] = tz_ref[g, a * pch:(a + 1) * pch, b * pch:(b + 1) * pch]
                for c in range(2):
                    gmd_ref[r0:r0 + pch, (c * gs + g) * n:(c * gs + g + 1) * n] = (
                        gm_ref[g, a * pch:(a + 1) * pch, c * n:(c + 1) * n])
                    ccd_ref[(c * gs + g) * n:(c * gs + g + 1) * n, r0:r0 + pch] = (
                        cc_ref[g, c * n:(c + 1) * n, a * pch:(a + 1) * pch])

    xcat = jnp.concatenate([u_ref[pl.ds(l, nj, stride=ls), :] for l in range(ls)], axis=1).astype(BF16)
    y = _dot(xcat, tzd_ref[...])
    x = _dot(xcat, gmd_ref[...])
    half = x.shape[1] // 2
    jrow = lax.broadcasted_iota(jnp.int32, (nj, 1), 0)
    pw = pw_ref[0]
    d, lvl = 1, 0
    while d < nj:
        xs = jnp.where(jrow >= d, pltpu.roll(x, d, axis=0), 0.0)
        xsw = jnp.concatenate([xs[:, half:], xs[:, :half]], axis=1)
        x = x + xs * pw[2 * lvl:2 * lvl + 1, :] + xsw * pw[2 * lvl + 1:2 * lvl + 2, :]
        d, lvl = 2 * d, lvl + 1
    h_in = jnp.where(jrow >= 1, pltpu.roll(x, 1, axis=0), 0.0)
    y = y + _dot(h_in.astype(BF16), ccd_ref[...])
    for l in range(ls):
        y_ref[pl.ds(l, nj, stride=ls), :] = y[:, l * LANES:(l + 1) * LANES]


def _s5(u, tz, gm, cc, pw, batch, seq):
    t, width = u.shape
    nslab = width // LANES
    gs = LANES // S5_GROUP
    nj = seq // S5_SUB
    sub_w = S5_SUB * LANES
    st_w = 2 * gs * S5_STATE
    pwd = pw.reshape(nslab, gs, pw.shape[1], 2, S5_STATE).transpose(0, 2, 3, 1, 4).reshape(nslab, pw.shape[1], st_w)
    grp_spec = lambda a: pl.BlockSpec((gs,) + a.shape[1:], lambda s, b: (s, 0, 0))
    return pl.pallas_call(
        functools.partial(_s5_kernel, nj=nj),
        grid=(nslab, batch),
        in_specs=[pl.BlockSpec((seq, LANES), lambda s, b: (b, s)),
                  grp_spec(tz), grp_spec(gm), grp_spec(cc),
                  pl.BlockSpec((1,) + pwd.shape[1:], lambda s, b: (s, 0, 0))],
        out_specs=pl.BlockSpec((seq, LANES), lambda s, b: (b, s)),
        out_shape=jax.ShapeDtypeStruct((t, width), F32),
        scratch_shapes=[pltpu.VMEM((sub_w, sub_w), BF16), pltpu.VMEM((sub_w, st_w), BF16),
                        pltpu.VMEM((st_w, sub_w), BF16)],
        compiler_params=_params("arbitrary", "arbitrary"),
        name="s5",
    )(u, tz, gm, cc, pwd)


def _s5_param_pack(lam_re, lam_im, b_re, b_im, c_re, c_im, log_dt, nj):
    ls = S5_SUB
    dt = jnp.exp(log_dt)[:, None]
    zr, zi = lam_re * dt, lam_im * dt

    def powers(steps):
        st = steps[:, None, None]
        mag = jnp.exp(st * zr)
        return mag * jnp.cos(st * zi), mag * jnp.sin(st * zi)

    ab_re, ab_im = powers(jnp.ones((1,), F32))
    ab_re, ab_im = ab_re[0], ab_im[0]
    den = lam_re * lam_re + lam_im * lam_im
    nr, ni = ab_re - 1.0, ab_im
    f_re = (nr * lam_re + ni * lam_im) / den
    f_im = (ni * lam_re - nr * lam_im) / den
    bb_re = f_re[..., None] * b_re - f_im[..., None] * b_im
    bb_im = f_re[..., None] * b_im + f_im[..., None] * b_re
    pr, pi = powers(jnp.arange(0, ls + 1, dtype=F32))
    cl_re = c_re[None] * pr[:, :, None, :] - c_im[None] * pi[:, :, None, :]
    cl_im = c_re[None] * pi[:, :, None, :] + c_im[None] * pr[:, :, None, :]
    kern = (jnp.einsum('dgpn,gnq->dgpq', cl_re[:ls], bb_re, precision=HIGHEST)
            - jnp.einsum('dgpn,gnq->dgpq', cl_im[:ls], bb_im, precision=HIGHEST))
    lag = jnp.arange(ls)[None, :] - jnp.arange(ls)[:, None]
    tz = jnp.where((lag >= 0)[:, :, None, None, None], kern[jnp.clip(lag, 0, ls - 1)], 0.0)
    g = lam_re.shape[0]
    pch = S5_GROUP
    tz = tz.transpose(2, 0, 4, 1, 3).reshape(g, ls * pch, ls * pch)
    rr, ri = pr[ls - 1::-1][:ls], pi[ls - 1::-1][:ls]
    gm_re = rr[..., None] * bb_re[None] - ri[..., None] * bb_im[None]
    gm_im = rr[..., None] * bb_im[None] + ri[..., None] * bb_re[None]
    gm = jnp.concatenate([gm_re, gm_im], axis=2)
    gm = gm.transpose(1, 0, 3, 2).reshape(g, ls * pch, 2 * S5_STATE)
    cc = jnp.concatenate([cl_re[1:], -cl_im[1:]], axis=3)
    cc = cc.transpose(1, 3, 0, 2).reshape(g, 2 * S5_STATE, ls * pch)
    lv = []
    d = 1
    while d < nj:
        lv.append(float(ls * d))
        d *= 2
    qr, qi = powers(jnp.asarray(lv, F32))
    pw = jnp.stack([jnp.concatenate([qr, qr], axis=-1), jnp.concatenate([-qi, qi], axis=-1)], axis=1)
    pw = pw.reshape(2 * len(lv), g, 2 * S5_STATE).transpose(1, 0, 2)
    pw = jnp.concatenate([pw, jnp.zeros((g, 16 - 2 * len(lv), 2 * S5_STATE), F32)], axis=1)
    return tz.astype(BF16), gm.astype(BF16), cc.astype(BF16), pw


def _ssd_kernel(xbc_ref, misc_ref, zc_ref, cw_ref, vec_ref, nw_ref, out_ref,
                st_ref, halo_ref, buf_ref, *, L):
    c = pl.program_id(1)
    width = M2_HEADS * HEAD_DIM
    gs = M2_STATE

    @pl.when(c == 0)
    def _():
        st_ref[...] = jnp.zeros_like(st_ref)
        halo_ref[...] = jnp.zeros_like(halo_ref)

    xbc = xbc_ref[...]
    buf_ref[0:SUBLANES, :] = halo_ref[...]
    buf_ref[SUBLANES:SUBLANES + L, :] = xbc
    halo_ref[...] = xbc[L - SUBLANES:L, :]
    cw = cw_ref[...]
    acc = cw[M2_CONV:M2_CONV + 1, :]
    for tap in range(M2_CONV):
        s0 = SUBLANES - (M2_CONV - 1) + tap
        acc = acc + cw[tap:tap + 1, :] * buf_ref[s0:s0 + L, :]
    xc = _silu(acc)
    xs = xc[:, :width]

    vec = vec_ref[...]
    dt = _softplus(misc_ref[:, 2 * LANES:3 * LANES] + vec[0:1, :])
    a_dt = dt * vec[1:2, :]
    il = lax.broadcasted_iota(jnp.int32, (L, 3 * L), 0)
    jl = lax.broadcasted_iota(jnp.int32, (L, 3 * L), 1) & (L - 1)
    tri3 = jnp.where(jl <= il, 1.0, 0.0).astype(BF16)
    cum = _dot(tri3, jnp.concatenate(_split3(a_dt), axis=0))
    cum_t = cum.T

    eh = lax.broadcasted_iota(jnp.int32, (3 * LANES, width), 0) & (LANES - 1)
    ec = lax.broadcasted_iota(jnp.int32, (3 * LANES, width), 1)
    expand3 = jnp.where((ec >> 6) == eh, 1.0, 0.0).astype(BF16)
    dt_x = _dot(jnp.concatenate(_split3(dt), axis=1), expand3)
    cum_x = _dot(jnp.concatenate(_split3(cum), axis=1), expand3)
    cum_lx = cum_x[L - 1:L, :]
    xdt = xs * dt_x
    xdd = (xdt * jnp.exp(cum_lx - cum_x)).astype(BF16)
    xdt_b = xdt.astype(BF16)
    ecum = jnp.exp(cum_x)
    e_tot = jnp.exp(cum_lx)

    tril = (lax.broadcasted_iota(jnp.int32, (L, L), 1) <= lax.broadcasted_iota(jnp.int32, (L, L), 0))
    lane = lax.broadcasted_iota(jnp.int32, (1, LANES), 1)
    m0 = lane < HEAD_DIM
    hpg = M2_HEADS // M2_GROUPS
    gw = hpg * HEAD_DIM
    groups = range(M2_GROUPS)
    heads = range(M2_HEADS)
    bm_f = [xc[:, width + g * gs:width + (g + 1) * gs] for g in groups]
    bm = [b.astype(BF16) for b in bm_f]
    cm = [xc[:, width + (M2_GROUPS + g) * gs:width + (M2_GROUPS + g + 1) * gs].astype(BF16) for g in groups]
    cbm = [jnp.where(tril, _dot_nt(cm[g], bm[g]), 0.0) for g in groups]
    st = [st_ref[:, g * gw:(g + 1) * gw] for g in groups]
    y_off = [_dot(cm[g], st[g].astype(BF16)) for g in groups]
    for g in groups:
        st_ref[:, g * gw:(g + 1) * gw] = (st[g] * e_tot[:, g * gw:(g + 1) * gw]
                                          + _dot(bm_f[g].T.astype(BF16), xdd[:, g * gw:(g + 1) * gw]))
    mh = [(cbm[h // hpg] * jnp.exp(jnp.minimum(cum[:, h:h + 1] - cum_t[h:h + 1, :], 0.0))).astype(BF16)
          for h in heads]
    zero = jnp.zeros((), BF16)
    yd = [_dot(mh[h], jnp.where(m0 if h % 2 == 0 else jnp.logical_not(m0),
                                xdt_b[:, (h // 2) * LANES:(h // 2 + 1) * LANES], zero)) for h in heads]
    y_diag = jnp.concatenate([yd[2 * p] + yd[2 * p + 1] for p in range(M2_HEADS // 2)], axis=1)
    y = y_diag + jnp.concatenate(y_off, axis=1) * ecum + xs * nw_ref[1:2, :]
    y = y * _silu(zc_ref[...])
    gw = width // M2_GROUPS
    outs = []
    for g in range(M2_GROUPS):
        yg = y[:, g * gw:(g + 1) * gw]
        outs.append(_rms(yg, nw_ref[0:1, g * gw:(g + 1) * gw]))
    out_ref[...] = jnp.concatenate(outs, axis=1)


def _ssd(xbc, misc, z, cw, vec, nw, batch, seq):
    L = M2_CHUNK
    t, cd = xbc.shape
    nc = seq // L
    width = M2_HEADS * HEAD_DIM
    return pl.pallas_call(
        functools.partial(_ssd_kernel, L=L),
        grid=(batch, nc),
        in_specs=[pl.BlockSpec((L, cd), lambda b, c: (b * nc + c, 0)),
                  pl.BlockSpec((L, misc.shape[1]), lambda b, c: (b * nc + c, 0)),
                  pl.BlockSpec((L, width), lambda b, c: (b * nc + c, 0)),
                  pl.BlockSpec((SUBLANES, cd), lambda b, c: (0, 0)),
                  pl.BlockSpec((SUBLANES, LANES), lambda b, c: (0, 0)),
                  pl.BlockSpec((SUBLANES, width), lambda b, c: (0, 0))],
        out_specs=pl.BlockSpec((L, width), lambda b, c: (b * nc + c, 0)),
        out_shape=jax.ShapeDtypeStruct((t, width), F32),
        scratch_shapes=[pltpu.VMEM((M2_STATE, width), F32), pltpu.VMEM((SUBLANES, cd), F32),
                        pltpu.VMEM((L + SUBLANES, cd), F32)],
        compiler_params=_params("arbitrary", "arbitrary"),
        name="ssd",
    )(xbc, misc, z, cw, vec, nw)


def _mla_prep_kernel(cq_ref, ckv_ref, misc_ref, cos_ref, sin_ref, qnw_ref, kvnw_ref, wq_ref, wkv_ref,
                     q_ref, k_ref, v_ref):
    nh = MLA_HEADS
    cos, sin = cos_ref[...], sin_ref[...]
    q = _dot(_rms(cq_ref[...], qnw_ref[...]).astype(BF16), wq_ref[...])
    scale = (MLA_NOPE + MLA_ROPE) ** -0.5 * math.log2(math.e)
    rope0 = nh * MLA_NOPE
    rot0 = rope0 + nh * MLA_ROPE
    for pair in range(nh // 2):
        sl = slice(pair * LANES, (pair + 1) * LANES)
        qr = (q[:, rope0:rot0][:, sl] * cos + q[:, rot0:][:, sl] * sin) * scale
        for hh in range(2):
            h = 2 * pair + hh
            q_ref[:, h * 2 * LANES:h * 2 * LANES + LANES] = (
                q[:, h * MLA_NOPE:(h + 1) * MLA_NOPE] * scale).astype(q_ref.dtype)
            q_ref[:, h * 2 * LANES + LANES:(h + 1) * 2 * LANES] = qr.astype(q_ref.dtype)
    kv = _dot(_rms(ckv_ref[...], kvnw_ref[...]).astype(BF16), wkv_ref[...])
    kpe = misc_ref[:, :LANES] * cos + misc_ref[:, LANES:2 * LANES] * sin
    lane = lax.broadcasted_iota(jnp.int32, (1, LANES), 1)
    slots = [jnp.where(lane < MLA_ROPE, kpe, 0.0), jnp.where(lane < MLA_ROPE, 0.0, kpe)]
    for h in range(nh):
        k_ref[:, h * 2 * LANES:h * 2 * LANES + LANES] = kv[:, h * MLA_NOPE:(h + 1) * MLA_NOPE].astype(k_ref.dtype)
        k_ref[:, h * 2 * LANES + LANES:(h + 1) * 2 * LANES] = slots[h % 2].astype(k_ref.dtype)
    v_ref[...] = kv[:, nh * MLA_NOPE:].astype(v_ref.dtype)


def _mla_prep(cq, ckv, misc, cos, sin, qnw, kvnw, wq, wkv, tm=256):
    t = cq.shape[0]
    nh = MLA_HEADS
    row = lambda n: pl.BlockSpec((tm, n), lambda i: (i, 0))
    full = lambda a: pl.BlockSpec(a.shape, lambda i: (0, 0))
    qnw, kvnw = qnw.reshape(1, -1), kvnw.reshape(1, -1)
    return pl.pallas_call(
        _mla_prep_kernel,
        grid=(t // tm,),
        in_specs=[row(cq.shape[1]), row(ckv.shape[1]), row(misc.shape[1]), row(LANES), row(LANES),
                  full(qnw), full(kvnw), full(wq), full(wkv)],
        out_specs=[row(nh * 2 * LANES), row(nh * 2 * LANES), row(nh * MLA_V)],
        out_shape=[jax.ShapeDtypeStruct((t, nh * 2 * LANES), BF16),
                   jax.ShapeDtypeStruct((t, nh * 2 * LANES), BF16),
                   jax.ShapeDtypeStruct((t, nh * MLA_V), BF16)],
        compiler_params=_params("arbitrary"),
        name="mla_prep",
    )(cq, ckv, misc, cos, sin, qnw, kvnw, wq, wkv)


def _flash_kernel(q_ref, k_ref, v_ref, o_ref, *, tq, hb):
    qi = pl.program_id(2)
    qw, vw = 2 * LANES, MLA_V
    qs = [q_ref[:, h * qw:(h + 1) * qw] for h in range(hb)]

    def update(h, carry, kb, vb, mask):
        m_prev, l_prev, acc = carry
        s = _dot_nt(qs[h], kb)
        if mask is not None:
            s = jnp.where(mask, s, -jnp.inf)
        m_new = jnp.maximum(m_prev, jnp.max(s, axis=-1, keepdims=True))
        alpha = jnp.exp2(m_prev - m_new)
        p = jnp.exp2(s - m_new)
        l_new = alpha * l_prev + jnp.sum(p, axis=-1, keepdims=True)
        return m_new, l_new, alpha * acc + _dot(p.astype(BF16), vb)

    def block(j, carries, mask):
        off = pl.multiple_of(j * tq, tq)
        return tuple(update(h, carries[h], k_ref[pl.ds(off, tq), h * qw:(h + 1) * qw],
                            v_ref[pl.ds(off, tq), h * vw:(h + 1) * vw], mask) for h in range(hb))

    init = tuple((jnp.full((tq, 1), -jnp.inf, F32), jnp.zeros((tq, 1), F32), jnp.zeros((tq, vw), F32))
                 for _ in range(hb))
    carries = lax.fori_loop(0, qi, lambda j, c: block(j, c, None), init)
    ri = lax.broadcasted_iota(jnp.int32, (tq, tq), 0)
    ci = lax.broadcasted_iota(jnp.int32, (tq, tq), 1)
    carries = block(qi, carries, ci <= ri)
    for h in range(hb):
        _, l_fin, acc = carries[h]
        o_ref[:, h * vw:(h + 1) * vw] = (acc / l_fin).astype(o_ref.dtype)


def _flash(q, k, v, batch, seq, tq=512, hb=2):
    t = q.shape[0]
    tq = min(tq, seq)
    nq = seq // tq
    ng = MLA_HEADS // hb
    return pl.pallas_call(
        functools.partial(_flash_kernel, tq=tq, hb=hb),
        grid=(batch, ng, nq),
        in_specs=[pl.BlockSpec((tq, hb * 2 * LANES), lambda b, g, i: (b * nq + i, g)),
                  pl.BlockSpec((seq, hb * 2 * LANES), lambda b, g, i: (b, g)),
                  pl.BlockSpec((seq, hb * MLA_V), lambda b, g, i: (b, g))],
        out_specs=pl.BlockSpec((tq, hb * MLA_V), lambda b, g, i: (b * nq + i, g)),
        out_shape=jax.ShapeDtypeStruct((t, MLA_HEADS * MLA_V), F32),
        compiler_params=_params("arbitrary", "arbitrary", "arbitrary"),
        name="mla_flash",
    )(q, k, v)


def _finish(h_ref, acc, fnw_ref, o_ref, final):
    hn = h_ref[...] + acc
    o_ref[...] = _rms(hn, fnw_ref[...]) if final else hn


def _even_tail_kernel(a_ref, ys_ref, u_ref, q_ref, z_ref, h_ref, mk_ref, mv_ref, vec_ref, gw_ref, ow_ref,
                      fnw_ref, o_ref, *, final):
    w = RW_WIDTH
    z = z_ref[...]
    ga = (a_ref[...] * _silu(z[:, :w])).astype(BF16)
    yb = _gelu_tanh(ys_ref[...] + vec_ref[0:1, :] * u_ref[...])
    gate = _sigmoid(_dot(yb.astype(BF16), gw_ref[...]) + vec_ref[1:2, :])
    gb = (yb * gate * _silu(z[:, w:2 * w])).astype(BF16)
    gm = (_mem_attend(q_ref[...], mk_ref[0], mv_ref[0]) * _silu(z[:, 2 * w:])).astype(BF16)
    acc = _dot(ga, ow_ref[0:w, :]) + _dot(gb, ow_ref[w:2 * w, :]) + _dot(gm, ow_ref[2 * w:, :])
    _finish(h_ref, acc, fnw_ref, o_ref, final)


def _odd_tail_kernel(c_ref, d_ref, q_ref, z_ref, h_ref, mk_ref, mv_ref, ow_ref, fnw_ref, o_ref, *, final):
    w = D_MODEL
    z = z_ref[...]
    gc = c_ref[...].astype(BF16)
    gd = (d_ref[...] * _silu(z[:, :w])).astype(BF16)
    gm = (_mem_attend(q_ref[...], mk_ref[0], mv_ref[0]) * _silu(z[:, w:])).astype(BF16)
    acc = _dot(gc, ow_ref[0:w, :]) + _dot(gd, ow_ref[w:2 * w, :]) + _dot(gm, ow_ref[2 * w:, :])
    _finish(h_ref, acc, fnw_ref, o_ref, final)


def _tail_call(kernel, rows, consts, h, mk, mv, fnw, batch, seq, final, name, tm=256):
    t, d = h.shape
    nt = seq // tm
    row = lambda a: pl.BlockSpec((tm, a.shape[1]), lambda b, i: (b * nt + i, 0))
    full = lambda a: pl.BlockSpec(a.shape, lambda b, i: (0,) * a.ndim)
    memspec = pl.BlockSpec((1,) + mk.shape[1:], lambda b, i: (b, 0, 0))
    fnw = fnw.reshape(1, d)
    return pl.pallas_call(
        functools.partial(kernel, final=final),
        grid=(batch, nt),
        in_specs=[row(a) for a in rows] + [row(h), memspec, memspec] + [full(a) for a in consts] + [full(fnw)],
        out_specs=row(h),
        out_shape=jax.ShapeDtypeStruct((t, d), F32),
        compiler_params=_params("arbitrary", "arbitrary"),
        name=name,
    )(*rows, h, mk, mv, *consts, fnw)


def _pad_rows(rows, n=SUBLANES):
    width = rows[0].shape[-1]
    return jnp.concatenate([r.reshape(1, width) for r in rows] + [jnp.zeros((n - len(rows), width), F32)], axis=0)


def _even_layer(h, batch, seq, mk, mv, nw, in_w, out_w, fnw, final, mu, w0, w2, a0, a2, k_k, k_a, r_k,
                ln_w, ln_b, lam_re, lam_im, b_re, b_im, c_re, c_im, s5_d, log_dt, glu_w, glu_b):
    rw_proj = 3 * RW_WIDTH + 2 * RW_LORA
    ev_width = 2 * RW_WIDTH + MEM_WIDTH
    splits = (rw_proj, RW_WIDTH, MEM_WIDTH, ev_width)
    p, u, q_mem, z = _norm_proj(h, nw, in_w.astype(BF16), splits, (F32, F32, F32, F32))
    rvec, wa = _rwkv_param_pack(w0, w2, a0, a2, k_k, k_a, r_k, ln_w, ln_b)
    a_out = _rwkv(p, mu, rvec, wa, batch, seq)

    nj = seq // S5_SUB
    groups = RW_WIDTH // S5_GROUP
    tz, gm, cc, pw = _s5_param_pack(lam_re, lam_im, b_re, b_im, c_re, c_im, log_dt, nj)
    ys = _s5(u, tz, gm, cc, pw, batch, seq)

    vec = _pad_rows([s5_d, glu_b])
    return _tail_call(_even_tail_kernel, [a_out, ys, u, q_mem, z], [vec, glu_w.astype(BF16), out_w.astype(BF16)],
                      h, mk, mv, fnw, batch, seq, final, "even_tail")


def _odd_layer(h, batch, seq, mk, mv, cos, sin, nw, in_w, out_w, fnw, final, conv_w, conv_b, dt_bias, a_log,
               m2_d, m2_norm_w, q_norm_w, wq_up, kv_norm_w, wkv_up):
    width = M2_HEADS * HEAD_DIM
    conv_dim = width + 2 * M2_GROUPS * M2_STATE
    nh = MLA_HEADS
    o_dt = conv_dim
    o_cq = o_dt + M2_HEADS
    o_ckv = o_cq + MLA_Q_RANK
    o_kr = o_ckv + MLA_KV_RANK
    o_qm = o_kr + MLA_ROPE
    o_z = o_qm + MEM_WIDTH
    half = MLA_ROPE // 2
    w_kr = in_w[:, o_kr:o_qm]
    w_kr_sw = jnp.concatenate([w_kr[:, half:], w_kr[:, :half]], axis=1)
    w_misc = jnp.concatenate([w_kr, w_kr, w_kr_sw, w_kr_sw, in_w[:, o_dt:o_cq],
                              jnp.zeros((in_w.shape[0], LANES - M2_HEADS), in_w.dtype)], axis=1)
    w_all = jnp.concatenate([in_w[:, :o_dt], in_w[:, o_cq:o_ckv], in_w[:, o_ckv:o_kr], in_w[:, o_qm:o_z],
                             in_w[:, o_z:], w_misc], axis=1).astype(BF16)
    splits = (conv_dim, MLA_Q_RANK, MLA_KV_RANK, MEM_WIDTH, width, D_MODEL + MEM_WIDTH, 3 * LANES)
    xbc, cq, ckv, q_mem, z_c, z_dm, misc = _norm_proj(h, nw, w_all, splits, (F32,) * 7)

    cw = _pad_rows(list(conv_w) + [conv_b])
    pad16 = lambda x: jnp.concatenate([x, jnp.zeros((LANES - M2_HEADS,), F32)])
    vec = _pad_rows([pad16(dt_bias), pad16(-jnp.exp(a_log))])
    nwd = _pad_rows([m2_norm_w, jnp.repeat(m2_d, HEAD_DIM)])
    c_out = _ssd(xbc, misc, z_c, cw, vec, nwd, batch, seq)

    wq = wq_up.reshape(MLA_Q_RANK, nh, MLA_NOPE + MLA_ROPE)
    wq_r = wq[:, :, MLA_NOPE:]
    wq_sw = jnp.concatenate([wq_r[:, :, half:], wq_r[:, :, :half]], axis=2)
    wq_all = jnp.concatenate([wq[:, :, :MLA_NOPE].reshape(MLA_Q_RANK, -1), wq_r.reshape(MLA_Q_RANK, -1),
                              wq_sw.reshape(MLA_Q_RANK, -1)], axis=1).astype(BF16)
    wkv = wkv_up.reshape(MLA_KV_RANK, nh, MLA_NOPE + MLA_V)
    wkv_all = jnp.concatenate([wkv[:, :, :MLA_NOPE].reshape(MLA_KV_RANK, -1),
                               wkv[:, :, MLA_NOPE:].reshape(MLA_KV_RANK, -1)], axis=1).astype(BF16)
    qc, kc, vv = _mla_prep(cq, ckv, misc, cos, sin, q_norm_w, kv_norm_w, wq_all, wkv_all)
    d_attn = _flash(qc, kc, vv, batch, seq)

    return _tail_call(_odd_tail_kernel, [c_out, d_attn, q_mem, z_dm], [out_w.astype(BF16)],
                      h, mk, mv, fnw, batch, seq, final, "odd_tail")


def _rope_tables(positions):
    inv = 1.0 / (ROPE_THETA ** (jnp.arange(0, MLA_ROPE, 2, dtype=F32) / MLA_ROPE))
    ang = positions.astype(F32).reshape(-1, 1) * inv
    cos, sin = jnp.cos(ang), jnp.sin(ang)
    return jnp.tile(cos, (1, 4)), jnp.tile(jnp.concatenate([-sin, sin], axis=1), (1, 2))


def kernel(x, mem, positions, norm_w, mem_norm_w, final_norm_w, mem_kv_w, ev_in_w, ev_out_w, rw_mu, rw_w0, rw_w2, rw_a0, rw_a2, rw_k_k, rw_k_a, rw_r_k, rw_ln_w, rw_ln_b, s5_lambda_re, s5_lambda_im, s5_b_re, s5_b_im, s5_c_re, s5_c_im, s5_d, s5_log_dt, s5_glu_w, s5_glu_b, od_in_w, od_out_w, m2_conv_w, m2_conv_b, m2_dt_bias, m2_a_log, m2_d, m2_norm_w, mla_q_norm_w, mla_wq_up, mla_kv_norm_w, mla_wkv_up):
    batch, seq, d = x.shape
    depth = norm_w.shape[0]
    mk, mv = _mem_kv(mem, mem_norm_w, mem_kv_w.astype(BF16))
    cos, sin = _rope_tables(positions)
    h = x.reshape(batch * seq, d)
    for layer in range(depth):
        i = layer // 2
        final = layer == depth - 1
        if layer % 2 == 0:
            h = _even_layer(h, batch, seq, mk[layer], mv[layer], norm_w[layer], ev_in_w[i], ev_out_w[i],
                            final_norm_w, final, rw_mu[i], rw_w0[i], rw_w2[i], rw_a0[i], rw_a2[i], rw_k_k[i],
                            rw_k_a[i], rw_r_k[i], rw_ln_w[i], rw_ln_b[i], s5_lambda_re[i], s5_lambda_im[i],
                            s5_b_re[i], s5_b_im[i], s5_c_re[i], s5_c_im[i], s5_d[i], s5_log_dt[i],
                            s5_glu_w[i], s5_glu_b[i])
        else:
            h = _odd_layer(h, batch, seq, mk[layer], mv[layer], cos, sin, norm_w[layer], od_in_w[i], od_out_w[i],
                           final_norm_w, final, m2_conv_w[i], m2_conv_b[i], m2_dt_bias[i], m2_a_log[i], m2_d[i],
                           m2_norm_w[i], mla_q_norm_w[i], mla_wq_up[i], mla_kv_norm_w[i], mla_wkv_up[i])
    return h.reshape(batch, seq, d)
```

```python
import functools
import math

import jax
import jax.numpy as jnp
from jax import lax
from jax.experimental import pallas as pl
from jax.experimental.pallas import tpu as pltpu

F32 = jnp.float32
BF16 = jnp.bfloat16
HIGHEST = lax.Precision.HIGHEST

LANES = 128
SUBLANES = 8
VMEM_LIMIT_BYTES = 56 * 1024 * 1024

D_MODEL = 1024
HEAD_DIM = 64
NORM_EPS = 1e-6
RW_WIDTH = 1024
RW_LORA = 64
RW_GN_EPS = 64e-5
RW_CHUNK = 64
S5_GROUP = 16
S5_STATE = 64
S5_SUB = 16
M2_HEADS = 16
M2_GROUPS = 2
M2_STATE = 128
M2_CONV = 4
M2_CHUNK = 128
MLA_HEADS = 8
MLA_NOPE = 128
MLA_ROPE = 64
MLA_V = 128
MLA_Q_RANK = 384
MLA_KV_RANK = 256
ROPE_THETA = 10000.0
MEM_HEADS = 4
MEM_WIDTH = MEM_HEADS * HEAD_DIM


def _dot(a, b, precision=None):
    return jnp.dot(a, b, preferred_element_type=F32, precision=precision)


def _dot_nt(a, b, precision=None):
    return lax.dot_general(a, b, (((1,), (1,)), ((), ())), preferred_element_type=F32,
                           precision=precision)


def _sigmoid(x):
    return 1.0 / (1.0 + jnp.exp(-x))


def _silu(x):
    return x * _sigmoid(x)


def _softplus(x):
    return jnp.maximum(x, 0.0) + jnp.log(1.0 + jnp.exp(-jnp.abs(x)))


def _gelu_tanh(x):
    return 0.5 * x * (1.0 + jnp.tanh(math.sqrt(2.0 / math.pi) * (x + 0.044715 * (x * x * x))))


def _rms(x, w, eps=NORM_EPS):
    ms = jnp.mean(x * x, axis=-1, keepdims=True)
    return x * lax.rsqrt(ms + eps) * w


def _params(*sem):
    return pltpu.CompilerParams(dimension_semantics=sem, vmem_limit_bytes=VMEM_LIMIT_BYTES)


def _norm_proj_kernel(x_ref, nw_ref, w_ref, *out_refs, splits):
    xn = _rms(x_ref[...], nw_ref[...]).astype(BF16)
    off = 0
    for o_ref, n in zip(out_refs, splits):
        o_ref[...] = _dot(xn, w_ref[:, off:off + n]).astype(o_ref.dtype)
        off += n


def _norm_proj(x, nw, w, splits, dtypes, tm=256):
    t, d = x.shape
    n = w.shape[1]
    assert sum(splits) == n and t % tm == 0
    return pl.pallas_call(
        functools.partial(_norm_proj_kernel, splits=splits),
        grid=(t // tm,),
        in_specs=[pl.BlockSpec((tm, d), lambda i: (i, 0)),
                  pl.BlockSpec((1, d), lambda i: (0, 0)),
                  pl.BlockSpec((d, n), lambda i: (0, 0))],
        out_specs=[pl.BlockSpec((tm, s), lambda i: (i, 0)) for s in splits],
        out_shape=[jax.ShapeDtypeStruct((t, s), dt) for s, dt in zip(splits, dtypes)],
        compiler_params=_params("arbitrary"),
        name="norm_proj",
    )(x, nw.reshape(1, d), w)


def _mem_kv_kernel(mem_ref, nw_ref, w_ref, k_ref, v_ref):
    mn = _rms(mem_ref[0], nw_ref[...]).astype(BF16)
    kv = _dot(mn, w_ref[0])
    k_ref[0, 0] = kv[:, :MEM_WIDTH].astype(k_ref.dtype)
    v_ref[0, 0] = kv[:, MEM_WIDTH:].astype(v_ref.dtype)


def _mem_kv(mem, nw, w):
    b, m, d = mem.shape
    depth = w.shape[0]
    shp = jax.ShapeDtypeStruct((depth, b, m, MEM_WIDTH), BF16)
    return pl.pallas_call(
        _mem_kv_kernel,
        grid=(depth, b),
        in_specs=[pl.BlockSpec((1, m, d), lambda l, i: (i, 0, 0)),
                  pl.BlockSpec((1, d), lambda l, i: (0, 0)),
                  pl.BlockSpec((1, d, 2 * MEM_WIDTH), lambda l, i: (l, 0, 0))],
        out_specs=[pl.BlockSpec((1, 1, m, MEM_WIDTH), lambda l, i: (l, i, 0, 0))] * 2,
        out_shape=[shp, shp],
        compiler_params=_params("arbitrary", "arbitrary"),
        name="mem_kv",
    )(mem, nw.reshape(1, d), w)


def _mem_attend(q, k, v):
    lane = lax.broadcasted_iota(jnp.int32, (1, MEM_WIDTH), 1)
    qb = (q * (HEAD_DIM ** -0.5)).astype(BF16)
    out = jnp.zeros(q.shape, F32)
    for h in range(MEM_HEADS):
        hm = (lane >> 6) == h
        sc = _dot_nt(qb, jnp.where(hm, k, jnp.zeros_like(k)))
        sc = sc - jnp.max(sc, axis=-1, keepdims=True)
        p = jnp.exp(sc)
        p = p / jnp.sum(p, axis=-1, keepdims=True)
        out = out + _dot(p.astype(BF16), jnp.where(hm, v, jnp.zeros_like(v)))
    return out


def _split3(x):
    hi = x.astype(BF16)
    r1 = x - hi.astype(F32)
    mid = r1.astype(BF16)
    lo = (r1 - mid.astype(F32)).astype(BF16)
    return hi, mid, lo


def _rwkv_kernel(p_ref, mu_ref, vec_ref, wa_ref, out_ref, st_ref, prev_ref, *, L):
    c = pl.program_id(1)
    W = RW_WIDTH

    @pl.when(c == 0)
    def _():
        st_ref[...] = jnp.zeros_like(st_ref)
        prev_ref[...] = jnp.zeros_like(prev_ref)

    x = p_ref[...]
    row_l = lax.broadcasted_iota(jnp.int32, (L, 1), 0)
    xs = jnp.where(row_l == 0, prev_ref[0:1, :], pltpu.roll(x, 1, axis=0))
    prev_ref[0:1, :] = x[L - 1:L, :]
    xm = x + (xs - x) * mu_ref[...]
    r, k, v, lat = xm[:, :W], xm[:, W:2 * W], xm[:, 2 * W:3 * W], xm[:, 3 * W:]

    w0, a0, k_k, k_a, r_k, ln_w, ln_b = [vec_ref[i:i + 1, :] for i in range(7)]
    lane = lax.broadcasted_iota(jnp.int32, (1, LANES), 1)
    lat_t = jnp.where(lane < RW_LORA, jnp.tanh(lat), lat).astype(BF16)
    la = _dot(lat_t, wa_ref[...])
    log_w = -_softplus(-(w0 + la[:, :W])) - 0.5
    lw = -jnp.exp(log_w)
    iclr = _sigmoid(a0 + la[:, W:])
    kk0 = k * k_k
    kp = k * (1.0 + (iclr - 1.0) * k_a)
    rkr = r * kp * r_k

    il = lax.broadcasted_iota(jnp.int32, (L, 3 * L), 0)
    jl = lax.broadcasted_iota(jnp.int32, (L, 3 * L), 1) & (L - 1)
    tri3 = jnp.where(jl <= il, 1.0, 0.0).astype(BF16)
    cum = _dot(tri3, jnp.concatenate(_split3(lw), axis=0))
    cum_l = cum[L - 1:L, :]
    e_neg = jnp.exp(-cum)
    e_exc = jnp.exp(cum - lw)
    e_inc = jnp.exp(cum)
    dend = jnp.exp(cum_l - cum)
    e_tot = jnp.exp(cum_l)

    i2 = lax.broadcasted_iota(jnp.int32, (LANES, LANES), 0)
    j2 = lax.broadcasted_iota(jnp.int32, (LANES, LANES), 1)
    same_head = (i2 >> 6) == (j2 >> 6)
    head_ones = jnp.where(same_head, 1.0, 0.0).astype(BF16)
    strict = same_head & (j2 < i2)
    incl = same_head & (j2 <= i2)
    eye = jnp.where(i2 == j2, 1.0, 0.0)
    levels = [((i2 & -(2 * s)) == (j2 & -(2 * s))) & ((i2 & s) != 0) & ((j2 & s) == 0)
              for s in (1, 2, 4, 8, 16, 32)]
    m0 = lane < HEAD_DIM
    stack = lambda t: jnp.concatenate([jnp.where(m0, t, 0.0), jnp.where(m0, 0.0, t)], axis=0)

    pairs = range(W // LANES)
    sls = [slice(p * LANES, (p + 1) * LANES) for p in pairs]
    each = lambda f: [f(p) for p in pairs]
    sums = each(lambda p: _dot(jnp.concatenate([kk0[:, sls[p]] * kk0[:, sls[p]], rkr[:, sls[p]]],
                                               axis=0).astype(BF16), head_ones))
    kk = each(lambda p: kk0[:, sls[p]] * lax.rsqrt(jnp.maximum(sums[p][:L], 1e-24)))
    bvec = each(lambda p: kk[p] * iclr[:, sls[p]])
    ar = each(lambda p: jnp.concatenate([stack(-kk[p] * e_exc[:, sls[p]]), stack(r[:, sls[p]] * e_inc[:, sls[p]])],
                                        axis=0).astype(BF16))
    b2 = each(lambda p: stack(bvec[p] * e_neg[:, sls[p]]).astype(BF16))
    k2 = each(lambda p: stack(kp[:, sls[p]] * e_neg[:, sls[p]]).astype(BF16))
    v2f = each(lambda p: stack(v[:, sls[p]]))
    v2 = each(lambda p: v2f[p].astype(BF16))
    bk_end = each(lambda p: jnp.concatenate([stack(bvec[p] * dend[:, sls[p]]), stack(kp[:, sls[p]] * dend[:, sls[p]])],
                                            axis=0).astype(BF16))
    sb = each(lambda p: _dot_nt(ar[p], b2[p]))
    sk = each(lambda p: _dot_nt(ar[p], k2[p]))
    aab = each(lambda p: jnp.where(strict, sb[p][:LANES], 0.0))

    inv = each(lambda p: eye + jnp.where(levels[0], aab[p], 0.0))
    for lvl in levels[1:]:
        invb = each(lambda p: inv[p].astype(BF16))
        t1 = each(lambda p: _dot(invb[p], jnp.where(lvl, aab[p], 0.0).astype(BF16)).astype(BF16))
        inv = each(lambda p: inv[p] + _dot(t1[p], invb[p]))

    st = each(lambda p: st_ref[p])
    sta = each(lambda p: _dot_nt(ar[p], st[p].astype(BF16)))
    rhs = each(lambda p: sta[p][:LANES] + _dot(jnp.where(strict, sk[p][:LANES], 0.0).astype(BF16), v2[p]))
    u2 = each(lambda p: _dot(inv[p].astype(BF16), rhs[p].astype(BF16)))
    y2 = each(lambda p: sta[p][LANES:] + _dot(jnp.where(incl, sb[p][LANES:], 0.0).astype(BF16), u2[p].astype(BF16))
              + _dot(jnp.where(incl, sk[p][LANES:], 0.0).astype(BF16), v2[p]))
    y = each(lambda p: y2[p][:L, :] + y2[p][L:, :])
    for p in pairs:
        uv_t = jnp.concatenate([u2[p].T, v2f[p].T], axis=1).astype(BF16)
        st_ref[p] = st[p] * e_tot[:, sls[p]] + _dot(uv_t, bk_end[p])

    mean = each(lambda p: _dot(y[p].astype(BF16), head_ones) * (1.0 / HEAD_DIM))
    dcen = each(lambda p: y[p] - mean[p])
    var = each(lambda p: _dot((dcen[p] * dcen[p]).astype(BF16), head_ones) * (1.0 / HEAD_DIM))
    for p in pairs:
        out_ref[:, sls[p]] = (dcen[p] * lax.rsqrt(var[p] + RW_GN_EPS) * ln_w[:, sls[p]] + ln_b[:, sls[p]]
                              + sums[p][L:] * v[:, sls[p]])


def _rwkv(p, mu, vec, wa, batch, seq):
    L = RW_CHUNK
    t, pw = p.shape
    nc = seq // L
    return pl.pallas_call(
        functools.partial(_rwkv_kernel, L=L),
        grid=(batch, nc),
        in_specs=[pl.BlockSpec((L, pw), lambda b, c: (b * nc + c, 0)),
                  pl.BlockSpec((1, pw), lambda b, c: (0, 0)),
                  pl.BlockSpec(vec.shape, lambda b, c: (0, 0)),
                  pl.BlockSpec(wa.shape, lambda b, c: (0, 0))],
        out_specs=pl.BlockSpec((L, RW_WIDTH), lambda b, c: (b * nc + c, 0)),
        out_shape=jax.ShapeDtypeStruct((t, RW_WIDTH), F32),
        scratch_shapes=[pltpu.VMEM((RW_WIDTH // LANES, LANES, LANES), F32), pltpu.VMEM((SUBLANES, pw), F32)],
        compiler_params=_params("arbitrary", "arbitrary"),
        name="rwkv7",
    )(p, mu.reshape(1, pw), vec, wa)


def _rwkv_param_pack(w0, w2, a0, a2, k_k, k_a, r_k, ln_w, ln_b):
    vec = _pad_rows([w0, a0, k_k, k_a, r_k.reshape(-1), ln_w, ln_b])
    z = jnp.zeros_like(w2)
    wa = jnp.concatenate([jnp.concatenate([w2, z], axis=1), jnp.concatenate([z, a2], axis=1)], axis=0)
    return vec, wa.astype(BF16)


def _s5_kernel(u_ref, tz_ref, gm_ref, cc_ref, pw_ref, y_ref, tzd_ref, gmd_ref, ccd_ref, *, nj):
    ls, pch, n = S5_SUB, S5_GROUP, S5_STATE
    gs = LANES // pch

    @pl.when(pl.program_id(1) == 0)
    def _():
        tzd_ref[...] = jnp.zeros_like(tzd_ref)
        gmd_ref[...] = jnp.zeros_like(gmd_ref)
        ccd_ref[...] = jnp.zeros_like(ccd_ref)
        for g in range(gs):
            for a in range(ls):
                r0 = a * LANES + g * pch
                for b in range(a, ls):
                    c0 = b * LANES + g * pch
                    tzd_ref[r0:r0 + pch, c0:c0 + pch] = tz_ref[g, a * pch:(a + 1) * pch, b * pch:(b + 1) * pch]
                for c in range(2):
                    gmd_ref[r0:r0 + pch, (c * gs + g) * n:(c * gs + g + 1) * n] = (
                        gm_ref[g, a * pch:(a + 1) * pch, c * n:(c + 1) * n])
                    ccd_ref[(c * gs + g) * n:(c * gs + g + 1) * n, r0:r0 + pch] = (
                        cc_ref[g, c * n:(c + 1) * n, a * pch:(a + 1) * pch])

    rows = u_ref.shape[0] // ls
    xcat = jnp.concatenate([u_ref[pl.ds(l, rows, stride=ls), :] for l in range(ls)], axis=1).astype(BF16)
    y = _dot(xcat, tzd_ref[...])
    x = _dot(xcat, gmd_ref[...])
    half = x.shape[1] // 2
    jrow = lax.broadcasted_iota(jnp.int32, (rows, 1), 0) & (nj - 1)
    pw = pw_ref[0]
    d, lvl = 1, 0
    while d < nj:
        xs = jnp.where(jrow >= d, pltpu.roll(x, d, axis=0), 0.0)
        xsw = jnp.concatenate([xs[:, half:], xs[:, :half]], axis=1)
        x = x + xs * pw[2 * lvl:2 * lvl + 1, :] + xsw * pw[2 * lvl + 1:2 * lvl + 2, :]
        d, lvl = 2 * d, lvl + 1
    h_in = jnp.where(jrow >= 1, pltpu.roll(x, 1, axis=0), 0.0)
    y = y + _dot(h_in.astype(BF16), ccd_ref[...])
    for l in range(ls):
        y_ref[pl.ds(l, rows, stride=ls), :] = y[:, l * LANES:(l + 1) * LANES]


def _s5(u, tz, gm, cc, pw, batch, seq, nb=2):
    t, width = u.shape
    nslab = width // LANES
    gs = LANES // S5_GROUP
    nj = seq // S5_SUB
    sub_w = S5_SUB * LANES
    st_w = 2 * gs * S5_STATE
    pwd = pw.reshape(nslab, gs, pw.shape[1], 2, S5_STATE).transpose(0, 2, 3, 1, 4).reshape(nslab, pw.shape[1], st_w)
    grp_spec = lambda a: pl.BlockSpec((gs,) + a.shape[1:], lambda s, b: (s, 0, 0))
    nb = nb if batch % nb == 0 else 1
    assert nj & (nj - 1) == 0
    return pl.pallas_call(
        functools.partial(_s5_kernel, nj=nj),
        grid=(nslab, batch // nb),
        in_specs=[pl.BlockSpec((nb * seq, LANES), lambda s, b: (b, s)),
                  grp_spec(tz), grp_spec(gm), grp_spec(cc),
                  pl.BlockSpec((1,) + pwd.shape[1:], lambda s, b: (s, 0, 0))],
        out_specs=pl.BlockSpec((nb * seq, LANES), lambda s, b: (b, s)),
        out_shape=jax.ShapeDtypeStruct((t, width), F32),
        scratch_shapes=[pltpu.VMEM((sub_w, sub_w), BF16), pltpu.VMEM((sub_w, st_w), BF16),
                        pltpu.VMEM((st_w, sub_w), BF16)],
        compiler_params=_params("arbitrary", "arbitrary"),
        name="s5",
    )(u, tz, gm, cc, pwd)


def _s5_param_pack(lam_re, lam_im, b_re, b_im, c_re, c_im, log_dt, nj):
    ls = S5_SUB
    dt = jnp.exp(log_dt)[:, None]
    zr, zi = lam_re * dt, lam_im * dt

    def powers(steps):
        st = steps[:, None, None]
        mag = jnp.exp(st * zr)
        return mag * jnp.cos(st * zi), mag * jnp.sin(st * zi)

    ab_re, ab_im = powers(jnp.ones((1,), F32))
    ab_re, ab_im = ab_re[0], ab_im[0]
    den = lam_re * lam_re + lam_im * lam_im
    nr, ni = ab_re - 1.0, ab_im
    f_re = (nr * lam_re + ni * lam_im) / den
    f_im = (ni * lam_re - nr * lam_im) / den
    bb_re = f_re[..., None] * b_re - f_im[..., None] * b_im
    bb_im = f_re[..., None] * b_im + f_im[..., None] * b_re
    pr, pi = powers(jnp.arange(0, ls + 1, dtype=F32))
    cl_re = c_re[None] * pr[:, :, None, :] - c_im[None] * pi[:, :, None, :]
    cl_im = c_re[None] * pi[:, :, None, :] + c_im[None] * pr[:, :, None, :]
    kern = (jnp.einsum('dgpn,gnq->dgpq', cl_re[:ls], bb_re, precision=HIGHEST)
            - jnp.einsum('dgpn,gnq->dgpq', cl_im[:ls], bb_im, precision=HIGHEST))
    lag = jnp.arange(ls)[None, :] - jnp.arange(ls)[:, None]
    tz = jnp.where((lag >= 0)[:, :, None, None, None], kern[jnp.clip(lag, 0, ls - 1)], 0.0)
    g = lam_re.shape[0]
    pch = S5_GROUP
    tz = tz.transpose(2, 0, 4, 1, 3).reshape(g, ls * pch, ls * pch)
    rr, ri = pr[ls - 1::-1][:ls], pi[ls - 1::-1][:ls]
    gm_re = rr[..., None] * bb_re[None] - ri[..., None] * bb_im[None]
    gm_im = rr[..., None] * bb_im[None] + ri[..., None] * bb_re[None]
    gm = jnp.concatenate([gm_re, gm_im], axis=2)
    gm = gm.transpose(1, 0, 3, 2).reshape(g, ls * pch, 2 * S5_STATE)
    cc = jnp.concatenate([cl_re[1:], -cl_im[1:]], axis=3)
    cc = cc.transpose(1, 3, 0, 2).reshape(g, 2 * S5_STATE, ls * pch)
    lv = []
    d = 1
    while d < nj:
        lv.append(float(ls * d))
        d *= 2
    qr, qi = powers(jnp.asarray(lv, F32))
    pw = jnp.stack([jnp.concatenate([qr, qr], axis=-1), jnp.concatenate([-qi, qi], axis=-1)], axis=1)
    pw = pw.reshape(2 * len(lv), g, 2 * S5_STATE).transpose(1, 0, 2)
    pw = jnp.concatenate([pw, jnp.zeros((g, 16 - 2 * len(lv), 2 * S5_STATE), F32)], axis=1)
    return tz.astype(BF16), gm.astype(BF16), cc.astype(BF16), pw


def _ssd_kernel(xbc_ref, misc_ref, zc_ref, cw_ref, vec_ref, nw_ref, out_ref,
                st_ref, halo_ref, buf_ref, *, L):
    c = pl.program_id(1)
    width = M2_HEADS * HEAD_DIM
    gs = M2_STATE

    @pl.when(c == 0)
    def _():
        st_ref[...] = jnp.zeros_like(st_ref)
        halo_ref[...] = jnp.zeros_like(halo_ref)

    xbc = xbc_ref[...]
    buf_ref[0:SUBLANES, :] = halo_ref[...]
    buf_ref[SUBLANES:SUBLANES + L, :] = xbc
    halo_ref[...] = xbc[L - SUBLANES:L, :]
    cw = cw_ref[...]
    nshift = M2_CONV - 1
    si = lax.broadcasted_iota(jnp.int32, (nshift * L, L + SUBLANES), 0)
    sj = lax.broadcasted_iota(jnp.int32, (nshift * L, L + SUBLANES), 1)
    tap_i = (si >= L).astype(jnp.int32) + (si >= 2 * L).astype(jnp.int32)
    shift = jnp.where(sj == (si - tap_i * L) + tap_i + (SUBLANES - nshift), 1.0, 0.0).astype(BF16)
    zsh = _dot(shift, buf_ref[...].astype(BF16))
    acc = cw[M2_CONV:M2_CONV + 1, :] + cw[nshift:M2_CONV, :] * xbc
    for tap in range(nshift):
        acc = acc + cw[tap:tap + 1, :] * zsh[tap * L:(tap + 1) * L, :]
    xc = _silu(acc)
    xs = xc[:, :width]

    vec = vec_ref[...]
    dt = _softplus(misc_ref[:, 2 * LANES:3 * LANES] + vec[0:1, :])
    a_dt = dt * vec[1:2, :]
    il = lax.broadcasted_iota(jnp.int32, (L, 3 * L), 0)
    jl = lax.broadcasted_iota(jnp.int32, (L, 3 * L), 1) & (L - 1)
    tri3 = jnp.where(jl <= il, 1.0, 0.0).astype(BF16)
    cum = _dot(tri3, jnp.concatenate(_split3(a_dt), axis=0))
    cum_t = cum.T

    eh = lax.broadcasted_iota(jnp.int32, (3 * LANES, width), 0) & (LANES - 1)
    ec = lax.broadcasted_iota(jnp.int32, (3 * LANES, width), 1)
    expand3 = jnp.where((ec >> 6) == eh, 1.0, 0.0).astype(BF16)
    dt_x = _dot(jnp.concatenate(_split3(dt), axis=1), expand3)
    cum_x = _dot(jnp.concatenate(_split3(cum), axis=1), expand3)
    cum_lx = cum_x[L - 1:L, :]
    xdt = xs * dt_x
    xdd = (xdt * jnp.exp(cum_lx - cum_x)).astype(BF16)
    xdt_b = xdt.astype(BF16)
    ecum = jnp.exp(cum_x)
    e_tot = jnp.exp(cum_lx)

    tril = (lax.broadcasted_iota(jnp.int32, (L, L), 1) <= lax.broadcasted_iota(jnp.int32, (L, L), 0))
    lane = lax.broadcasted_iota(jnp.int32, (1, LANES), 1)
    m0 = lane < HEAD_DIM
    hpg = M2_HEADS // M2_GROUPS
    gw = hpg * HEAD_DIM
    groups = range(M2_GROUPS)
    heads = range(M2_HEADS)
    bm_f = [xc[:, width + g * gs:width + (g + 1) * gs] for g in groups]
    bm = [b.astype(BF16) for b in bm_f]
    cm = [xc[:, width + (M2_GROUPS + g) * gs:width + (M2_GROUPS + g + 1) * gs].astype(BF16) for g in groups]
    cbm = [jnp.where(tril, _dot_nt(cm[g], bm[g]), 0.0) for g in groups]
    st = [st_ref[:, g * gw:(g + 1) * gw] for g in groups]
    y_off = [_dot(cm[g], st[g].astype(BF16)) for g in groups]
    for g in groups:
        st_ref[:, g * gw:(g + 1) * gw] = (st[g] * e_tot[:, g * gw:(g + 1) * gw]
                                          + _dot(bm_f[g].T.astype(BF16), xdd[:, g * gw:(g + 1) * gw]))
    mh = [(cbm[h // hpg] * jnp.exp(jnp.minimum(cum[:, h:h + 1] - cum_t[h:h + 1, :], 0.0))).astype(BF16)
          for h in heads]
    zero = jnp.zeros((), BF16)
    yd = [_dot(mh[h], jnp.where(m0 if h % 2 == 0 else jnp.logical_not(m0),
                                xdt_b[:, (h // 2) * LANES:(h // 2 + 1) * LANES], zero)) for h in heads]
    y_diag = jnp.concatenate([yd[2 * p] + yd[2 * p + 1] for p in range(M2_HEADS // 2)], axis=1)
    y = y_diag + jnp.concatenate(y_off, axis=1) * ecum + xs * nw_ref[1:2, :]
    y = y * _silu(zc_ref[...])
    gw = width // M2_GROUPS
    outs = []
    for g in range(M2_GROUPS):
        yg = y[:, g * gw:(g + 1) * gw]
        outs.append(_rms(yg, nw_ref[0:1, g * gw:(g + 1) * gw]))
    out_ref[...] = jnp.concatenate(outs, axis=1)


def _ssd(xbc, misc, z, cw, vec, nw, batch, seq):
    L = M2_CHUNK
    t, cd = xbc.shape
    nc = seq // L
    width = M2_HEADS * HEAD_DIM
    return pl.pallas_call(
        functools.partial(_ssd_kernel, L=L),
        grid=(batch, nc),
        in_specs=[pl.BlockSpec((L, cd), lambda b, c: (b * nc + c, 0)),
                  pl.BlockSpec((L, misc.shape[1]), lambda b, c: (b * nc + c, 0)),
                  pl.BlockSpec((L, width), lambda b, c: (b * nc + c, 0)),
                  pl.BlockSpec((SUBLANES, cd), lambda b, c: (0, 0)),
                  pl.BlockSpec((SUBLANES, LANES), lambda b, c: (0, 0)),
                  pl.BlockSpec((SUBLANES, width), lambda b, c: (0, 0))],
        out_specs=pl.BlockSpec((L, width), lambda b, c: (b * nc + c, 0)),
        out_shape=jax.ShapeDtypeStruct((t, width), F32),
        scratch_shapes=[pltpu.VMEM((M2_STATE, width), F32), pltpu.VMEM((SUBLANES, cd), F32),
                        pltpu.VMEM((L + SUBLANES, cd), F32)],
        compiler_params=_params("arbitrary", "arbitrary"),
        name="ssd",
    )(xbc, misc, z, cw, vec, nw)


def _mla_prep_kernel(cq_ref, ckv_ref, misc_ref, cos_ref, sin_ref, qnw_ref, kvnw_ref, wq_ref, wkv_ref,
                     q_ref, k_ref, v_ref):
    nh = MLA_HEADS
    cos, sin = cos_ref[...], sin_ref[...]
    q = _dot(_rms(cq_ref[...], qnw_ref[...]).astype(BF16), wq_ref[...])
    scale = (MLA_NOPE + MLA_ROPE) ** -0.5 * math.log2(math.e)
    rope0 = nh * MLA_NOPE
    rot0 = rope0 + nh * MLA_ROPE
    for pair in range(nh // 2):
        sl = slice(pair * LANES, (pair + 1) * LANES)
        qr = (q[:, rope0:rot0][:, sl] * cos + q[:, rot0:][:, sl] * sin) * scale
        for hh in range(2):
            h = 2 * pair + hh
            q_ref[:, h * 2 * LANES:h * 2 * LANES + LANES] = (
                q[:, h * MLA_NOPE:(h + 1) * MLA_NOPE] * scale).astype(q_ref.dtype)
            q_ref[:, h * 2 * LANES + LANES:(h + 1) * 2 * LANES] = qr.astype(q_ref.dtype)
    kv = _dot(_rms(ckv_ref[...], kvnw_ref[...]).astype(BF16), wkv_ref[...])
    kpe = misc_ref[:, :LANES] * cos + misc_ref[:, LANES:2 * LANES] * sin
    lane = lax.broadcasted_iota(jnp.int32, (1, LANES), 1)
    slots = [jnp.where(lane < MLA_ROPE, kpe, 0.0), jnp.where(lane < MLA_ROPE, 0.0, kpe)]
    for h in range(nh):
        k_ref[:, h * 2 * LANES:h * 2 * LANES + LANES] = kv[:, h * MLA_NOPE:(h + 1) * MLA_NOPE].astype(k_ref.dtype)
        k_ref[:, h * 2 * LANES + LANES:(h + 1) * 2 * LANES] = slots[h % 2].astype(k_ref.dtype)
    v_ref[...] = kv[:, nh * MLA_NOPE:].astype(v_ref.dtype)


def _mla_prep(cq, ckv, misc, cos, sin, qnw, kvnw, wq, wkv, tm=256):
    t = cq.shape[0]
    nh = MLA_HEADS
    row = lambda n: pl.BlockSpec((tm, n), lambda i: (i, 0))
    full = lambda a: pl.BlockSpec(a.shape, lambda i: (0, 0))
    qnw, kvnw = qnw.reshape(1, -1), kvnw.reshape(1, -1)
    return pl.pallas_call(
        _mla_prep_kernel,
        grid=(t // tm,),
        in_specs=[row(cq.shape[1]), row(ckv.shape[1]), row(misc.shape[1]), row(LANES), row(LANES),
                  full(qnw), full(kvnw), full(wq), full(wkv)],
        out_specs=[row(nh * 2 * LANES), row(nh * 2 * LANES), row(nh * MLA_V)],
        out_shape=[jax.ShapeDtypeStruct((t, nh * 2 * LANES), BF16),
                   jax.ShapeDtypeStruct((t, nh * 2 * LANES), BF16),
                   jax.ShapeDtypeStruct((t, nh * MLA_V), BF16)],
        compiler_params=_params("arbitrary"),
        name="mla_prep",
    )(cq, ckv, misc, cos, sin, qnw, kvnw, wq, wkv)


def _flash_kernel(q_ref, k_ref, v_ref, o_ref, *, tq, hb):
    qi = pl.program_id(2)
    qw, vw = 2 * LANES, MLA_V
    qs = [q_ref[:, h * qw:(h + 1) * qw] for h in range(hb)]

    def block(j, carries, mask):
        off = pl.multiple_of(j * tq, tq)
        hs = range(hb)
        s = [_dot_nt(qs[h], k_ref[pl.ds(off, tq), h * qw:(h + 1) * qw]) for h in hs]
        if mask is not None:
            s = [jnp.where(mask, s[h], -jnp.inf) for h in hs]
        m_new = [jnp.maximum(carries[h][0], jnp.max(s[h], axis=-1, keepdims=True)) for h in hs]
        alpha = [jnp.exp2(carries[h][0] - m_new[h]) for h in hs]
        p = [jnp.exp2(s[h] - m_new[h]) for h in hs]
        l_new = [alpha[h] * carries[h][1] + jnp.sum(p[h], axis=-1, keepdims=True) for h in hs]
        pv = [_dot(p[h].astype(BF16), v_ref[pl.ds(off, tq), h * vw:(h + 1) * vw]) for h in hs]
        return tuple((m_new[h], l_new[h], alpha[h] * carries[h][2] + pv[h]) for h in hs)

    init = tuple((jnp.full((tq, 1), -jnp.inf, F32), jnp.zeros((tq, 1), F32), jnp.zeros((tq, vw), F32))
                 for _ in range(hb))
    carries = lax.fori_loop(0, qi, lambda j, c: block(j, c, None), init)
    ri = lax.broadcasted_iota(jnp.int32, (tq, tq), 0)
    ci = lax.broadcasted_iota(jnp.int32, (tq, tq), 1)
    carries = block(qi, carries, ci <= ri)
    for h in range(hb):
        _, l_fin, acc = carries[h]
        o_ref[:, h * vw:(h + 1) * vw] = (acc / l_fin).astype(o_ref.dtype)


def _flash(q, k, v, batch, seq, tq=512, hb=2):
    t = q.shape[0]
    tq = min(tq, seq)
    nq = seq // tq
    ng = MLA_HEADS // hb
    return pl.pallas_call(
        functools.partial(_flash_kernel, tq=tq, hb=hb),
        grid=(batch, ng, nq),
        in_specs=[pl.BlockSpec((tq, hb * 2 * LANES), lambda b, g, i: (b * nq + i, g)),
                  pl.BlockSpec((seq, hb * 2 * LANES), lambda b, g, i: (b, g)),
                  pl.BlockSpec((seq, hb * MLA_V), lambda b, g, i: (b, g))],
        out_specs=pl.BlockSpec((tq, hb * MLA_V), lambda b, g, i: (b * nq + i, g)),
        out_shape=jax.ShapeDtypeStruct((t, MLA_HEADS * MLA_V), F32),
        compiler_params=_params("arbitrary", "arbitrary", "arbitrary"),
        name="mla_flash",
    )(q, k, v)


def _finish(h_ref, acc, fnw_ref, o_ref, final):
    hn = h_ref[...] + acc
    o_ref[...] = _rms(hn, fnw_ref[...]) if final else hn


def _even_tail_kernel(a_ref, ys_ref, u_ref, q_ref, z_ref, h_ref, mk_ref, mv_ref, vec_ref, gw_ref, ow_ref,
                      fnw_ref, o_ref, *, final):
    w = RW_WIDTH
    z = z_ref[...]
    ga = (a_ref[...] * _silu(z[:, :w])).astype(BF16)
    yb = _gelu_tanh(ys_ref[...] + vec_ref[0:1, :] * u_ref[...])
    gate = _sigmoid(_dot(yb.astype(BF16), gw_ref[...]) + vec_ref[1:2, :])
    gb = (yb * gate * _silu(z[:, w:2 * w])).astype(BF16)
    gm = (_mem_attend(q_ref[...], mk_ref[0], mv_ref[0]) * _silu(z[:, 2 * w:])).astype(BF16)
    acc = _dot(ga, ow_ref[0:w, :]) + _dot(gb, ow_ref[w:2 * w, :]) + _dot(gm, ow_ref[2 * w:, :])
    _finish(h_ref, acc, fnw_ref, o_ref, final)


def _odd_tail_kernel(c_ref, d_ref, q_ref, z_ref, h_ref, mk_ref, mv_ref, ow_ref, fnw_ref, o_ref, *, final):
    w = D_MODEL
    z = z_ref[...]
    gc = c_ref[...].astype(BF16)
    gd = (d_ref[...] * _silu(z[:, :w])).astype(BF16)
    gm = (_mem_attend(q_ref[...], mk_ref[0], mv_ref[0]) * _silu(z[:, w:])).astype(BF16)
    acc = _dot(gc, ow_ref[0:w, :]) + _dot(gd, ow_ref[w:2 * w, :]) + _dot(gm, ow_ref[2 * w:, :])
    _finish(h_ref, acc, fnw_ref, o_ref, final)


def _tail_call(kernel, rows, consts, h, mk, mv, fnw, batch, seq, final, name, tm=256):
    t, d = h.shape
    nt = seq // tm
    row = lambda a: pl.BlockSpec((tm, a.shape[1]), lambda b, i: (b * nt + i, 0))
    full = lambda a: pl.BlockSpec(a.shape, lambda b, i: (0,) * a.ndim)
    memspec = pl.BlockSpec((1,) + mk.shape[1:], lambda b, i: (b, 0, 0))
    fnw = fnw.reshape(1, d)
    return pl.pallas_call(
        functools.partial(kernel, final=final),
        grid=(batch, nt),
        in_specs=[row(a) for a in rows] + [row(h), memspec, memspec] + [full(a) for a in consts] + [full(fnw)],
        out_specs=row(h),
        out_shape=jax.ShapeDtypeStruct((t, d), F32),
        compiler_params=_params("arbitrary", "arbitrary"),
        name=name,
    )(*rows, h, mk, mv, *consts, fnw)


def _pad_rows(rows, n=SUBLANES):
    width = rows[0].shape[-1]
    return jnp.concatenate([r.reshape(1, width) for r in rows] + [jnp.zeros((n - len(rows), width), F32)], axis=0)


def _even_layer(h, batch, seq, mk, mv, nw, in_w, out_w, fnw, final, mu, w0, w2, a0, a2, k_k, k_a, r_k,
                ln_w, ln_b, lam_re, lam_im, b_re, b_im, c_re, c_im, s5_d, log_dt, glu_w, glu_b):
    rw_proj = 3 * RW_WIDTH + 2 * RW_LORA
    ev_width = 2 * RW_WIDTH + MEM_WIDTH
    splits = (rw_proj, RW_WIDTH, MEM_WIDTH, ev_width)
    p, u, q_mem, z = _norm_proj(h, nw, in_w.astype(BF16), splits, (F32, F32, F32, F32))
    rvec, wa = _rwkv_param_pack(w0, w2, a0, a2, k_k, k_a, r_k, ln_w, ln_b)
    a_out = _rwkv(p, mu, rvec, wa, batch, seq)

    nj = seq // S5_SUB
    groups = RW_WIDTH // S5_GROUP
    tz, gm, cc, pw = _s5_param_pack(lam_re, lam_im, b_re, b_im, c_re, c_im, log_dt, nj)
    ys = _s5(u, tz, gm, cc, pw, batch, seq)

    vec = _pad_rows([s5_d, glu_b])
    return _tail_call(_even_tail_kernel, [a_out, ys, u, q_mem, z], [vec, glu_w.astype(BF16), out_w.astype(BF16)],
                      h, mk, mv, fnw, batch, seq, final, "even_tail")


def _odd_layer(h, batch, seq, mk, mv, cos, sin, nw, in_w, out_w, fnw, final, conv_w, conv_b, dt_bias, a_log,
               m2_d, m2_norm_w, q_norm_w, wq_up, kv_norm_w, wkv_up):
    width = M2_HEADS * HEAD_DIM
    conv_dim = width + 2 * M2_GROUPS * M2_STATE
    nh = MLA_HEADS
    o_dt = conv_dim
    o_cq = o_dt + M2_HEADS
    o_ckv = o_cq + MLA_Q_RANK
    o_kr = o_ckv + MLA_KV_RANK
    o_qm = o_kr + MLA_ROPE
    o_z = o_qm + MEM_WIDTH
    half = MLA_ROPE // 2
    w_kr = in_w[:, o_kr:o_qm]
    w_kr_sw = jnp.concatenate([w_kr[:, half:], w_kr[:, :half]], axis=1)
    w_misc = jnp.concatenate([w_kr, w_kr, w_kr_sw, w_kr_sw, in_w[:, o_dt:o_cq],
                              jnp.zeros((in_w.shape[0], LANES - M2_HEADS), in_w.dtype)], axis=1)
    w_all = jnp.concatenate([in_w[:, :o_dt], in_w[:, o_cq:o_ckv], in_w[:, o_ckv:o_kr], in_w[:, o_qm:o_z],
                             in_w[:, o_z:], w_misc], axis=1).astype(BF16)
    splits = (conv_dim, MLA_Q_RANK, MLA_KV_RANK, MEM_WIDTH, width, D_MODEL + MEM_WIDTH, 3 * LANES)
    xbc, cq, ckv, q_mem, z_c, z_dm, misc = _norm_proj(h, nw, w_all, splits, (F32,) * 7)

    cw = _pad_rows(list(conv_w) + [conv_b])
    pad16 = lambda x: jnp.concatenate([x, jnp.zeros((LANES - M2_HEADS,), F32)])
    vec = _pad_rows([pad16(dt_bias), pad16(-jnp.exp(a_log))])
    nwd = _pad_rows([m2_norm_w, jnp.repeat(m2_d, HEAD_DIM)])
    c_out = _ssd(xbc, misc, z_c, cw, vec, nwd, batch, seq)

    wq = wq_up.reshape(MLA_Q_RANK, nh, MLA_NOPE + MLA_ROPE)
    wq_r = wq[:, :, MLA_NOPE:]
    wq_sw = jnp.concatenate([wq_r[:, :, half:], wq_r[:, :, :half]], axis=2)
    wq_all = jnp.concatenate([wq[:, :, :MLA_NOPE].reshape(MLA_Q_RANK, -1), wq_r.reshape(MLA_Q_RANK, -1),
                              wq_sw.reshape(MLA_Q_RANK, -1)], axis=1).astype(BF16)
    wkv = wkv_up.reshape(MLA_KV_RANK, nh, MLA_NOPE + MLA_V)
    wkv_all = jnp.concatenate([wkv[:, :, :MLA_NOPE].reshape(MLA_KV_RANK, -1),
                               wkv[:, :, MLA_NOPE:].reshape(MLA_KV_RANK, -1)], axis=1).astype(BF16)
    qc, kc, vv = _mla_prep(cq, ckv, misc, cos, sin, q_norm_w, kv_norm_w, wq_all, wkv_all)
    d_attn = _flash(qc, kc, vv, batch, seq)

    return _tail_call(_odd_tail_kernel, [c_out, d_attn, q_mem, z_dm], [out_w.astype(BF16)],
                      h, mk, mv, fnw, batch, seq, final, "odd_tail")


def _rope_tables(positions):
    inv = 1.0 / (ROPE_THETA ** (jnp.arange(0, MLA_ROPE, 2, dtype=F32) / MLA_ROPE))
    ang = positions.astype(F32).reshape(-1, 1) * inv
    cos, sin = jnp.cos(ang), jnp.sin(ang)
    return jnp.tile(cos, (1, 4)), jnp.tile(jnp.concatenate([-sin, sin], axis=1), (1, 2))


def kernel(x, mem, positions, norm_w, mem_norm_w, final_norm_w, mem_kv_w, ev_in_w, ev_out_w, rw_mu, rw_w0, rw_w2, rw_a0, rw_a2, rw_k_k, rw_k_a, rw_r_k, rw_ln_w, rw_ln_b, s5_lambda_re, s5_lambda_im, s5_b_re, s5_b_im, s5_c_re, s5_c_im, s5_d, s5_log_dt, s5_glu_w, s5_glu_b, od_in_w, od_out_w, m2_conv_w, m2_conv_b, m2_dt_bias, m2_a_log, m2_d, m2_norm_w, mla_q_norm_w, mla_wq_up, mla_kv_norm_w, mla_wkv_up):
    batch, seq, d = x.shape
    depth = norm_w.shape[0]
    mk, mv = _mem_kv(mem, mem_norm_w, mem_kv_w.astype(BF16))
    cos, sin = _rope_tables(positions)
    h = x.reshape(batch * seq, d)
    for layer in range(depth):
        i = layer // 2
        final = layer == depth - 1
        if layer % 2 == 0:
            h = _even_layer(h, batch, seq, mk[layer], mv[layer], norm_w[layer], ev_in_w[i], ev_out_w[i],
                            final_norm_w, final, rw_mu[i], rw_w0[i], rw_w2[i], rw_a0[i], rw_a2[i], rw_k_k[i],
                            rw_k_a[i], rw_r_k[i], rw_ln_w[i], rw_ln_b[i], s5_lambda_re[i], s5_lambda_im[i],
                            s5_b_re[i], s5_b_im[i], s5_c_re[i], s5_c_im[i], s5_d[i], s5_log_dt[i],
                            s5_glu_w[i], s5_glu_b[i])
        else:
            h = _odd_layer(h, batch, seq, mk[layer], mv[layer], cos, sin, norm_w[layer], od_in_w[i], od_out_w[i],
                           final_norm_w, final, m2_conv_w[i], m2_conv_b[i], m2_dt_bias[i], m2_a_log[i], m2_d[i],
                           m2_norm_w[i], mla_q_norm_w[i], mla_wq_up[i], mla_kv_norm_w[i], mla_wkv_up[i])
    return h.reshape(batch, seq, d)
```

```python
import functools
import math

import jax
import jax.numpy as jnp
from jax import lax
from jax.experimental import pallas as pl
from jax.experimental.pallas import tpu as pltpu

F32 = jnp.float32
BF16 = jnp.bfloat16
HIGHEST = lax.Precision.HIGHEST

LANES = 128
SUBLANES = 8
VMEM_LIMIT_BYTES = 56 * 1024 * 1024

D_MODEL = 1024
HEAD_DIM = 64
NORM_EPS = 1e-6
RW_WIDTH = 1024
RW_LORA = 64
RW_GN_EPS = 64e-5
RW_CHUNK = 64
S5_GROUP = 16
S5_STATE = 64
S5_SUB = 16
M2_HEADS = 16
M2_GROUPS = 2
M2_STATE = 128
M2_CONV = 4
M2_CHUNK = 128
MLA_HEADS = 8
MLA_NOPE = 128
MLA_ROPE = 64
MLA_V = 128
MLA_Q_RANK = 384
MLA_KV_RANK = 256
ROPE_THETA = 10000.0
MEM_HEADS = 4
MEM_WIDTH = MEM_HEADS * HEAD_DIM


def _dot(a, b, precision=None):
    return jnp.dot(a, b, preferred_element_type=F32, precision=precision)


def _dot_nt(a, b, precision=None):
    return lax.dot_general(a, b, (((1,), (1,)), ((), ())), preferred_element_type=F32,
                           precision=precision)


def _sigmoid(x):
    return 1.0 / (1.0 + jnp.exp(-x))


def _silu(x):
    return x * _sigmoid(x)


def _softplus(x):
    return jnp.maximum(x, 0.0) + jnp.log(1.0 + jnp.exp(-jnp.abs(x)))


def _gelu_tanh(x):
    return 0.5 * x * (1.0 + jnp.tanh(math.sqrt(2.0 / math.pi) * (x + 0.044715 * (x * x * x))))


def _rms(x, w, eps=NORM_EPS):
    ms = jnp.mean(x * x, axis=-1, keepdims=True)
    return x * lax.rsqrt(ms + eps) * w


def _params(*sem):
    return pltpu.CompilerParams(dimension_semantics=sem, vmem_limit_bytes=VMEM_LIMIT_BYTES)


def _norm_proj_kernel(x_ref, nw_ref, w_ref, *out_refs, splits):
    xn = _rms(x_ref[...], nw_ref[...]).astype(BF16)
    off = 0
    for o_ref, n in zip(out_refs, splits):
        o_ref[...] = _dot(xn, w_ref[:, off:off + n]).astype(o_ref.dtype)
        off += n


def _norm_proj(x, nw, w, splits, dtypes, tm=256):
    t, d = x.shape
    n = w.shape[1]
    assert sum(splits) == n and t % tm == 0
    return pl.pallas_call(
        functools.partial(_norm_proj_kernel, splits=splits),
        grid=(t // tm,),
        in_specs=[pl.BlockSpec((tm, d), lambda i: (i, 0)),
                  pl.BlockSpec((1, d), lambda i: (0, 0)),
                  pl.BlockSpec((d, n), lambda i: (0, 0))],
        out_specs=[pl.BlockSpec((tm, s), lambda i: (i, 0)) for s in splits],
        out_shape=[jax.ShapeDtypeStruct((t, s), dt) for s, dt in zip(splits, dtypes)],
        compiler_params=_params("arbitrary"),
        name="norm_proj",
    )(x, nw.reshape(1, d), w)


def _mem_kv_kernel(mem_ref, nw_ref, w_ref, k_ref, v_ref):
    mn = _rms(mem_ref[0], nw_ref[...]).astype(BF16)
    kv = _dot(mn, w_ref[0])
    k_ref[0, 0] = kv[:, :MEM_WIDTH].astype(k_ref.dtype)
    v_ref[0, 0] = kv[:, MEM_WIDTH:].astype(v_ref.dtype)


def _mem_kv(mem, nw, w):
    b, m, d = mem.shape
    depth = w.shape[0]
    shp = jax.ShapeDtypeStruct((depth, b, m, MEM_WIDTH), BF16)
    return pl.pallas_call(
        _mem_kv_kernel,
        grid=(depth, b),
        in_specs=[pl.BlockSpec((1, m, d), lambda l, i: (i, 0, 0)),
                  pl.BlockSpec((1, d), lambda l, i: (0, 0)),
                  pl.BlockSpec((1, d, 2 * MEM_WIDTH), lambda l, i: (l, 0, 0))],
        out_specs=[pl.BlockSpec((1, 1, m, MEM_WIDTH), lambda l, i: (l, i, 0, 0))] * 2,
        out_shape=[shp, shp],
        compiler_params=_params("arbitrary", "arbitrary"),
        name="mem_kv",
    )(mem, nw.reshape(1, d), w)


def _mem_attend(q, k, v):
    lane = lax.broadcasted_iota(jnp.int32, (1, MEM_WIDTH), 1)
    qb = (q * (HEAD_DIM ** -0.5)).astype(BF16)
    out = jnp.zeros(q.shape, F32)
    for h in range(MEM_HEADS):
        hm = (lane >> 6) == h
        sc = _dot_nt(qb, jnp.where(hm, k, jnp.zeros_like(k)))
        sc = sc - jnp.max(sc, axis=-1, keepdims=True)
        p = jnp.exp(sc)
        p = p / jnp.sum(p, axis=-1, keepdims=True)
        out = out + _dot(p.astype(BF16), jnp.where(hm, v, jnp.zeros_like(v)))
    return out


def _split3(x):
    hi = x.astype(BF16)
    r1 = x - hi.astype(F32)
    mid = r1.astype(BF16)
    lo = (r1 - mid.astype(F32)).astype(BF16)
    return hi, mid, lo


def _rwkv_kernel(p_ref, mu_ref, vec_ref, wa_ref, out_ref, st_ref, prev_ref, *, L):
    c = pl.program_id(1)
    W = RW_WIDTH

    @pl.when(c == 0)
    def _():
        st_ref[...] = jnp.zeros_like(st_ref)
        prev_ref[...] = jnp.zeros_like(prev_ref)

    x = p_ref[...]
    R = x.shape[0]
    nck = R // L
    row_l = lax.broadcasted_iota(jnp.int32, (R, 1), 0)
    xs = jnp.where(row_l == 0, prev_ref[0:1, :], pltpu.roll(x, 1, axis=0))
    prev_ref[0:1, :] = x[R - 1:R, :]
    xm = x + (xs - x) * mu_ref[...]
    r, k, v, lat = xm[:, :W], xm[:, W:2 * W], xm[:, 2 * W:3 * W], xm[:, 3 * W:]

    w0, a0, k_k, k_a, r_k, ln_w, ln_b = [vec_ref[i:i + 1, :] for i in range(7)]
    lane = lax.broadcasted_iota(jnp.int32, (1, LANES), 1)
    lat_t = jnp.where(lane < RW_LORA, jnp.tanh(lat), lat).astype(BF16)
    la = _dot(lat_t, wa_ref[...])
    log_w = -_softplus(-(w0 + la[:, :W])) - 0.5
    lw = -jnp.exp(log_w)
    iclr = _sigmoid(a0 + la[:, W:])
    kk0 = k * k_k
    kp = k * (1.0 + (iclr - 1.0) * k_a)
    rkr = r * kp * r_k

    il = lax.broadcasted_iota(jnp.int32, (R, 3 * R), 0)
    jl = lax.broadcasted_iota(jnp.int32, (R, 3 * R), 1) & (R - 1)
    same_chunk = (il & -L) == (jl & -L)
    tri3 = jnp.where(same_chunk & (jl <= il), 1.0, 0.0).astype(BF16)
    cum = _dot(tri3, jnp.concatenate(_split3(lw), axis=0))
    cum_ls = [cum[(ck + 1) * L - 1:(ck + 1) * L, :] for ck in range(nck)]
    cum_end = jnp.concatenate([jnp.broadcast_to(cl, (L, W)) for cl in cum_ls], axis=0)
    e_neg = jnp.exp(-cum)
    e_exc = jnp.exp(cum - lw)
    e_inc = jnp.exp(cum)
    dend = jnp.exp(cum_end - cum)
    e_tot = [jnp.exp(cl) for cl in cum_ls]

    i2 = lax.broadcasted_iota(jnp.int32, (LANES, LANES), 0)
    j2 = lax.broadcasted_iota(jnp.int32, (LANES, LANES), 1)
    same_head = (i2 >> 6) == (j2 >> 6)
    head_ones = jnp.where(same_head, 1.0, 0.0).astype(BF16)
    strict = same_head & (j2 < i2)
    incl = same_head & (j2 <= i2)
    eye = jnp.where(i2 == j2, 1.0, 0.0)
    levels = [((i2 & -(2 * s)) == (j2 & -(2 * s))) & ((i2 & s) != 0) & ((j2 & s) == 0)
              for s in (1, 2, 4, 8, 16, 32)]
    m0 = lane < HEAD_DIM
    stack = lambda t: jnp.concatenate([jnp.where(m0, t, 0.0), jnp.where(m0, 0.0, t)], axis=0)

    npair = W // LANES
    units = [(slice(ck * L, (ck + 1) * L), slice(p * LANES, (p + 1) * LANES))
             for ck in range(nck) for p in range(npair)]
    each = lambda f: [f(i) for i in range(len(units))]
    at = lambda arr, i: arr[units[i][0], units[i][1]]
    sums = each(lambda i: _dot(jnp.concatenate([at(kk0, i) * at(kk0, i), at(rkr, i)], axis=0).astype(BF16),
                               head_ones))
    kk = each(lambda i: at(kk0, i) * lax.rsqrt(jnp.maximum(sums[i][:L], 1e-24)))
    bvec = each(lambda i: kk[i] * at(iclr, i))
    ar = each(lambda i: jnp.concatenate([stack(-kk[i] * at(e_exc, i)), stack(at(r, i) * at(e_inc, i))],
                                        axis=0).astype(BF16))
    b2 = each(lambda i: stack(bvec[i] * at(e_neg, i)).astype(BF16))
    k2 = each(lambda i: stack(at(kp, i) * at(e_neg, i)).astype(BF16))
    v2f = each(lambda i: stack(at(v, i)))
    v2 = each(lambda i: v2f[i].astype(BF16))
    bk_end = each(lambda i: jnp.concatenate([stack(bvec[i] * at(dend, i)), stack(at(kp, i) * at(dend, i))],
                                            axis=0).astype(BF16))
    sb = each(lambda i: _dot_nt(ar[i], b2[i]))
    sk = each(lambda i: _dot_nt(ar[i], k2[i]))
    aab = each(lambda i: jnp.where(strict, sb[i][:LANES], 0.0))

    inv = each(lambda i: eye + jnp.where(levels[0], aab[i], 0.0))
    for lvl in levels[1:]:
        invb = each(lambda i: inv[i].astype(BF16))
        t1 = each(lambda i: _dot(invb[i], jnp.where(lvl, aab[i], 0.0).astype(BF16)).astype(BF16))
        inv = each(lambda i: inv[i] + _dot(t1[i], invb[i]))
    aak_v = each(lambda i: _dot(jnp.where(strict, sk[i][:LANES], 0.0).astype(BF16), v2[i]))
    rk_v = each(lambda i: _dot(jnp.where(incl, sk[i][LANES:], 0.0).astype(BF16), v2[i]))

    st = [st_ref[p] for p in range(npair)]
    y = [None] * len(units)
    for ck in range(nck):
        ids = [ck * npair + p for p in range(npair)]
        sta = [_dot_nt(ar[i], st[p].astype(BF16)) for p, i in enumerate(ids)]
        u2 = [_dot(inv[i].astype(BF16), (sta[p][:LANES] + aak_v[i]).astype(BF16)) for p, i in enumerate(ids)]
        y2 = [sta[p][LANES:] + _dot(jnp.where(incl, sb[i][LANES:], 0.0).astype(BF16), u2[p].astype(BF16)) + rk_v[i]
              for p, i in enumerate(ids)]
        for p, i in enumerate(ids):
            y[i] = y2[p][:L, :] + y2[p][L:, :]
            uv_t = jnp.concatenate([u2[p].T, v2f[i].T], axis=1).astype(BF16)
            st[p] = st[p] * e_tot[ck][:, units[i][1]] + _dot(uv_t, bk_end[i])
    for p in range(npair):
        st_ref[p] = st[p]

    mean = each(lambda i: _dot(y[i].astype(BF16), head_ones) * (1.0 / HEAD_DIM))
    dcen = each(lambda i: y[i] - mean[i])
    var = each(lambda i: _dot((dcen[i] * dcen[i]).astype(BF16), head_ones) * (1.0 / HEAD_DIM))
    for i, (rs, ls_) in enumerate(units):
        out_ref[rs, ls_] = (dcen[i] * lax.rsqrt(var[i] + RW_GN_EPS) * ln_w[:, ls_] + ln_b[:, ls_]
                            + sums[i][L:] * at(v, i))


def _rwkv(p, mu, vec, wa, batch, seq, nck=2):
    L = RW_CHUNK * (nck if seq % (nck * RW_CHUNK) == 0 else 1)
    t, pw = p.shape
    nc = seq // L
    return pl.pallas_call(
        functools.partial(_rwkv_kernel, L=RW_CHUNK),
        grid=(batch, nc),
        in_specs=[pl.BlockSpec((L, pw), lambda b, c: (b * nc + c, 0)),
                  pl.BlockSpec((1, pw), lambda b, c: (0, 0)),
                  pl.BlockSpec(vec.shape, lambda b, c: (0, 0)),
                  pl.BlockSpec(wa.shape, lambda b, c: (0, 0))],
        out_specs=pl.BlockSpec((L, RW_WIDTH), lambda b, c: (b * nc + c, 0)),
        out_shape=jax.ShapeDtypeStruct((t, RW_WIDTH), F32),
        scratch_shapes=[pltpu.VMEM((RW_WIDTH // LANES, LANES, LANES), F32), pltpu.VMEM((SUBLANES, pw), F32)],
        compiler_params=_params("arbitrary", "arbitrary"),
        name="rwkv7",
    )(p, mu.reshape(1, pw), vec, wa)


def _rwkv_param_pack(w0, w2, a0, a2, k_k, k_a, r_k, ln_w, ln_b):
    vec = _pad_rows([w0, a0, k_k, k_a, r_k.reshape(-1), ln_w, ln_b])
    z = jnp.zeros_like(w2)
    wa = jnp.concatenate([jnp.concatenate([w2, z], axis=1), jnp.concatenate([z, a2], axis=1)], axis=0)
    return vec, wa.astype(BF16)


def _s5_kernel(u_ref, tz_ref, gm_ref, cc_ref, pw_ref, y_ref, tzd_ref, gmd_ref, ccd_ref, *, nj):
    ls, pch, n = S5_SUB, S5_GROUP, S5_STATE
    gs = LANES // pch

    @pl.when(pl.program_id(1) == 0)
    def _():
        tzd_ref[...] = jnp.zeros_like(tzd_ref)
        gmd_ref[...] = jnp.zeros_like(gmd_ref)
        ccd_ref[...] = jnp.zeros_like(ccd_ref)
        for g in range(gs):
            for a in range(ls):
                r0 = a * LANES + g * pch
                for b in range(a, ls):
                    c0 = b * LANES + g * pch
                    tzd_ref[r0:r0 + pch, c0:c0 + pch] = tz_ref[g, a * pch:(a + 1) * pch, b * pch:(b + 1) * pch]
                for c in range(2):
                    gmd_ref[r0:r0 + pch, (c * gs + g) * n:(c * gs + g + 1) * n] = (
                        gm_ref[g, a * pch:(a + 1) * pch, c * n:(c + 1) * n])
                    ccd_ref[(c * gs + g) * n:(c * gs + g + 1) * n, r0:r0 + pch] = (
                        cc_ref[g, c * n:(c + 1) * n, a * pch:(a + 1) * pch])

    rows = u_ref.shape[0] // ls
    xcat = jnp.concatenate([u_ref[pl.ds(l, rows, stride=ls), :] for l in range(ls)], axis=1).astype(BF16)
    y = _dot(xcat, tzd_ref[...])
    x = _dot(xcat, gmd_ref[...])
    half = x.shape[1] // 2
    jrow = lax.broadcasted_iota(jnp.int32, (rows, 1), 0) & (nj - 1)
    pw = pw_ref[0]
    d, lvl = 1, 0
    while d < nj:
        xs = jnp.where(jrow >= d, pltpu.roll(x, d, axis=0), 0.0)
        xsw = jnp.concatenate([xs[:, half:], xs[:, :half]], axis=1)
        x = x + xs * pw[2 * lvl:2 * lvl + 1, :] + xsw * pw[2 * lvl + 1:2 * lvl + 2, :]
        d, lvl = 2 * d, lvl + 1
    h_in = jnp.where(jrow >= 1, pltpu.roll(x, 1, axis=0), 0.0)
    y = y + _dot(h_in.astype(BF16), ccd_ref[...])
    for l in range(ls):
        y_ref[pl.ds(l, rows, stride=ls), :] = y[:, l * LANES:(l + 1) * LANES]


def _s5(u, tz, gm, cc, pw, batch, seq, nb=2):
    t, width = u.shape
    nslab = width // LANES
    gs = LANES // S5_GROUP
    nj = seq // S5_SUB
    sub_w = S5_SUB * LANES
    st_w = 2 * gs * S5_STATE
    pwd = pw.reshape(nslab, gs, pw.shape[1], 2, S5_STATE).transpose(0, 2, 3, 1, 4).reshape(nslab, pw.shape[1], st_w)
    grp_spec = lambda a: pl.BlockSpec((gs,) + a.shape[1:], lambda s, b: (s, 0, 0))
    nb = nb if batch % nb == 0 else 1
    assert nj & (nj - 1) == 0
    return pl.pallas_call(
        functools.partial(_s5_kernel, nj=nj),
        grid=(nslab, batch // nb),
        in_specs=[pl.BlockSpec((nb * seq, LANES), lambda s, b: (b, s)),
                  grp_spec(tz), grp_spec(gm), grp_spec(cc),
                  pl.BlockSpec((1,) + pwd.shape[1:], lambda s, b: (s, 0, 0))],
        out_specs=pl.BlockSpec((nb * seq, LANES), lambda s, b: (b, s)),
        out_shape=jax.ShapeDtypeStruct((t, width), F32),
        scratch_shapes=[pltpu.VMEM((sub_w, sub_w), BF16), pltpu.VMEM((sub_w, st_w), BF16),
                        pltpu.VMEM((st_w, sub_w), BF16)],
        compiler_params=_params("arbitrary", "arbitrary"),
        name="s5",
    )(u, tz, gm, cc, pwd)


def _s5_param_pack(lam_re, lam_im, b_re, b_im, c_re, c_im, log_dt, nj):
    ls = S5_SUB
    dt = jnp.exp(log_dt)[:, None]
    zr, zi = lam_re * dt, lam_im * dt

    def powers(steps):
        st = steps[:, None, None]
        mag = jnp.exp(st * zr)
        return mag * jnp.cos(st * zi), mag * jnp.sin(st * zi)

    ab_re, ab_im = powers(jnp.ones((1,), F32))
    ab_re, ab_im = ab_re[0], ab_im[0]
    den = lam_re * lam_re + lam_im * lam_im
    nr, ni = ab_re - 1.0, ab_im
    f_re = (nr * lam_re + ni * lam_im) / den
    f_im = (ni * lam_re - nr * lam_im) / den
    bb_re = f_re[..., None] * b_re - f_im[..., None] * b_im
    bb_im = f_re[..., None] * b_im + f_im[..., None] * b_re
    pr, pi = powers(jnp.arange(0, ls + 1, dtype=F32))
    cl_re = c_re[None] * pr[:, :, None, :] - c_im[None] * pi[:, :, None, :]
    cl_im = c_re[None] * pi[:, :, None, :] + c_im[None] * pr[:, :, None, :]
    kern = jnp.einsum('dgpn,gnq->dgpq', jnp.concatenate([cl_re[:ls], -cl_im[:ls]], axis=3),
                      jnp.concatenate([bb_re, bb_im], axis=1), precision=HIGHEST)
    lag = jnp.arange(ls)[None, :] - jnp.arange(ls)[:, None]
    tz = jnp.where((lag >= 0)[:, :, None, None, None], kern[jnp.clip(lag, 0, ls - 1)], 0.0)
    g = lam_re.shape[0]
    pch = S5_GROUP
    tz = tz.transpose(2, 0, 4, 1, 3).reshape(g, ls * pch, ls * pch)
    rr, ri = pr[ls - 1::-1][:ls], pi[ls - 1::-1][:ls]
    gm_re = rr[..., None] * bb_re[None] - ri[..., None] * bb_im[None]
    gm_im = rr[..., None] * bb_im[None] + ri[..., None] * bb_re[None]
    gm = jnp.concatenate([gm_re, gm_im], axis=2)
    gm = gm.transpose(1, 0, 3, 2).reshape(g, ls * pch, 2 * S5_STATE)
    cc = jnp.concatenate([cl_re[1:], -cl_im[1:]], axis=3)
    cc = cc.transpose(1, 3, 0, 2).reshape(g, 2 * S5_STATE, ls * pch)
    lv = []
    d = 1
    while d < nj:
        lv.append(float(ls * d))
        d *= 2
    qr, qi = powers(jnp.asarray(lv, F32))
    pw = jnp.stack([jnp.concatenate([qr, qr], axis=-1), jnp.concatenate([-qi, qi], axis=-1)], axis=1)
    pw = pw.reshape(2 * len(lv), g, 2 * S5_STATE).transpose(1, 0, 2)
    pw = jnp.concatenate([pw, jnp.zeros((g, 16 - 2 * len(lv), 2 * S5_STATE), F32)], axis=1)
    return tz.astype(BF16), gm.astype(BF16), cc.astype(BF16), pw


def _ssd_kernel(xbc_ref, misc_ref, zc_ref, cw_ref, vec_ref, nw_ref, out_ref,
                st_ref, halo_ref, buf_ref, *, L):
    c = pl.program_id(1)
    width = M2_HEADS * HEAD_DIM
    gs = M2_STATE

    @pl.when(c == 0)
    def _():
        st_ref[...] = jnp.zeros_like(st_ref)
        halo_ref[...] = jnp.zeros_like(halo_ref)

    xbc = xbc_ref[...]
    buf_ref[0:SUBLANES, :] = halo_ref[...]
    buf_ref[SUBLANES:SUBLANES + L, :] = xbc
    halo_ref[...] = xbc[L - SUBLANES:L, :]
    cw = cw_ref[...]
    nshift = M2_CONV - 1
    si = lax.broadcasted_iota(jnp.int32, (nshift * L, L + SUBLANES), 0)
    sj = lax.broadcasted_iota(jnp.int32, (nshift * L, L + SUBLANES), 1)
    tap_i = (si >= L).astype(jnp.int32) + (si >= 2 * L).astype(jnp.int32)
    shift = jnp.where(sj == (si - tap_i * L) + tap_i + (SUBLANES - nshift), 1.0, 0.0).astype(BF16)
    zsh = _dot(shift, buf_ref[...].astype(BF16))
    acc = cw[M2_CONV:M2_CONV + 1, :] + cw[nshift:M2_CONV, :] * xbc
    for tap in range(nshift):
        acc = acc + cw[tap:tap + 1, :] * zsh[tap * L:(tap + 1) * L, :]
    xc = _silu(acc)
    xs = xc[:, :width]

    vec = vec_ref[...]
    dt = _softplus(misc_ref[:, 2 * LANES:3 * LANES] + vec[0:1, :])
    a_dt = dt * vec[1:2, :]
    il = lax.broadcasted_iota(jnp.int32, (L, 3 * L), 0)
    jl = lax.broadcasted_iota(jnp.int32, (L, 3 * L), 1) & (L - 1)
    tri3 = jnp.where(jl <= il, 1.0, 0.0).astype(BF16)
    cum = _dot(tri3, jnp.concatenate(_split3(a_dt), axis=0))
    cum_t = cum.T

    eh = lax.broadcasted_iota(jnp.int32, (3 * LANES, width), 0) & (LANES - 1)
    ec = lax.broadcasted_iota(jnp.int32, (3 * LANES, width), 1)
    expand3 = jnp.where((ec >> 6) == eh, 1.0, 0.0).astype(BF16)
    dt_x = _dot(jnp.concatenate(_split3(dt), axis=1), expand3)
    cum_x = _dot(jnp.concatenate(_split3(cum), axis=1), expand3)
    cum_lx = cum_x[L - 1:L, :]
    xdt = xs * dt_x
    xdd = (xdt * jnp.exp(cum_lx - cum_x)).astype(BF16)
    xdt_b = xdt.astype(BF16)
    ecum = jnp.exp(cum_x)
    e_tot = jnp.exp(cum_lx)

    tril = (lax.broadcasted_iota(jnp.int32, (L, L), 1) <= lax.broadcasted_iota(jnp.int32, (L, L), 0))
    lane = lax.broadcasted_iota(jnp.int32, (1, LANES), 1)
    m0 = lane < HEAD_DIM
    hpg = M2_HEADS // M2_GROUPS
    gw = hpg * HEAD_DIM
    groups = range(M2_GROUPS)
    heads = range(M2_HEADS)
    bm_f = [xc[:, width + g * gs:width + (g + 1) * gs] for g in groups]
    bm = [b.astype(BF16) for b in bm_f]
    cm = [xc[:, width + (M2_GROUPS + g) * gs:width + (M2_GROUPS + g + 1) * gs].astype(BF16) for g in groups]
    cbm = [jnp.where(tril, _dot_nt(cm[g], bm[g]), 0.0) for g in groups]
    st = [st_ref[:, g * gw:(g + 1) * gw] for g in groups]
    y_off = [_dot(cm[g], st[g].astype(BF16)) for g in groups]
    for g in groups:
        st_ref[:, g * gw:(g + 1) * gw] = (st[g] * e_tot[:, g * gw:(g + 1) * gw]
                                          + _dot(bm_f[g].T.astype(BF16), xdd[:, g * gw:(g + 1) * gw]))
    mh = [(cbm[h // hpg] * jnp.exp(jnp.minimum(cum[:, h:h + 1] - cum_t[h:h + 1, :], 0.0))).astype(BF16)
          for h in heads]
    zero = jnp.zeros((), BF16)
    yd = [_dot(mh[h], jnp.where(m0 if h % 2 == 0 else jnp.logical_not(m0),
                                xdt_b[:, (h // 2) * LANES:(h // 2 + 1) * LANES], zero)) for h in heads]
    y_diag = jnp.concatenate([yd[2 * p] + yd[2 * p + 1] for p in range(M2_HEADS // 2)], axis=1)
    y = y_diag + jnp.concatenate(y_off, axis=1) * ecum + xs * nw_ref[1:2, :]
    y = y * _silu(zc_ref[...])
    gw = width // M2_GROUPS
    outs = []
    for g in range(M2_GROUPS):
        yg = y[:, g * gw:(g + 1) * gw]
        outs.append(_rms(yg, nw_ref[0:1, g * gw:(g + 1) * gw]))
    out_ref[...] = jnp.concatenate(outs, axis=1)


def _ssd(xbc, misc, z, cw, vec, nw, batch, seq):
    L = M2_CHUNK
    t, cd = xbc.shape
    nc = seq // L
    width = M2_HEADS * HEAD_DIM
    return pl.pallas_call(
        functools.partial(_ssd_kernel, L=L),
        grid=(batch, nc),
        in_specs=[pl.BlockSpec((L, cd), lambda b, c: (b * nc + c, 0)),
                  pl.BlockSpec((L, misc.shape[1]), lambda b, c: (b * nc + c, 0)),
                  pl.BlockSpec((L, width), lambda b, c: (b * nc + c, 0)),
                  pl.BlockSpec((SUBLANES, cd), lambda b, c: (0, 0)),
                  pl.BlockSpec((SUBLANES, LANES), lambda b, c: (0, 0)),
                  pl.BlockSpec((SUBLANES, width), lambda b, c: (0, 0))],
        out_specs=pl.BlockSpec((L, width), lambda b, c: (b * nc + c, 0)),
        out_shape=jax.ShapeDtypeStruct((t, width), F32),
        scratch_shapes=[pltpu.VMEM((M2_STATE, width), F32), pltpu.VMEM((SUBLANES, cd), F32),
                        pltpu.VMEM((L + SUBLANES, cd), F32)],
        compiler_params=_params("arbitrary", "arbitrary"),
        name="ssd",
    )(xbc, misc, z, cw, vec, nw)


def _mla_prep_kernel(cq_ref, ckv_ref, misc_ref, cos_ref, sin_ref, qnw_ref, kvnw_ref, wq_ref, wkv_ref,
                     q_ref, k_ref, v_ref):
    nh = MLA_HEADS
    cos, sin = cos_ref[...], sin_ref[...]
    q = _dot(_rms(cq_ref[...], qnw_ref[...]).astype(BF16), wq_ref[...])
    scale = (MLA_NOPE + MLA_ROPE) ** -0.5 * math.log2(math.e)
    rope0 = nh * MLA_NOPE
    rot0 = rope0 + nh * MLA_ROPE
    for pair in range(nh // 2):
        sl = slice(pair * LANES, (pair + 1) * LANES)
        qr = (q[:, rope0:rot0][:, sl] * cos + q[:, rot0:][:, sl] * sin) * scale
        for hh in range(2):
            h = 2 * pair + hh
            q_ref[:, h * 2 * LANES:h * 2 * LANES + LANES] = (
                q[:, h * MLA_NOPE:(h + 1) * MLA_NOPE] * scale).astype(q_ref.dtype)
            q_ref[:, h * 2 * LANES + LANES:(h + 1) * 2 * LANES] = qr.astype(q_ref.dtype)
    kv = _dot(_rms(ckv_ref[...], kvnw_ref[...]).astype(BF16), wkv_ref[...])
    kpe = misc_ref[:, :LANES] * cos + misc_ref[:, LANES:2 * LANES] * sin
    lane = lax.broadcasted_iota(jnp.int32, (1, LANES), 1)
    slots = [jnp.where(lane < MLA_ROPE, kpe, 0.0), jnp.where(lane < MLA_ROPE, 0.0, kpe)]
    for h in range(nh):
        k_ref[:, h * 2 * LANES:h * 2 * LANES + LANES] = kv[:, h * MLA_NOPE:(h + 1) * MLA_NOPE].astype(k_ref.dtype)
        k_ref[:, h * 2 * LANES + LANES:(h + 1) * 2 * LANES] = slots[h % 2].astype(k_ref.dtype)
    v_ref[...] = kv[:, nh * MLA_NOPE:].astype(v_ref.dtype)


def _mla_prep(cq, ckv, misc, cos, sin, qnw, kvnw, wq, wkv, tm=256):
    t = cq.shape[0]
    nh = MLA_HEADS
    row = lambda n: pl.BlockSpec((tm, n), lambda i: (i, 0))
    full = lambda a: pl.BlockSpec(a.shape, lambda i: (0, 0))
    qnw, kvnw = qnw.reshape(1, -1), kvnw.reshape(1, -1)
    return pl.pallas_call(
        _mla_prep_kernel,
        grid=(t // tm,),
        in_specs=[row(cq.shape[1]), row(ckv.shape[1]), row(misc.shape[1]), row(LANES), row(LANES),
                  full(qnw), full(kvnw), full(wq), full(wkv)],
        out_specs=[row(nh * 2 * LANES), row(nh * 2 * LANES), row(nh * MLA_V)],
        out_shape=[jax.ShapeDtypeStruct((t, nh * 2 * LANES), BF16),
                   jax.ShapeDtypeStruct((t, nh * 2 * LANES), BF16),
                   jax.ShapeDtypeStruct((t, nh * MLA_V), BF16)],
        compiler_params=_params("arbitrary"),
        name="mla_prep",
    )(cq, ckv, misc, cos, sin, qnw, kvnw, wq, wkv)


def _flash_kernel(q_ref, k_ref, v_ref, o_ref, *, tq, hb):
    qi = pl.program_id(2)
    qw, vw = 2 * LANES, MLA_V
    qs = [q_ref[:, h * qw:(h + 1) * qw] for h in range(hb)]

    def block(j, carries, mask):
        off = pl.multiple_of(j * tq, tq)
        hs = range(hb)
        s = [_dot_nt(qs[h], k_ref[pl.ds(off, tq), h * qw:(h + 1) * qw]) for h in hs]
        if mask is not None:
            s = [jnp.where(mask, s[h], -jnp.inf) for h in hs]
        m_new = [jnp.maximum(carries[h][0], jnp.max(s[h], axis=-1, keepdims=True)) for h in hs]
        alpha = [jnp.exp2(carries[h][0] - m_new[h]) for h in hs]
        p = [jnp.exp2(s[h] - m_new[h]) for h in hs]
        l_new = [alpha[h] * carries[h][1] + jnp.sum(p[h], axis=-1, keepdims=True) for h in hs]
        pv = [_dot(p[h].astype(BF16), v_ref[pl.ds(off, tq), h * vw:(h + 1) * vw]) for h in hs]
        return tuple((m_new[h], l_new[h], alpha[h] * carries[h][2] + pv[h]) for h in hs)

    init = tuple((jnp.full((tq, 1), -jnp.inf, F32), jnp.zeros((tq, 1), F32), jnp.zeros((tq, vw), F32))
                 for _ in range(hb))
    carries = lax.fori_loop(0, qi, lambda j, c: block(j, c, None), init)
    ri = lax.broadcasted_iota(jnp.int32, (tq, tq), 0)
    ci = lax.broadcasted_iota(jnp.int32, (tq, tq), 1)
    carries = block(qi, carries, ci <= ri)
    for h in range(hb):
        _, l_fin, acc = carries[h]
        o_ref[:, h * vw:(h + 1) * vw] = (acc / l_fin).astype(o_ref.dtype)


def _flash(q, k, v, batch, seq, tq=512, hb=2):
    t = q.shape[0]
    tq = min(tq, seq)
    nq = seq // tq
    ng = MLA_HEADS // hb
    return pl.pallas_call(
        functools.partial(_flash_kernel, tq=tq, hb=hb),
        grid=(batch, ng, nq),
        in_specs=[pl.BlockSpec((tq, hb * 2 * LANES), lambda b, g, i: (b * nq + i, g)),
                  pl.BlockSpec((seq, hb * 2 * LANES), lambda b, g, i: (b, g)),
                  pl.BlockSpec((seq, hb * MLA_V), lambda b, g, i: (b, g))],
        out_specs=pl.BlockSpec((tq, hb * MLA_V), lambda b, g, i: (b * nq + i, g)),
        out_shape=jax.ShapeDtypeStruct((t, MLA_HEADS * MLA_V), F32),
        compiler_params=_params("arbitrary", "arbitrary", "arbitrary"),
        name="mla_flash",
    )(q, k, v)


def _finish(h_ref, acc, fnw_ref, o_ref, final):
    hn = h_ref[...] + acc
    o_ref[...] = _rms(hn, fnw_ref[...]) if final else hn


def _even_tail_kernel(a_ref, ys_ref, u_ref, q_ref, z_ref, h_ref, mk_ref, mv_ref, vec_ref, gw_ref, ow_ref,
                      fnw_ref, o_ref, *, final):
    w = RW_WIDTH
    z = z_ref[...]
    ga = (a_ref[...] * _silu(z[:, :w])).astype(BF16)
    yb = _gelu_tanh(ys_ref[...] + vec_ref[0:1, :] * u_ref[...])
    gate = _sigmoid(_dot(yb.astype(BF16), gw_ref[...]) + vec_ref[1:2, :])
    gb = (yb * gate * _silu(z[:, w:2 * w])).astype(BF16)
    gm = (_mem_attend(q_ref[...], mk_ref[0], mv_ref[0]) * _silu(z[:, 2 * w:])).astype(BF16)
    acc = _dot(ga, ow_ref[0:w, :]) + _dot(gb, ow_ref[w:2 * w, :]) + _dot(gm, ow_ref[2 * w:, :])
    _finish(h_ref, acc, fnw_ref, o_ref, final)


def _odd_tail_kernel(c_ref, d_ref, q_ref, z_ref, h_ref, mk_ref, mv_ref, ow_ref, fnw_ref, o_ref, *, final):
    w = D_MODEL
    z = z_ref[...]
    gc = c_ref[...].astype(BF16)
    gd = (d_ref[...] * _silu(z[:, :w])).astype(BF16)
    gm = (_mem_attend(q_ref[...], mk_ref[0], mv_ref[0]) * _silu(z[:, w:])).astype(BF16)
    acc = _dot(gc, ow_ref[0:w, :]) + _dot(gd, ow_ref[w:2 * w, :]) + _dot(gm, ow_ref[2 * w:, :])
    _finish(h_ref, acc, fnw_ref, o_ref, final)


def _tail_call(kernel, rows, consts, h, mk, mv, fnw, batch, seq, final, name, tm=256):
    t, d = h.shape
    nt = seq // tm
    row = lambda a: pl.BlockSpec((tm, a.shape[1]), lambda b, i: (b * nt + i, 0))
    full = lambda a: pl.BlockSpec(a.shape, lambda b, i: (0,) * a.ndim)
    memspec = pl.BlockSpec((1,) + mk.shape[1:], lambda b, i: (b, 0, 0))
    fnw = fnw.reshape(1, d)
    return pl.pallas_call(
        functools.partial(kernel, final=final),
        grid=(batch, nt),
        in_specs=[row(a) for a in rows] + [row(h), memspec, memspec] + [full(a) for a in consts] + [full(fnw)],
        out_specs=row(h),
        out_shape=jax.ShapeDtypeStruct((t, d), F32),
        compiler_params=_params("arbitrary", "arbitrary"),
        name=name,
    )(*rows, h, mk, mv, *consts, fnw)


def _pad_rows(rows, n=SUBLANES):
    width = rows[0].shape[-1]
    return jnp.concatenate([r.reshape(1, width) for r in rows] + [jnp.zeros((n - len(rows), width), F32)], axis=0)


def _even_layer(h, batch, seq, mk, mv, nw, in_w, out_w, fnw, final, mu, w0, w2, a0, a2, k_k, k_a, r_k,
                ln_w, ln_b, s5_ops, s5_d, glu_w, glu_b):
    rw_proj = 3 * RW_WIDTH + 2 * RW_LORA
    ev_width = 2 * RW_WIDTH + MEM_WIDTH
    splits = (rw_proj, RW_WIDTH, MEM_WIDTH, ev_width)
    p, u, q_mem, z = _norm_proj(h, nw, in_w.astype(BF16), splits, (F32, F32, F32, F32))
    rvec, wa = _rwkv_param_pack(w0, w2, a0, a2, k_k, k_a, r_k, ln_w, ln_b)
    a_out = _rwkv(p, mu, rvec, wa, batch, seq)
    ys = _s5(u, *s5_ops, batch, seq)

    vec = _pad_rows([s5_d, glu_b])
    return _tail_call(_even_tail_kernel, [a_out, ys, u, q_mem, z], [vec, glu_w.astype(BF16), out_w.astype(BF16)],
                      h, mk, mv, fnw, batch, seq, final, "even_tail")


def _odd_layer(h, batch, seq, mk, mv, cos, sin, nw, in_w, out_w, fnw, final, conv_w, conv_b, dt_bias, a_log,
               m2_d, m2_norm_w, q_norm_w, wq_up, kv_norm_w, wkv_up):
    width = M2_HEADS * HEAD_DIM
    conv_dim = width + 2 * M2_GROUPS * M2_STATE
    nh = MLA_HEADS
    o_dt = conv_dim
    o_cq = o_dt + M2_HEADS
    o_ckv = o_cq + MLA_Q_RANK
    o_kr = o_ckv + MLA_KV_RANK
    o_qm = o_kr + MLA_ROPE
    o_z = o_qm + MEM_WIDTH
    half = MLA_ROPE // 2
    w_kr = in_w[:, o_kr:o_qm]
    w_kr_sw = jnp.concatenate([w_kr[:, half:], w_kr[:, :half]], axis=1)
    w_misc = jnp.concatenate([w_kr, w_kr, w_kr_sw, w_kr_sw, in_w[:, o_dt:o_cq],
                              jnp.zeros((in_w.shape[0], LANES - M2_HEADS), in_w.dtype)], axis=1)
    w_all = jnp.concatenate([in_w[:, :o_dt], in_w[:, o_cq:o_ckv], in_w[:, o_ckv:o_kr], in_w[:, o_qm:o_z],
                             in_w[:, o_z:], w_misc], axis=1).astype(BF16)
    splits = (conv_dim, MLA_Q_RANK, MLA_KV_RANK, MEM_WIDTH, width, D_MODEL + MEM_WIDTH, 3 * LANES)
    xbc, cq, ckv, q_mem, z_c, z_dm, misc = _norm_proj(h, nw, w_all, splits, (F32,) * 7)

    cw = _pad_rows(list(conv_w) + [conv_b])
    pad16 = lambda x: jnp.concatenate([x, jnp.zeros((LANES - M2_HEADS,), F32)])
    vec = _pad_rows([pad16(dt_bias), pad16(-jnp.exp(a_log))])
    nwd = _pad_rows([m2_norm_w, jnp.repeat(m2_d, HEAD_DIM)])
    c_out = _ssd(xbc, misc, z_c, cw, vec, nwd, batch, seq)

    wq = wq_up.reshape(MLA_Q_RANK, nh, MLA_NOPE + MLA_ROPE)
    wq_r = wq[:, :, MLA_NOPE:]
    wq_sw = jnp.concatenate([wq_r[:, :, half:], wq_r[:, :, :half]], axis=2)
    wq_all = jnp.concatenate([wq[:, :, :MLA_NOPE].reshape(MLA_Q_RANK, -1), wq_r.reshape(MLA_Q_RANK, -1),
                              wq_sw.reshape(MLA_Q_RANK, -1)], axis=1).astype(BF16)
    wkv = wkv_up.reshape(MLA_KV_RANK, nh, MLA_NOPE + MLA_V)
    wkv_all = jnp.concatenate([wkv[:, :, :MLA_NOPE].reshape(MLA_KV_RANK, -1),
                               wkv[:, :, MLA_NOPE:].reshape(MLA_KV_RANK, -1)], axis=1).astype(BF16)
    qc, kc, vv = _mla_prep(cq, ckv, misc, cos, sin, q_norm_w, kv_norm_w, wq_all, wkv_all)
    d_attn = _flash(qc, kc, vv, batch, seq)

    return _tail_call(_odd_tail_kernel, [c_out, d_attn, q_mem, z_dm], [out_w.astype(BF16)],
                      h, mk, mv, fnw, batch, seq, final, "odd_tail")


def _rope_tables(positions):
    inv = 1.0 / (ROPE_THETA ** (jnp.arange(0, MLA_ROPE, 2, dtype=F32) / MLA_ROPE))
    ang = positions.astype(F32).reshape(-1, 1) * inv
    cos, sin = jnp.cos(ang), jnp.sin(ang)
    return jnp.tile(cos, (1, 4)), jnp.tile(jnp.concatenate([-sin, sin], axis=1), (1, 2))


def kernel(x, mem, positions, norm_w, mem_norm_w, final_norm_w, mem_kv_w, ev_in_w, ev_out_w, rw_mu, rw_w0, rw_w2, rw_a0, rw_a2, rw_k_k, rw_k_a, rw_r_k, rw_ln_w, rw_ln_b, s5_lambda_re, s5_lambda_im, s5_b_re, s5_b_im, s5_c_re, s5_c_im, s5_d, s5_log_dt, s5_glu_w, s5_glu_b, od_in_w, od_out_w, m2_conv_w, m2_conv_b, m2_dt_bias, m2_a_log, m2_d, m2_norm_w, mla_q_norm_w, mla_wq_up, mla_kv_norm_w, mla_wkv_up):
    batch, seq, d = x.shape
    depth = norm_w.shape[0]
    mk, mv = _mem_kv(mem, mem_norm_w, mem_kv_w.astype(BF16))
    cos, sin = _rope_tables(positions)
    s5_ops = jax.vmap(functools.partial(_s5_param_pack, nj=seq // S5_SUB))(
        s5_lambda_re, s5_lambda_im, s5_b_re, s5_b_im, s5_c_re, s5_c_im, s5_log_dt)
    ev_in_w, ev_out_w, s5_glu_w = ev_in_w.astype(BF16), ev_out_w.astype(BF16), s5_glu_w.astype(BF16)
    od_out_w = od_out_w.astype(BF16)
    h = x.reshape(batch * seq, d)
    for layer in range(depth):
        i = layer // 2
        final = layer == depth - 1
        if layer % 2 == 0:
            h = _even_layer(h, batch, seq, mk[layer], mv[layer], norm_w[layer], ev_in_w[i], ev_out_w[i],
                            final_norm_w, final, rw_mu[i], rw_w0[i], rw_w2[i], rw_a0[i], rw_a2[i], rw_k_k[i],
                            rw_k_a[i], rw_r_k[i], rw_ln_w[i], rw_ln_b[i], tuple(op[i] for op in s5_ops), s5_d[i],
                            s5_glu_w[i], s5_glu_b[i])
        else:
            h = _odd_layer(h, batch, seq, mk[layer], mv[layer], cos, sin, norm_w[layer], od_in_w[i], od_out_w[i],
                           final_norm_w, final, m2_conv_w[i], m2_conv_b[i], m2_dt_bias[i], m2_a_log[i], m2_d[i],
                           m2_norm_w[i], mla_q_norm_w[i], mla_wq_up[i], mla_kv_norm_w[i], mla_wkv_up[i])
    return h.reshape(batch, seq, d)
```

```python
import functools
import math

import jax
import jax.numpy as jnp
from jax import lax
from jax.experimental import pallas as pl
from jax.experimental.pallas import tpu as pltpu

F32 = jnp.float32
BF16 = jnp.bfloat16
HIGHEST = lax.Precision.HIGHEST

LANES = 128
SUBLANES = 8
VMEM_LIMIT_BYTES = 56 * 1024 * 1024

D_MODEL = 1024
HEAD_DIM = 64
NORM_EPS = 1e-6
RW_WIDTH = 1024
RW_LORA = 64
RW_GN_EPS = 64e-5
RW_CHUNK = 64
S5_GROUP = 16
S5_STATE = 64
S5_SUB = 16
M2_HEADS = 16
M2_GROUPS = 2
M2_STATE = 128
M2_CONV = 4
M2_CHUNK = 128
MLA_HEADS = 8
MLA_NOPE = 128
MLA_ROPE = 64
MLA_V = 128
MLA_Q_RANK = 384
MLA_KV_RANK = 256
ROPE_THETA = 10000.0
MEM_HEADS = 4
MEM_WIDTH = MEM_HEADS * HEAD_DIM


def _dot(a, b, precision=None):
    return jnp.dot(a, b, preferred_element_type=F32, precision=precision)


def _dot_nt(a, b, precision=None):
    return lax.dot_general(a, b, (((1,), (1,)), ((), ())), preferred_element_type=F32,
                           precision=precision)


def _sigmoid(x):
    return 1.0 / (1.0 + jnp.exp(-x))


def _silu(x):
    return x * _sigmoid(x)


def _softplus(x):
    return jnp.maximum(x, 0.0) + jnp.log(1.0 + jnp.exp(-jnp.abs(x)))


def _gelu_tanh(x):
    return 0.5 * x * (1.0 + jnp.tanh(math.sqrt(2.0 / math.pi) * (x + 0.044715 * (x * x * x))))


def _rms(x, w, eps=NORM_EPS):
    ms = jnp.mean(x * x, axis=-1, keepdims=True)
    return x * lax.rsqrt(ms + eps) * w


def _params(*sem):
    return pltpu.CompilerParams(dimension_semantics=sem, vmem_limit_bytes=VMEM_LIMIT_BYTES)


def _norm_proj_kernel(x_ref, nw_ref, w_ref, *out_refs, splits):
    xn = _rms(x_ref[...], nw_ref[...]).astype(BF16)
    off = 0
    for o_ref, n in zip(out_refs, splits):
        o_ref[...] = _dot(xn, w_ref[:, off:off + n]).astype(o_ref.dtype)
        off += n


def _norm_proj(x, nw, w, splits, dtypes, tm=256):
    t, d = x.shape
    n = w.shape[1]
    assert sum(splits) == n and t % tm == 0
    return pl.pallas_call(
        functools.partial(_norm_proj_kernel, splits=splits),
        grid=(t // tm,),
        in_specs=[pl.BlockSpec((tm, d), lambda i: (i, 0)),
                  pl.BlockSpec((1, d), lambda i: (0, 0)),
                  pl.BlockSpec((d, n), lambda i: (0, 0))],
        out_specs=[pl.BlockSpec((tm, s), lambda i: (i, 0)) for s in splits],
        out_shape=[jax.ShapeDtypeStruct((t, s), dt) for s, dt in zip(splits, dtypes)],
        compiler_params=_params("arbitrary"),
        name="norm_proj",
    )(x, nw.reshape(1, d), w)


def _mem_kv_kernel(mem_ref, nw_ref, w_ref, k_ref, v_ref):
    mn = _rms(mem_ref[0], nw_ref[...]).astype(BF16)
    kv = _dot(mn, w_ref[0])
    k_ref[0, 0] = kv[:, :MEM_WIDTH].astype(k_ref.dtype)
    v_ref[0, 0] = kv[:, MEM_WIDTH:].astype(v_ref.dtype)


def _mem_kv(mem, nw, w):
    b, m, d = mem.shape
    depth = w.shape[0]
    shp = jax.ShapeDtypeStruct((depth, b, m, MEM_WIDTH), BF16)
    return pl.pallas_call(
        _mem_kv_kernel,
        grid=(depth, b),
        in_specs=[pl.BlockSpec((1, m, d), lambda l, i: (i, 0, 0)),
                  pl.BlockSpec((1, d), lambda l, i: (0, 0)),
                  pl.BlockSpec((1, d, 2 * MEM_WIDTH), lambda l, i: (l, 0, 0))],
        out_specs=[pl.BlockSpec((1, 1, m, MEM_WIDTH), lambda l, i: (l, i, 0, 0))] * 2,
        out_shape=[shp, shp],
        compiler_params=_params("arbitrary", "arbitrary"),
        name="mem_kv",
    )(mem, nw.reshape(1, d), w)


def _mem_attend(q, k, v):
    lane = lax.broadcasted_iota(jnp.int32, (1, MEM_WIDTH), 1)
    qb = (q * (HEAD_DIM ** -0.5)).astype(BF16)
    out = jnp.zeros(q.shape, F32)
    for h in range(MEM_HEADS):
        hm = (lane >> 6) == h
        sc = _dot_nt(qb, jnp.where(hm, k, jnp.zeros_like(k)))
        sc = sc - jnp.max(sc, axis=-1, keepdims=True)
        p = jnp.exp(sc)
        p = p / jnp.sum(p, axis=-1, keepdims=True)
        out = out + _dot(p.astype(BF16), jnp.where(hm, v, jnp.zeros_like(v)))
    return out


def _split3(x):
    hi = x.astype(BF16)
    r1 = x - hi.astype(F32)
    mid = r1.astype(BF16)
    lo = (r1 - mid.astype(F32)).astype(BF16)
    return hi, mid, lo


def _rwkv_kernel(p_ref, mu_ref, vec_ref, wa_ref, out_ref, st_ref, prev_ref, *, L):
    c = pl.program_id(1)
    W = RW_WIDTH

    @pl.when(c == 0)
    def _():
        st_ref[...] = jnp.zeros_like(st_ref)
        prev_ref[...] = jnp.zeros_like(prev_ref)

    x = p_ref[...]
    R = x.shape[0]
    nck = R // L
    row_l = lax.broadcasted_iota(jnp.int32, (R, 1), 0)
    xs = jnp.where(row_l == 0, prev_ref[0:1, :], pltpu.roll(x, 1, axis=0))
    prev_ref[0:1, :] = x[R - 1:R, :]
    xm = x + (xs - x) * mu_ref[...]
    r, k, v, lat = xm[:, :W], xm[:, W:2 * W], xm[:, 2 * W:3 * W], xm[:, 3 * W:]

    w0, a0, k_k, k_a, r_k, ln_w, ln_b = [vec_ref[i:i + 1, :] for i in range(7)]
    lane = lax.broadcasted_iota(jnp.int32, (1, LANES), 1)
    lat_t = jnp.where(lane < RW_LORA, jnp.tanh(lat), lat).astype(BF16)
    la = _dot(lat_t, wa_ref[...])
    log_w = -_softplus(-(w0 + la[:, :W])) - 0.5
    lw = -jnp.exp(log_w)
    iclr = _sigmoid(a0 + la[:, W:])
    kk0 = k * k_k
    kp = k * (1.0 + (iclr - 1.0) * k_a)
    rkr = r * kp * r_k

    il = lax.broadcasted_iota(jnp.int32, (R, 3 * R), 0)
    jl = lax.broadcasted_iota(jnp.int32, (R, 3 * R), 1) & (R - 1)
    same_chunk = (il & -L) == (jl & -L)
    tri3 = jnp.where(same_chunk & (jl <= il), 1.0, 0.0).astype(BF16)
    cum = _dot(tri3, jnp.concatenate(_split3(lw), axis=0))
    cum_ls = [cum[(ck + 1) * L - 1:(ck + 1) * L, :] for ck in range(nck)]
    cum_end = jnp.concatenate([jnp.broadcast_to(cl, (L, W)) for cl in cum_ls], axis=0)
    e_neg = jnp.exp(-cum)
    e_exc = jnp.exp(cum - lw)
    e_inc = jnp.exp(cum)
    dend = jnp.exp(cum_end - cum)
    e_tot = [jnp.exp(cl) for cl in cum_ls]

    i2 = lax.broadcasted_iota(jnp.int32, (LANES, LANES), 0)
    j2 = lax.broadcasted_iota(jnp.int32, (LANES, LANES), 1)
    same_head = (i2 >> 6) == (j2 >> 6)
    head_ones = jnp.where(same_head, 1.0, 0.0).astype(BF16)
    strict = same_head & (j2 < i2)
    incl = same_head & (j2 <= i2)
    eye = jnp.where(i2 == j2, 1.0, 0.0)
    levels = [((i2 & -(2 * s)) == (j2 & -(2 * s))) & ((i2 & s) != 0) & ((j2 & s) == 0)
              for s in (1, 2, 4, 8, 16, 32)]
    m0 = lane < HEAD_DIM
    stack = lambda t: jnp.concatenate([jnp.where(m0, t, 0.0), jnp.where(m0, 0.0, t)], axis=0)

    npair = W // LANES
    units = [(slice(ck * L, (ck + 1) * L), slice(p * LANES, (p + 1) * LANES))
             for ck in range(nck) for p in range(npair)]
    each = lambda f: [f(i) for i in range(len(units))]
    at = lambda arr, i: arr[units[i][0], units[i][1]]
    sums = each(lambda i: _dot(jnp.concatenate([at(kk0, i) * at(kk0, i), at(rkr, i)], axis=0).astype(BF16),
                               head_ones))
    kk = each(lambda i: at(kk0, i) * lax.rsqrt(jnp.maximum(sums[i][:L], 1e-24)))
    bvec = each(lambda i: kk[i] * at(iclr, i))
    ar = each(lambda i: jnp.concatenate([stack(-kk[i] * at(e_exc, i)), stack(at(r, i) * at(e_inc, i))],
                                        axis=0).astype(BF16))
    b2 = each(lambda i: stack(bvec[i] * at(e_neg, i)).astype(BF16))
    k2 = each(lambda i: stack(at(kp, i) * at(e_neg, i)).astype(BF16))
    v2f = each(lambda i: stack(at(v, i)))
    v2 = each(lambda i: v2f[i].astype(BF16))
    bk_end = each(lambda i: jnp.concatenate([stack(bvec[i] * at(dend, i)), stack(at(kp, i) * at(dend, i))],
                                            axis=0).astype(BF16))
    sb = each(lambda i: _dot_nt(ar[i], b2[i]))
    sk = each(lambda i: _dot_nt(ar[i], k2[i]))
    aab = each(lambda i: jnp.where(strict, sb[i][:LANES], 0.0))

    inv = each(lambda i: eye + jnp.where(levels[0], aab[i], 0.0))
    for lvl in levels[1:]:
        invb = each(lambda i: inv[i].astype(BF16))
        t1 = each(lambda i: _dot(invb[i], jnp.where(lvl, aab[i], 0.0).astype(BF16)).astype(BF16))
        inv = each(lambda i: inv[i] + _dot(t1[i], invb[i]))
    aak_v = each(lambda i: _dot(jnp.where(strict, sk[i][:LANES], 0.0).astype(BF16), v2[i]))
    rk_v = each(lambda i: _dot(jnp.where(incl, sk[i][LANES:], 0.0).astype(BF16), v2[i]))

    st = [st_ref[p] for p in range(npair)]
    y = [None] * len(units)
    for ck in range(nck):
        ids = [ck * npair + p for p in range(npair)]
        sta = [_dot_nt(ar[i], st[p].astype(BF16)) for p, i in enumerate(ids)]
        u2 = [_dot(inv[i].astype(BF16), (sta[p][:LANES] + aak_v[i]).astype(BF16)) for p, i in enumerate(ids)]
        y2 = [sta[p][LANES:] + _dot(jnp.where(incl, sb[i][LANES:], 0.0).astype(BF16), u2[p].astype(BF16)) + rk_v[i]
              for p, i in enumerate(ids)]
        for p, i in enumerate(ids):
            y[i] = y2[p][:L, :] + y2[p][L:, :]
            uv_t = jnp.concatenate([u2[p].T, v2f[i].T], axis=1).astype(BF16)
            st[p] = st[p] * e_tot[ck][:, units[i][1]] + _dot(uv_t, bk_end[i])
    for p in range(npair):
        st_ref[p] = st[p]

    mean = each(lambda i: _dot(y[i].astype(BF16), head_ones) * (1.0 / HEAD_DIM))
    dcen = each(lambda i: y[i] - mean[i])
    var = each(lambda i: _dot((dcen[i] * dcen[i]).astype(BF16), head_ones) * (1.0 / HEAD_DIM))
    for i, (rs, ls_) in enumerate(units):
        out_ref[rs, ls_] = (dcen[i] * lax.rsqrt(var[i] + RW_GN_EPS) * ln_w[:, ls_] + ln_b[:, ls_]
                            + sums[i][L:] * at(v, i))


def _rwkv(p, mu, vec, wa, batch, seq, nck=2):
    L = RW_CHUNK * (nck if seq % (nck * RW_CHUNK) == 0 else 1)
    t, pw = p.shape
    nc = seq // L
    return pl.pallas_call(
        functools.partial(_rwkv_kernel, L=RW_CHUNK),
        grid=(batch, nc),
        in_specs=[pl.BlockSpec((L, pw), lambda b, c: (b * nc + c, 0)),
                  pl.BlockSpec((1, pw), lambda b, c: (0, 0)),
                  pl.BlockSpec(vec.shape, lambda b, c: (0, 0)),
                  pl.BlockSpec(wa.shape, lambda b, c: (0, 0))],
        out_specs=pl.BlockSpec((L, RW_WIDTH), lambda b, c: (b * nc + c, 0)),
        out_shape=jax.ShapeDtypeStruct((t, RW_WIDTH), F32),
        scratch_shapes=[pltpu.VMEM((RW_WIDTH // LANES, LANES, LANES), F32), pltpu.VMEM((SUBLANES, pw), F32)],
        compiler_params=_params("arbitrary", "arbitrary"),
        name="rwkv7",
    )(p, mu.reshape(1, pw), vec, wa)


def _rwkv_param_pack(w0, w2, a0, a2, k_k, k_a, r_k, ln_w, ln_b):
    vec = _pad_rows([w0, a0, k_k, k_a, r_k.reshape(-1), ln_w, ln_b])
    z = jnp.zeros_like(w2)
    wa = jnp.concatenate([jnp.concatenate([w2, z], axis=1), jnp.concatenate([z, a2], axis=1)], axis=0)
    return vec, wa.astype(BF16)


def _s5_kernel(u_ref, tz_ref, gm_ref, cc_ref, pw_ref, y_ref, tzd_ref, gmd_ref, ccd_ref, *, nj):
    ls, pch, n = S5_SUB, S5_GROUP, S5_STATE
    gs = LANES // pch

    @pl.when(pl.program_id(1) == 0)
    def _():
        tzd_ref[...] = jnp.zeros_like(tzd_ref)
        gmd_ref[...] = jnp.zeros_like(gmd_ref)
        ccd_ref[...] = jnp.zeros_like(ccd_ref)
        for g in range(gs):
            for a in range(ls):
                r0 = a * LANES + g * pch
                for b in range(a, ls):
                    c0 = b * LANES + g * pch
                    tzd_ref[r0:r0 + pch, c0:c0 + pch] = tz_ref[g, a * pch:(a + 1) * pch, b * pch:(b + 1) * pch]
                for c in range(2):
                    gmd_ref[r0:r0 + pch, (c * gs + g) * n:(c * gs + g + 1) * n] = (
                        gm_ref[g, a * pch:(a + 1) * pch, c * n:(c + 1) * n])
                    ccd_ref[(c * gs + g) * n:(c * gs + g + 1) * n, r0:r0 + pch] = (
                        cc_ref[g, c * n:(c + 1) * n, a * pch:(a + 1) * pch])

    rows = u_ref.shape[0] // ls
    xcat = jnp.concatenate([u_ref[pl.ds(l, rows, stride=ls), :] for l in range(ls)], axis=1).astype(BF16)
    y = _dot(xcat, tzd_ref[...])
    x = _dot(xcat, gmd_ref[...])
    half = x.shape[1] // 2
    jrow = lax.broadcasted_iota(jnp.int32, (rows, 1), 0) & (nj - 1)
    pw = pw_ref[0]
    d, lvl = 1, 0
    while d < nj:
        xs = jnp.where(jrow >= d, pltpu.roll(x, d, axis=0), 0.0)
        xsw = jnp.concatenate([xs[:, half:], xs[:, :half]], axis=1)
        x = x + xs * pw[2 * lvl:2 * lvl + 1, :] + xsw * pw[2 * lvl + 1:2 * lvl + 2, :]
        d, lvl = 2 * d, lvl + 1
    h_in = jnp.where(jrow >= 1, pltpu.roll(x, 1, axis=0), 0.0)
    y = y + _dot(h_in.astype(BF16), ccd_ref[...])
    for l in range(ls):
        y_ref[pl.ds(l, rows, stride=ls), :] = y[:, l * LANES:(l + 1) * LANES]


def _s5(u, tz, gm, cc, pw, batch, seq, nb=2):
    t, width = u.shape
    nslab = width // LANES
    gs = LANES // S5_GROUP
    nj = seq // S5_SUB
    sub_w = S5_SUB * LANES
    st_w = 2 * gs * S5_STATE
    pwd = pw.reshape(nslab, gs, pw.shape[1], 2, S5_STATE).transpose(0, 2, 3, 1, 4).reshape(nslab, pw.shape[1], st_w)
    grp_spec = lambda a: pl.BlockSpec((gs,) + a.shape[1:], lambda s, b: (s, 0, 0))
    nb = nb if batch % nb == 0 else 1
    assert nj & (nj - 1) == 0
    return pl.pallas_call(
        functools.partial(_s5_kernel, nj=nj),
        grid=(nslab, batch // nb),
        in_specs=[pl.BlockSpec((nb * seq, LANES), lambda s, b: (b, s)),
                  grp_spec(tz), grp_spec(gm), grp_spec(cc),
                  pl.BlockSpec((1,) + pwd.shape[1:], lambda s, b: (s, 0, 0))],
        out_specs=pl.BlockSpec((nb * seq, LANES), lambda s, b: (b, s)),
        out_shape=jax.ShapeDtypeStruct((t, width), F32),
        scratch_shapes=[pltpu.VMEM((sub_w, sub_w), BF16), pltpu.VMEM((sub_w, st_w), BF16),
                        pltpu.VMEM((st_w, sub_w), BF16)],
        compiler_params=_params("arbitrary", "arbitrary"),
        name="s5",
    )(u, tz, gm, cc, pwd)


def _s5_param_pack(lam_re, lam_im, b_re, b_im, c_re, c_im, log_dt, nj):
    ls = S5_SUB
    dt = jnp.exp(log_dt)[:, None]
    zr, zi = lam_re * dt, lam_im * dt

    def powers(steps):
        st = steps[:, None, None]
        mag = jnp.exp(st * zr)
        return mag * jnp.cos(st * zi), mag * jnp.sin(st * zi)

    ab_re, ab_im = powers(jnp.ones((1,), F32))
    ab_re, ab_im = ab_re[0], ab_im[0]
    den = lam_re * lam_re + lam_im * lam_im
    nr, ni = ab_re - 1.0, ab_im
    f_re = (nr * lam_re + ni * lam_im) / den
    f_im = (ni * lam_re - nr * lam_im) / den
    bb_re = f_re[..., None] * b_re - f_im[..., None] * b_im
    bb_im = f_re[..., None] * b_im + f_im[..., None] * b_re
    pr, pi = powers(jnp.arange(0, ls + 1, dtype=F32))
    cl_re = c_re[None] * pr[:, :, None, :] - c_im[None] * pi[:, :, None, :]
    cl_im = c_re[None] * pi[:, :, None, :] + c_im[None] * pr[:, :, None, :]
    kern = jnp.einsum('dgpn,gnq->dgpq', jnp.concatenate([cl_re[:ls], -cl_im[:ls]], axis=3),
                      jnp.concatenate([bb_re, bb_im], axis=1), precision=HIGHEST)
    lag = jnp.arange(ls)[None, :] - jnp.arange(ls)[:, None]
    tz = jnp.where((lag >= 0)[:, :, None, None, None], kern[jnp.clip(lag, 0, ls - 1)], 0.0)
    g = lam_re.shape[0]
    pch = S5_GROUP
    tz = tz.transpose(2, 0, 4, 1, 3).reshape(g, ls * pch, ls * pch)
    rr, ri = pr[ls - 1::-1][:ls], pi[ls - 1::-1][:ls]
    gm_re = rr[..., None] * bb_re[None] - ri[..., None] * bb_im[None]
    gm_im = rr[..., None] * bb_im[None] + ri[..., None] * bb_re[None]
    gm = jnp.concatenate([gm_re, gm_im], axis=2)
    gm = gm.transpose(1, 0, 3, 2).reshape(g, ls * pch, 2 * S5_STATE)
    cc = jnp.concatenate([cl_re[1:], -cl_im[1:]], axis=3)
    cc = cc.transpose(1, 3, 0, 2).reshape(g, 2 * S5_STATE, ls * pch)
    lv = []
    d = 1
    while d < nj:
        lv.append(float(ls * d))
        d *= 2
    qr, qi = powers(jnp.asarray(lv, F32))
    pw = jnp.stack([jnp.concatenate([qr, qr], axis=-1), jnp.concatenate([-qi, qi], axis=-1)], axis=1)
    pw = pw.reshape(2 * len(lv), g, 2 * S5_STATE).transpose(1, 0, 2)
    pw = jnp.concatenate([pw, jnp.zeros((g, 16 - 2 * len(lv), 2 * S5_STATE), F32)], axis=1)
    return tz.astype(BF16), gm.astype(BF16), cc.astype(BF16), pw


def _ssd_kernel(xbc_ref, misc_ref, zc_ref, cw_ref, vec_ref, nw_ref, out_ref,
                st_ref, halo_ref, buf_ref, *, L):
    c = pl.program_id(1)
    width = M2_HEADS * HEAD_DIM
    gs = M2_STATE

    @pl.when(c == 0)
    def _():
        st_ref[...] = jnp.zeros_like(st_ref)
        halo_ref[...] = jnp.zeros_like(halo_ref)

    xbc = xbc_ref[...]
    buf_ref[0:SUBLANES, :] = halo_ref[...]
    buf_ref[SUBLANES:SUBLANES + L, :] = xbc
    halo_ref[...] = xbc[L - SUBLANES:L, :]
    cw = cw_ref[...]
    nshift = M2_CONV - 1
    si = lax.broadcasted_iota(jnp.int32, (nshift * L, L + SUBLANES), 0)
    sj = lax.broadcasted_iota(jnp.int32, (nshift * L, L + SUBLANES), 1)
    tap_i = (si >= L).astype(jnp.int32) + (si >= 2 * L).astype(jnp.int32)
    shift = jnp.where(sj == (si - tap_i * L) + tap_i + (SUBLANES - nshift), 1.0, 0.0).astype(BF16)
    zsh = _dot(shift, buf_ref[...].astype(BF16))
    acc = cw[M2_CONV:M2_CONV + 1, :] + cw[nshift:M2_CONV, :] * xbc
    for tap in range(nshift):
        acc = acc + cw[tap:tap + 1, :] * zsh[tap * L:(tap + 1) * L, :]
    xc = _silu(acc)
    xs = xc[:, :width]

    vec = vec_ref[...]
    dt = _softplus(misc_ref[:, 2 * LANES:3 * LANES] + vec[0:1, :])
    a_dt = dt * vec[1:2, :]
    il = lax.broadcasted_iota(jnp.int32, (L, 3 * L), 0)
    jl = lax.broadcasted_iota(jnp.int32, (L, 3 * L), 1) & (L - 1)
    tri3 = jnp.where(jl <= il, 1.0, 0.0).astype(BF16)
    cum = _dot(tri3, jnp.concatenate(_split3(a_dt), axis=0))
    cum_t = cum.T

    eh = lax.broadcasted_iota(jnp.int32, (3 * LANES, width), 0) & (LANES - 1)
    ec = lax.broadcasted_iota(jnp.int32, (3 * LANES, width), 1)
    expand3 = jnp.where((ec >> 6) == eh, 1.0, 0.0).astype(BF16)
    dt_x = _dot(jnp.concatenate(_split3(dt), axis=1), expand3)
    cum_x = _dot(jnp.concatenate(_split3(cum), axis=1), expand3)
    cum_lx = cum_x[L - 1:L, :]
    xdt = xs * dt_x
    xdd = (xdt * jnp.exp(cum_lx - cum_x)).astype(BF16)
    xdt_b = xdt.astype(BF16)
    ecum = jnp.exp(cum_x)
    e_tot = jnp.exp(cum_lx)

    tril = (lax.broadcasted_iota(jnp.int32, (L, L), 1) <= lax.broadcasted_iota(jnp.int32, (L, L), 0))
    lane = lax.broadcasted_iota(jnp.int32, (1, LANES), 1)
    m0 = lane < HEAD_DIM
    hpg = M2_HEADS // M2_GROUPS
    gw = hpg * HEAD_DIM
    groups = range(M2_GROUPS)
    heads = range(M2_HEADS)
    bm_f = [xc[:, width + g * gs:width + (g + 1) * gs] for g in groups]
    bm = [b.astype(BF16) for b in bm_f]
    cm = [xc[:, width + (M2_GROUPS + g) * gs:width + (M2_GROUPS + g + 1) * gs].astype(BF16) for g in groups]
    cbm = [jnp.where(tril, _dot_nt(cm[g], bm[g]), 0.0) for g in groups]
    st = [st_ref[:, g * gw:(g + 1) * gw] for g in groups]
    y_off = [_dot(cm[g], st[g].astype(BF16)) for g in groups]
    for g in groups:
        st_ref[:, g * gw:(g + 1) * gw] = (st[g] * e_tot[:, g * gw:(g + 1) * gw]
                                          + _dot(bm_f[g].T.astype(BF16), xdd[:, g * gw:(g + 1) * gw]))
    mh = [(cbm[h // hpg] * jnp.exp(jnp.minimum(cum[:, h:h + 1] - cum_t[h:h + 1, :], 0.0))).astype(BF16)
          for h in heads]
    zero = jnp.zeros((), BF16)
    yd = [_dot(mh[h], jnp.where(m0 if h % 2 == 0 else jnp.logical_not(m0),
                                xdt_b[:, (h // 2) * LANES:(h // 2 + 1) * LANES], zero)) for h in heads]
    y_diag = jnp.concatenate([yd[2 * p] + yd[2 * p + 1] for p in range(M2_HEADS // 2)], axis=1)
    y = y_diag + jnp.concatenate(y_off, axis=1) * ecum + xs * nw_ref[1:2, :]
    y = y * _silu(zc_ref[...])
    gw = width // M2_GROUPS
    outs = []
    for g in range(M2_GROUPS):
        yg = y[:, g * gw:(g + 1) * gw]
        outs.append(_rms(yg, nw_ref[0:1, g * gw:(g + 1) * gw]))
    out_ref[...] = jnp.concatenate(outs, axis=1)


def _ssd(xbc, misc, z, cw, vec, nw, batch, seq):
    L = M2_CHUNK
    t, cd = xbc.shape
    nc = seq // L
    width = M2_HEADS * HEAD_DIM
    return pl.pallas_call(
        functools.partial(_ssd_kernel, L=L),
        grid=(batch, nc),
        in_specs=[pl.BlockSpec((L, cd), lambda b, c: (b * nc + c, 0)),
                  pl.BlockSpec((L, misc.shape[1]), lambda b, c: (b * nc + c, 0)),
                  pl.BlockSpec((L, width), lambda b, c: (b * nc + c, 0)),
                  pl.BlockSpec((SUBLANES, cd), lambda b, c: (0, 0)),
                  pl.BlockSpec((SUBLANES, LANES), lambda b, c: (0, 0)),
                  pl.BlockSpec((SUBLANES, width), lambda b, c: (0, 0))],
        out_specs=pl.BlockSpec((L, width), lambda b, c: (b * nc + c, 0)),
        out_shape=jax.ShapeDtypeStruct((t, width), F32),
        scratch_shapes=[pltpu.VMEM((M2_STATE, width), F32), pltpu.VMEM((SUBLANES, cd), F32),
                        pltpu.VMEM((L + SUBLANES, cd), F32)],
        compiler_params=_params("arbitrary", "arbitrary"),
        name="ssd",
    )(xbc, misc, z, cw, vec, nw)


def _mla_prep_kernel(cq_ref, ckv_ref, misc_ref, cos_ref, sin_ref, qnw_ref, kvnw_ref, wq_ref, wkv_ref,
                     q_ref, k_ref, v_ref):
    nh = MLA_HEADS
    cos, sin = cos_ref[...], sin_ref[...]
    q = _dot(_rms(cq_ref[...], qnw_ref[...]).astype(BF16), wq_ref[...])
    scale = (MLA_NOPE + MLA_ROPE) ** -0.5 * math.log2(math.e)
    rope0 = nh * MLA_NOPE
    rot0 = rope0 + nh * MLA_ROPE
    for pair in range(nh // 2):
        sl = slice(pair * LANES, (pair + 1) * LANES)
        qr = (q[:, rope0:rot0][:, sl] * cos + q[:, rot0:][:, sl] * sin) * scale
        for hh in range(2):
            h = 2 * pair + hh
            q_ref[:, h * 2 * LANES:h * 2 * LANES + LANES] = (
                q[:, h * MLA_NOPE:(h + 1) * MLA_NOPE] * scale).astype(q_ref.dtype)
            q_ref[:, h * 2 * LANES + LANES:(h + 1) * 2 * LANES] = qr.astype(q_ref.dtype)
    kv = _dot(_rms(ckv_ref[...], kvnw_ref[...]).astype(BF16), wkv_ref[...])
    kpe = misc_ref[:, :LANES] * cos + misc_ref[:, LANES:2 * LANES] * sin
    lane = lax.broadcasted_iota(jnp.int32, (1, LANES), 1)
    slots = [jnp.where(lane < MLA_ROPE, kpe, 0.0), jnp.where(lane < MLA_ROPE, 0.0, kpe)]
    for h in range(nh):
        k_ref[:, h * 2 * LANES:h * 2 * LANES + LANES] = kv[:, h * MLA_NOPE:(h + 1) * MLA_NOPE].astype(k_ref.dtype)
        k_ref[:, h * 2 * LANES + LANES:(h + 1) * 2 * LANES] = slots[h % 2].astype(k_ref.dtype)
    v_ref[...] = kv[:, nh * MLA_NOPE:].T.astype(v_ref.dtype)


def _mla_prep(cq, ckv, misc, cos, sin, qnw, kvnw, wq, wkv, tm=256):
    t = cq.shape[0]
    nh = MLA_HEADS
    row = lambda n: pl.BlockSpec((tm, n), lambda i: (i, 0))
    full = lambda a: pl.BlockSpec(a.shape, lambda i: (0, 0))
    qnw, kvnw = qnw.reshape(1, -1), kvnw.reshape(1, -1)
    return pl.pallas_call(
        _mla_prep_kernel,
        grid=(t // tm,),
        in_specs=[row(cq.shape[1]), row(ckv.shape[1]), row(misc.shape[1]), row(LANES), row(LANES),
                  full(qnw), full(kvnw), full(wq), full(wkv)],
        out_specs=[row(nh * 2 * LANES), row(nh * 2 * LANES), pl.BlockSpec((nh * MLA_V, tm), lambda i: (0, i))],
        out_shape=[jax.ShapeDtypeStruct((t, nh * 2 * LANES), BF16),
                   jax.ShapeDtypeStruct((t, nh * 2 * LANES), BF16),
                   jax.ShapeDtypeStruct((nh * MLA_V, t), BF16)],
        compiler_params=_params("arbitrary"),
        name="mla_prep",
    )(cq, ckv, misc, cos, sin, qnw, kvnw, wq, wkv)


def _flash_kernel(q_ref, k_ref, v_ref, o_ref, *, tq, hb):
    qi = pl.program_id(2)
    qw, vw = 2 * LANES, MLA_V
    qs = [q_ref[:, h * qw:(h + 1) * qw] for h in range(hb)]

    def block(j, carries, mask):
        off = pl.multiple_of(j * tq, tq)
        hs = range(hb)
        s = [_dot_nt(k_ref[pl.ds(off, tq), h * qw:(h + 1) * qw], qs[h]) for h in hs]
        if mask is not None:
            s = [jnp.where(mask, s[h], -jnp.inf) for h in hs]
        m_new = [jnp.maximum(carries[h][0], jnp.max(s[h], axis=0, keepdims=True)) for h in hs]
        alpha = [jnp.exp2(carries[h][0] - m_new[h]) for h in hs]
        p = [jnp.exp2(s[h] - m_new[h]) for h in hs]
        l_new = [alpha[h] * carries[h][1] + jnp.sum(p[h], axis=0, keepdims=True) for h in hs]
        pv = [_dot(v_ref[h * vw:(h + 1) * vw, pl.ds(off, tq)], p[h].astype(BF16)) for h in hs]
        return tuple((m_new[h], l_new[h], alpha[h] * carries[h][2] + pv[h]) for h in hs)

    init = tuple((jnp.full((1, tq), -jnp.inf, F32), jnp.zeros((1, tq), F32), jnp.zeros((vw, tq), F32))
                 for _ in range(hb))
    carries = lax.fori_loop(0, qi, lambda j, c: block(j, c, None), init)
    ki = lax.broadcasted_iota(jnp.int32, (tq, tq), 0)
    qj = lax.broadcasted_iota(jnp.int32, (tq, tq), 1)
    carries = block(qi, carries, ki <= qj)
    for h in range(hb):
        _, l_fin, acc = carries[h]
        o_ref[:, h * vw:(h + 1) * vw] = (acc / l_fin).T.astype(o_ref.dtype)


def _flash(q, k, v, batch, seq, tq=512, hb=4):
    t = q.shape[0]
    tq = min(tq, seq)
    nq = seq // tq
    ng = MLA_HEADS // hb
    return pl.pallas_call(
        functools.partial(_flash_kernel, tq=tq, hb=hb),
        grid=(batch, ng, nq),
        in_specs=[pl.BlockSpec((tq, hb * 2 * LANES), lambda b, g, i: (b * nq + i, g)),
                  pl.BlockSpec((seq, hb * 2 * LANES), lambda b, g, i: (b, g)),
                  pl.BlockSpec((hb * MLA_V, seq), lambda b, g, i: (g, b))],
        out_specs=pl.BlockSpec((tq, hb * MLA_V), lambda b, g, i: (b * nq + i, g)),
        out_shape=jax.ShapeDtypeStruct((t, MLA_HEADS * MLA_V), F32),
        compiler_params=_params("arbitrary", "arbitrary", "arbitrary"),
        name="mla_flash",
    )(q, k, v)


def _finish(h_ref, acc, fnw_ref, o_ref, final):
    hn = h_ref[...] + acc
    o_ref[...] = _rms(hn, fnw_ref[...]) if final else hn


def _even_tail_kernel(a_ref, ys_ref, u_ref, q_ref, z_ref, h_ref, mk_ref, mv_ref, vec_ref, gw_ref, ow_ref,
                      fnw_ref, o_ref, *, final):
    w = RW_WIDTH
    z = z_ref[...]
    ga = (a_ref[...] * _silu(z[:, :w])).astype(BF16)
    yb = _gelu_tanh(ys_ref[...] + vec_ref[0:1, :] * u_ref[...])
    gate = _sigmoid(_dot(yb.astype(BF16), gw_ref[...]) + vec_ref[1:2, :])
    gb = (yb * gate * _silu(z[:, w:2 * w])).astype(BF16)
    gm = (_mem_attend(q_ref[...], mk_ref[0], mv_ref[0]) * _silu(z[:, 2 * w:])).astype(BF16)
    acc = _dot(ga, ow_ref[0:w, :]) + _dot(gb, ow_ref[w:2 * w, :]) + _dot(gm, ow_ref[2 * w:, :])
    _finish(h_ref, acc, fnw_ref, o_ref, final)


def _odd_tail_kernel(c_ref, d_ref, q_ref, z_ref, h_ref, mk_ref, mv_ref, ow_ref, fnw_ref, o_ref, *, final):
    w = D_MODEL
    z = z_ref[...]
    gc = c_ref[...].astype(BF16)
    gd = (d_ref[...] * _silu(z[:, :w])).astype(BF16)
    gm = (_mem_attend(q_ref[...], mk_ref[0], mv_ref[0]) * _silu(z[:, w:])).astype(BF16)
    acc = _dot(gc, ow_ref[0:w, :]) + _dot(gd, ow_ref[w:2 * w, :]) + _dot(gm, ow_ref[2 * w:, :])
    _finish(h_ref, acc, fnw_ref, o_ref, final)


def _tail_call(kernel, rows, consts, h, mk, mv, fnw, batch, seq, final, name, tm=256):
    t, d = h.shape
    nt = seq // tm
    row = lambda a: pl.BlockSpec((tm, a.shape[1]), lambda b, i: (b * nt + i, 0))
    full = lambda a: pl.BlockSpec(a.shape, lambda b, i: (0,) * a.ndim)
    memspec = pl.BlockSpec((1,) + mk.shape[1:], lambda b, i: (b, 0, 0))
    fnw = fnw.reshape(1, d)
    return pl.pallas_call(
        functools.partial(kernel, final=final),
        grid=(batch, nt),
        in_specs=[row(a) for a in rows] + [row(h), memspec, memspec] + [full(a) for a in consts] + [full(fnw)],
        out_specs=row(h),
        out_shape=jax.ShapeDtypeStruct((t, d), F32),
        compiler_params=_params("arbitrary", "arbitrary"),
        name=name,
    )(*rows, h, mk, mv, *consts, fnw)


def _pad_rows(rows, n=SUBLANES):
    width = rows[0].shape[-1]
    return jnp.concatenate([r.reshape(1, width) for r in rows] + [jnp.zeros((n - len(rows), width), F32)], axis=0)


def _even_layer(h, batch, seq, mk, mv, nw, in_w, out_w, fnw, final, mu, w0, w2, a0, a2, k_k, k_a, r_k,
                ln_w, ln_b, s5_ops, s5_d, glu_w, glu_b):
    rw_proj = 3 * RW_WIDTH + 2 * RW_LORA
    ev_width = 2 * RW_WIDTH + MEM_WIDTH
    splits = (rw_proj, RW_WIDTH, MEM_WIDTH, ev_width)
    p, u, q_mem, z = _norm_proj(h, nw, in_w.astype(BF16), splits, (F32, F32, F32, F32))
    rvec, wa = _rwkv_param_pack(w0, w2, a0, a2, k_k, k_a, r_k, ln_w, ln_b)
    a_out = _rwkv(p, mu, rvec, wa, batch, seq)
    ys = _s5(u, *s5_ops, batch, seq)

    vec = _pad_rows([s5_d, glu_b])
    return _tail_call(_even_tail_kernel, [a_out, ys, u, q_mem, z], [vec, glu_w.astype(BF16), out_w.astype(BF16)],
                      h, mk, mv, fnw, batch, seq, final, "even_tail")


def _odd_layer(h, batch, seq, mk, mv, cos, sin, nw, in_w, out_w, fnw, final, conv_w, conv_b, dt_bias, a_log,
               m2_d, m2_norm_w, q_norm_w, wq_up, kv_norm_w, wkv_up):
    width = M2_HEADS * HEAD_DIM
    conv_dim = width + 2 * M2_GROUPS * M2_STATE
    nh = MLA_HEADS
    o_dt = conv_dim
    o_cq = o_dt + M2_HEADS
    o_ckv = o_cq + MLA_Q_RANK
    o_kr = o_ckv + MLA_KV_RANK
    o_qm = o_kr + MLA_ROPE
    o_z = o_qm + MEM_WIDTH
    half = MLA_ROPE // 2
    w_kr = in_w[:, o_kr:o_qm]
    w_kr_sw = jnp.concatenate([w_kr[:, half:], w_kr[:, :half]], axis=1)
    w_misc = jnp.concatenate([w_kr, w_kr, w_kr_sw, w_kr_sw, in_w[:, o_dt:o_cq],
                              jnp.zeros((in_w.shape[0], LANES - M2_HEADS), in_w.dtype)], axis=1)
    w_all = jnp.concatenate([in_w[:, :o_dt], in_w[:, o_cq:o_ckv], in_w[:, o_ckv:o_kr], in_w[:, o_qm:o_z],
                             in_w[:, o_z:], w_misc], axis=1).astype(BF16)
    splits = (conv_dim, MLA_Q_RANK, MLA_KV_RANK, MEM_WIDTH, width, D_MODEL + MEM_WIDTH, 3 * LANES)
    xbc, cq, ckv, q_mem, z_c, z_dm, misc = _norm_proj(h, nw, w_all, splits, (F32,) * 7)

    cw = _pad_rows(list(conv_w) + [conv_b])
    pad16 = lambda x: jnp.concatenate([x, jnp.zeros((LANES - M2_HEADS,), F32)])
    vec = _pad_rows([pad16(dt_bias), pad16(-jnp.exp(a_log))])
    nwd = _pad_rows([m2_norm_w, jnp.repeat(m2_d, HEAD_DIM)])
    c_out = _ssd(xbc, misc, z_c, cw, vec, nwd, batch, seq)

    wq = wq_up.reshape(MLA_Q_RANK, nh, MLA_NOPE + MLA_ROPE)
    wq_r = wq[:, :, MLA_NOPE:]
    wq_sw = jnp.concatenate([wq_r[:, :, half:], wq_r[:, :, :half]], axis=2)
    wq_all = jnp.concatenate([wq[:, :, :MLA_NOPE].reshape(MLA_Q_RANK, -1), wq_r.reshape(MLA_Q_RANK, -1),
                              wq_sw.reshape(MLA_Q_RANK, -1)], axis=1).astype(BF16)
    wkv = wkv_up.reshape(MLA_KV_RANK, nh, MLA_NOPE + MLA_V)
    wkv_all = jnp.concatenate([wkv[:, :, :MLA_NOPE].reshape(MLA_KV_RANK, -1),
                               wkv[:, :, MLA_NOPE:].reshape(MLA_KV_RANK, -1)], axis=1).astype(BF16)
    qc, kc, vv = _mla_prep(cq, ckv, misc, cos, sin, q_norm_w, kv_norm_w, wq_all, wkv_all)
    d_attn = _flash(qc, kc, vv, batch, seq)

    return _tail_call(_odd_tail_kernel, [c_out, d_attn, q_mem, z_dm], [out_w.astype(BF16)],
                      h, mk, mv, fnw, batch, seq, final, "odd_tail")


def _rope_tables(positions):
    inv = 1.0 / (ROPE_THETA ** (jnp.arange(0, MLA_ROPE, 2, dtype=F32) / MLA_ROPE))
    ang = positions.astype(F32).reshape(-1, 1) * inv
    cos, sin = jnp.cos(ang), jnp.sin(ang)
    return jnp.tile(cos, (1, 4)), jnp.tile(jnp.concatenate([-sin, sin], axis=1), (1, 2))


def kernel(x, mem, positions, norm_w, mem_norm_w, final_norm_w, mem_kv_w, ev_in_w, ev_out_w, rw_mu, rw_w0, rw_w2, rw_a0, rw_a2, rw_k_k, rw_k_a, rw_r_k, rw_ln_w, rw_ln_b, s5_lambda_re, s5_lambda_im, s5_b_re, s5_b_im, s5_c_re, s5_c_im, s5_d, s5_log_dt, s5_glu_w, s5_glu_b, od_in_w, od_out_w, m2_conv_w, m2_conv_b, m2_dt_bias, m2_a_log, m2_d, m2_norm_w, mla_q_norm_w, mla_wq_up, mla_kv_norm_w, mla_wkv_up):
    batch, seq, d = x.shape
    depth = norm_w.shape[0]
    mk, mv = _mem_kv(mem, mem_norm_w, mem_kv_w.astype(BF16))
    cos, sin = _rope_tables(positions)
    s5_ops = jax.vmap(functools.partial(_s5_param_pack, nj=seq // S5_SUB))(
        s5_lambda_re, s5_lambda_im, s5_b_re, s5_b_im, s5_c_re, s5_c_im, s5_log_dt)
    ev_in_w, ev_out_w, s5_glu_w = ev_in_w.astype(BF16), ev_out_w.astype(BF16), s5_glu_w.astype(BF16)
    od_out_w = od_out_w.astype(BF16)
    h = x.reshape(batch * seq, d)
    for layer in range(depth):
        i = layer // 2
        final = layer == depth - 1
        if layer % 2 == 0:
            h = _even_layer(h, batch, seq, mk[layer], mv[layer], norm_w[layer], ev_in_w[i], ev_out_w[i],
                            final_norm_w, final, rw_mu[i], rw_w0[i], rw_w2[i], rw_a0[i], rw_a2[i], rw_k_k[i],
                            rw_k_a[i], rw_r_k[i], rw_ln_w[i], rw_ln_b[i], tuple(op[i] for op in s5_ops), s5_d[i],
                            s5_glu_w[i], s5_glu_b[i])
        else:
            h = _odd_layer(h, batch, seq, mk[layer], mv[layer], cos, sin, norm_w[layer], od_in_w[i], od_out_w[i],
                           final_norm_w, final, m2_conv_w[i], m2_conv_b[i], m2_dt_bias[i], m2_a_log[i], m2_d[i],
                           m2_norm_w[i], mla_q_norm_w[i], mla_wq_up[i], mla_kv_norm_w[i], mla_wkv_up[i])
    return h.reshape(batch, seq, d)
```

```python
import functools
import math

import jax
import jax.numpy as jnp
from jax import lax
from jax.experimental import pallas as pl
from jax.experimental.pallas import tpu as pltpu

F32 = jnp.float32
BF16 = jnp.bfloat16
HIGHEST = lax.Precision.HIGHEST

LANES = 128
SUBLANES = 8
VMEM_LIMIT_BYTES = 56 * 1024 * 1024

D_MODEL = 1024
HEAD_DIM = 64
NORM_EPS = 1e-6
RW_WIDTH = 1024
RW_LORA = 64
RW_GN_EPS = 64e-5
RW_CHUNK = 64
S5_GROUP = 16
S5_STATE = 64
S5_SUB = 16
M2_HEADS = 16
M2_GROUPS = 2
M2_STATE = 128
M2_CONV = 4
M2_CHUNK = 128
MLA_HEADS = 8
MLA_NOPE = 128
MLA_ROPE = 64
MLA_V = 128
MLA_Q_RANK = 384
MLA_KV_RANK = 256
ROPE_THETA = 10000.0
MEM_HEADS = 4
MEM_WIDTH = MEM_HEADS * HEAD_DIM


def _dot(a, b, precision=None):
    return jnp.dot(a, b, preferred_element_type=F32, precision=precision)


def _dot_nt(a, b, precision=None):
    return lax.dot_general(a, b, (((1,), (1,)), ((), ())), preferred_element_type=F32,
                           precision=precision)


def _sigmoid(x):
    return 1.0 / (1.0 + jnp.exp(-x))


def _silu(x):
    return x * _sigmoid(x)


def _softplus(x):
    return jnp.maximum(x, 0.0) + jnp.log(1.0 + jnp.exp(-jnp.abs(x)))


def _gelu_tanh(x):
    return 0.5 * x * (1.0 + jnp.tanh(math.sqrt(2.0 / math.pi) * (x + 0.044715 * (x * x * x))))


def _rms(x, w, eps=NORM_EPS):
    ms = jnp.mean(x * x, axis=-1, keepdims=True)
    return x * lax.rsqrt(ms + eps) * w


def _params(*sem):
    return pltpu.CompilerParams(dimension_semantics=sem, vmem_limit_bytes=VMEM_LIMIT_BYTES)


def _norm_proj_kernel(x_ref, nw_ref, w_ref, *out_refs, splits):
    xn = _rms(x_ref[...], nw_ref[...]).astype(BF16)
    off = 0
    for o_ref, n in zip(out_refs, splits):
        o_ref[...] = _dot(xn, w_ref[:, off:off + n]).astype(o_ref.dtype)
        off += n


def _norm_proj(x, nw, w, splits, dtypes, tm=256):
    t, d = x.shape
    n = w.shape[1]
    assert sum(splits) == n and t % tm == 0
    return pl.pallas_call(
        functools.partial(_norm_proj_kernel, splits=splits),
        grid=(t // tm,),
        in_specs=[pl.BlockSpec((tm, d), lambda i: (i, 0)),
                  pl.BlockSpec((1, d), lambda i: (0, 0)),
                  pl.BlockSpec((d, n), lambda i: (0, 0))],
        out_specs=[pl.BlockSpec((tm, s), lambda i: (i, 0)) for s in splits],
        out_shape=[jax.ShapeDtypeStruct((t, s), dt) for s, dt in zip(splits, dtypes)],
        compiler_params=_params("arbitrary"),
        name="norm_proj",
    )(x, nw.reshape(1, d), w)


def _mem_kv_kernel(mem_ref, nw_ref, w_ref, k_ref, v_ref):
    mn = _rms(mem_ref[0], nw_ref[...]).astype(BF16)
    kv = _dot(mn, w_ref[0])
    k_ref[0, 0] = kv[:, :MEM_WIDTH].astype(k_ref.dtype)
    v_ref[0, 0] = kv[:, MEM_WIDTH:].astype(v_ref.dtype)


def _mem_kv(mem, nw, w):
    b, m, d = mem.shape
    depth = w.shape[0]
    shp = jax.ShapeDtypeStruct((depth, b, m, MEM_WIDTH), BF16)
    return pl.pallas_call(
        _mem_kv_kernel,
        grid=(depth, b),
        in_specs=[pl.BlockSpec((1, m, d), lambda l, i: (i, 0, 0)),
                  pl.BlockSpec((1, d), lambda l, i: (0, 0)),
                  pl.BlockSpec((1, d, 2 * MEM_WIDTH), lambda l, i: (l, 0, 0))],
        out_specs=[pl.BlockSpec((1, 1, m, MEM_WIDTH), lambda l, i: (l, i, 0, 0))] * 2,
        out_shape=[shp, shp],
        compiler_params=_params("arbitrary", "arbitrary"),
        name="mem_kv",
    )(mem, nw.reshape(1, d), w)


def _mem_attend(q, k, v):
    lane = lax.broadcasted_iota(jnp.int32, (1, MEM_WIDTH), 1)
    qb = (q * (HEAD_DIM ** -0.5)).astype(BF16)
    out = jnp.zeros(q.shape, F32)
    for h in range(MEM_HEADS):
        hm = (lane >> 6) == h
        sc = _dot_nt(qb, jnp.where(hm, k, jnp.zeros_like(k)))
        sc = sc - jnp.max(sc, axis=-1, keepdims=True)
        p = jnp.exp(sc)
        p = p / jnp.sum(p, axis=-1, keepdims=True)
        out = out + _dot(p.astype(BF16), jnp.where(hm, v, jnp.zeros_like(v)))
    return out


def _split3(x):
    hi = x.astype(BF16)
    r1 = x - hi.astype(F32)
    mid = r1.astype(BF16)
    lo = (r1 - mid.astype(F32)).astype(BF16)
    return hi, mid, lo


def _rwkv_kernel(p_ref, mu_ref, vec_ref, wa_ref, out_ref, st_ref, prev_ref, *, L):
    c = pl.program_id(1)
    W = RW_WIDTH

    @pl.when(c == 0)
    def _():
        st_ref[...] = jnp.zeros_like(st_ref)
        prev_ref[...] = jnp.zeros_like(prev_ref)

    x = p_ref[...]
    R = x.shape[0]
    nck = R // L
    row_l = lax.broadcasted_iota(jnp.int32, (R, 1), 0)
    xs = jnp.where(row_l == 0, prev_ref[0:1, :], pltpu.roll(x, 1, axis=0))
    prev_ref[0:1, :] = x[R - 1:R, :]
    xm = x + (xs - x) * mu_ref[...]
    r, k, v, lat = xm[:, :W], xm[:, W:2 * W], xm[:, 2 * W:3 * W], xm[:, 3 * W:]

    w0, a0, k_k, k_a, r_k, ln_w, ln_b = [vec_ref[i:i + 1, :] for i in range(7)]
    lane = lax.broadcasted_iota(jnp.int32, (1, LANES), 1)
    lat_t = jnp.where(lane < RW_LORA, jnp.tanh(lat), lat).astype(BF16)
    la = _dot(lat_t, wa_ref[...])
    log_w = -_softplus(-(w0 + la[:, :W])) - 0.5
    lw = -jnp.exp(log_w)
    iclr = _sigmoid(a0 + la[:, W:])
    kk0 = k * k_k
    kp = k * (1.0 + (iclr - 1.0) * k_a)
    rkr = r * kp * r_k

    il = lax.broadcasted_iota(jnp.int32, (R, 3 * R), 0)
    jl = lax.broadcasted_iota(jnp.int32, (R, 3 * R), 1) & (R - 1)
    same_chunk = (il & -L) == (jl & -L)
    tri3 = jnp.where(same_chunk & (jl <= il), 1.0, 0.0).astype(BF16)
    cum = _dot(tri3, jnp.concatenate(_split3(lw), axis=0))
    cum_ls = [cum[(ck + 1) * L - 1:(ck + 1) * L, :] for ck in range(nck)]
    cum_end = jnp.concatenate([jnp.broadcast_to(cl, (L, W)) for cl in cum_ls], axis=0)
    e_neg = jnp.exp(-cum)
    e_exc = jnp.exp(cum - lw)
    e_inc = jnp.exp(cum)
    dend = jnp.exp(cum_end - cum)
    e_tot = [jnp.exp(cl) for cl in cum_ls]

    i2 = lax.broadcasted_iota(jnp.int32, (LANES, LANES), 0)
    j2 = lax.broadcasted_iota(jnp.int32, (LANES, LANES), 1)
    same_head = (i2 >> 6) == (j2 >> 6)
    head_ones = jnp.where(same_head, 1.0, 0.0).astype(BF16)
    strict = same_head & (j2 < i2)
    incl = same_head & (j2 <= i2)
    eye = jnp.where(i2 == j2, 1.0, 0.0)
    levels = [((i2 & -(2 * s)) == (j2 & -(2 * s))) & ((i2 & s) != 0) & ((j2 & s) == 0)
              for s in (1, 2, 4, 8, 16, 32)]
    m0 = lane < HEAD_DIM
    stack = lambda t: jnp.concatenate([jnp.where(m0, t, 0.0), jnp.where(m0, 0.0, t)], axis=0)

    npair = W // LANES
    units = [(slice(ck * L, (ck + 1) * L), slice(p * LANES, (p + 1) * LANES))
             for ck in range(nck) for p in range(npair)]
    each = lambda f: [f(i) for i in range(len(units))]
    at = lambda arr, i: arr[units[i][0], units[i][1]]
    sums = each(lambda i: _dot(jnp.concatenate([at(kk0, i) * at(kk0, i), at(rkr, i)], axis=0).astype(BF16),
                               head_ones))
    kk = each(lambda i: at(kk0, i) * lax.rsqrt(jnp.maximum(sums[i][:L], 1e-24)))
    bvec = each(lambda i: kk[i] * at(iclr, i))
    ar = each(lambda i: jnp.concatenate([stack(-kk[i] * at(e_exc, i)), stack(at(r, i) * at(e_inc, i))],
                                        axis=0).astype(BF16))
    b2 = each(lambda i: stack(bvec[i] * at(e_neg, i)).astype(BF16))
    k2 = each(lambda i: stack(at(kp, i) * at(e_neg, i)).astype(BF16))
    v2f = each(lambda i: stack(at(v, i)))
    v2 = each(lambda i: v2f[i].astype(BF16))
    bk_end = each(lambda i: jnp.concatenate([stack(bvec[i] * at(dend, i)), stack(at(kp, i) * at(dend, i))],
                                            axis=0).astype(BF16))
    sb = each(lambda i: _dot_nt(ar[i], b2[i]))
    sk = each(lambda i: _dot_nt(ar[i], k2[i]))
    aab = each(lambda i: jnp.where(strict, sb[i][:LANES], 0.0))

    inv = each(lambda i: eye + jnp.where(levels[0], aab[i], 0.0))
    for lvl in levels[1:]:
        invb = each(lambda i: inv[i].astype(BF16))
        t1 = each(lambda i: _dot(invb[i], jnp.where(lvl, aab[i], 0.0).astype(BF16)).astype(BF16))
        inv = each(lambda i: inv[i] + _dot(t1[i], invb[i]))
    aak_v = each(lambda i: _dot(jnp.where(strict, sk[i][:LANES], 0.0).astype(BF16), v2[i]))
    rk_v = each(lambda i: _dot(jnp.where(incl, sk[i][LANES:], 0.0).astype(BF16), v2[i]))

    st = [st_ref[p] for p in range(npair)]
    y = [None] * len(units)
    for ck in range(nck):
        ids = [ck * npair + p for p in range(npair)]
        sta = [_dot_nt(ar[i], st[p].astype(BF16)) for p, i in enumerate(ids)]
        u2 = [_dot(inv[i].astype(BF16), (sta[p][:LANES] + aak_v[i]).astype(BF16)) for p, i in enumerate(ids)]
        y2 = [sta[p][LANES:] + _dot(jnp.where(incl, sb[i][LANES:], 0.0).astype(BF16), u2[p].astype(BF16)) + rk_v[i]
              for p, i in enumerate(ids)]
        for p, i in enumerate(ids):
            y[i] = y2[p][:L, :] + y2[p][L:, :]
            uv_t = jnp.concatenate([u2[p].T, v2f[i].T], axis=1).astype(BF16)
            st[p] = st[p] * e_tot[ck][:, units[i][1]] + _dot(uv_t, bk_end[i])
    for p in range(npair):
        st_ref[p] = st[p]

    mean = each(lambda i: _dot(y[i].astype(BF16), head_ones) * (1.0 / HEAD_DIM))
    dcen = each(lambda i: y[i] - mean[i])
    var = each(lambda i: _dot((dcen[i] * dcen[i]).astype(BF16), head_ones) * (1.0 / HEAD_DIM))
    for i, (rs, ls_) in enumerate(units):
        out_ref[rs, ls_] = (dcen[i] * lax.rsqrt(var[i] + RW_GN_EPS) * ln_w[:, ls_] + ln_b[:, ls_]
                            + sums[i][L:] * at(v, i))


def _rwkv(p, mu, vec, wa, batch, seq, nck=2):
    L = RW_CHUNK * (nck if seq % (nck * RW_CHUNK) == 0 else 1)
    t, pw = p.shape
    nc = seq // L
    return pl.pallas_call(
        functools.partial(_rwkv_kernel, L=RW_CHUNK),
        grid=(batch, nc),
        in_specs=[pl.BlockSpec((L, pw), lambda b, c: (b * nc + c, 0)),
                  pl.BlockSpec((1, pw), lambda b, c: (0, 0)),
                  pl.BlockSpec(vec.shape, lambda b, c: (0, 0)),
                  pl.BlockSpec(wa.shape, lambda b, c: (0, 0))],
        out_specs=pl.BlockSpec((L, RW_WIDTH), lambda b, c: (b * nc + c, 0)),
        out_shape=jax.ShapeDtypeStruct((t, RW_WIDTH), F32),
        scratch_shapes=[pltpu.VMEM((RW_WIDTH // LANES, LANES, LANES), F32), pltpu.VMEM((SUBLANES, pw), F32)],
        compiler_params=_params("arbitrary", "arbitrary"),
        name="rwkv7",
    )(p, mu.reshape(1, pw), vec, wa)


def _rwkv_param_pack(w0, w2, a0, a2, k_k, k_a, r_k, ln_w, ln_b):
    vec = _pad_rows([w0, a0, k_k, k_a, r_k.reshape(-1), ln_w, ln_b])
    z = jnp.zeros_like(w2)
    wa = jnp.concatenate([jnp.concatenate([w2, z], axis=1), jnp.concatenate([z, a2], axis=1)], axis=0)
    return vec, wa.astype(BF16)


def _s5_kernel(u_ref, tz_ref, gm_ref, cc_ref, pw_ref, y_ref, tzd_ref, gmd_ref, ccd_ref, *, nj):
    ls, pch, n = S5_SUB, S5_GROUP, S5_STATE
    gs = LANES // pch

    @pl.when(pl.program_id(1) == 0)
    def _():
        tzd_ref[...] = jnp.zeros_like(tzd_ref)
        gmd_ref[...] = jnp.zeros_like(gmd_ref)
        ccd_ref[...] = jnp.zeros_like(ccd_ref)
        for g in range(gs):
            for a in range(ls):
                r0 = a * LANES + g * pch
                for b in range(a, ls):
                    c0 = b * LANES + g * pch
                    tzd_ref[r0:r0 + pch, c0:c0 + pch] = tz_ref[g, a * pch:(a + 1) * pch, b * pch:(b + 1) * pch]
                for c in range(2):
                    gmd_ref[r0:r0 + pch, (c * gs + g) * n:(c * gs + g + 1) * n] = (
                        gm_ref[g, a * pch:(a + 1) * pch, c * n:(c + 1) * n])
                    ccd_ref[(c * gs + g) * n:(c * gs + g + 1) * n, r0:r0 + pch] = (
                        cc_ref[g, c * n:(c + 1) * n, a * pch:(a + 1) * pch])

    rows = u_ref.shape[0] // ls
    xcat = jnp.concatenate([u_ref[pl.ds(l, rows, stride=ls), :] for l in range(ls)], axis=1).astype(BF16)
    otile = 2 * LANES
    y = jnp.concatenate([_dot(xcat[:, :(m + 1) * otile], tzd_ref[:(m + 1) * otile, m * otile:(m + 1) * otile])
                         for m in range(ls * LANES // otile)], axis=1)
    x = _dot(xcat, gmd_ref[...])
    half = x.shape[1] // 2
    jrow = lax.broadcasted_iota(jnp.int32, (rows, 1), 0) & (nj - 1)
    pw = pw_ref[0]
    d, lvl = 1, 0
    while d < nj:
        xs = jnp.where(jrow >= d, pltpu.roll(x, d, axis=0), 0.0)
        xsw = jnp.concatenate([xs[:, half:], xs[:, :half]], axis=1)
        x = x + xs * pw[2 * lvl:2 * lvl + 1, :] + xsw * pw[2 * lvl + 1:2 * lvl + 2, :]
        d, lvl = 2 * d, lvl + 1
    h_in = jnp.where(jrow >= 1, pltpu.roll(x, 1, axis=0), 0.0)
    y = y + _dot(h_in.astype(BF16), ccd_ref[...])
    for l in range(ls):
        y_ref[pl.ds(l, rows, stride=ls), :] = y[:, l * LANES:(l + 1) * LANES]


def _s5(u, tz, gm, cc, pw, batch, seq, nb=2):
    t, width = u.shape
    nslab = width // LANES
    gs = LANES // S5_GROUP
    nj = seq // S5_SUB
    sub_w = S5_SUB * LANES
    st_w = 2 * gs * S5_STATE
    pwd = pw.reshape(nslab, gs, pw.shape[1], 2, S5_STATE).transpose(0, 2, 3, 1, 4).reshape(nslab, pw.shape[1], st_w)
    grp_spec = lambda a: pl.BlockSpec((gs,) + a.shape[1:], lambda s, b: (s, 0, 0))
    nb = nb if batch % nb == 0 else 1
    assert nj & (nj - 1) == 0
    return pl.pallas_call(
        functools.partial(_s5_kernel, nj=nj),
        grid=(nslab, batch // nb),
        in_specs=[pl.BlockSpec((nb * seq, LANES), lambda s, b: (b, s)),
                  grp_spec(tz), grp_spec(gm), grp_spec(cc),
                  pl.BlockSpec((1,) + pwd.shape[1:], lambda s, b: (s, 0, 0))],
        out_specs=pl.BlockSpec((nb * seq, LANES), lambda s, b: (b, s)),
        out_shape=jax.ShapeDtypeStruct((t, width), F32),
        scratch_shapes=[pltpu.VMEM((sub_w, sub_w), BF16), pltpu.VMEM((sub_w, st_w), BF16),
                        pltpu.VMEM((st_w, sub_w), BF16)],
        compiler_params=_params("arbitrary", "arbitrary"),
        name="s5",
    )(u, tz, gm, cc, pwd)


def _s5_param_pack(lam_re, lam_im, b_re, b_im, c_re, c_im, log_dt, nj):
    ls = S5_SUB
    dt = jnp.exp(log_dt)[:, None]
    zr, zi = lam_re * dt, lam_im * dt

    def powers(steps):
        st = steps[:, None, None]
        mag = jnp.exp(st * zr)
        return mag * jnp.cos(st * zi), mag * jnp.sin(st * zi)

    ab_re, ab_im = powers(jnp.ones((1,), F32))
    ab_re, ab_im = ab_re[0], ab_im[0]
    den = lam_re * lam_re + lam_im * lam_im
    nr, ni = ab_re - 1.0, ab_im
    f_re = (nr * lam_re + ni * lam_im) / den
    f_im = (ni * lam_re - nr * lam_im) / den
    bb_re = f_re[..., None] * b_re - f_im[..., None] * b_im
    bb_im = f_re[..., None] * b_im + f_im[..., None] * b_re
    pr, pi = powers(jnp.arange(0, ls + 1, dtype=F32))
    cl_re = c_re[None] * pr[:, :, None, :] - c_im[None] * pi[:, :, None, :]
    cl_im = c_re[None] * pi[:, :, None, :] + c_im[None] * pr[:, :, None, :]
    kern = jnp.einsum('dgpn,gnq->dgpq', jnp.concatenate([cl_re[:ls], -cl_im[:ls]], axis=3),
                      jnp.concatenate([bb_re, bb_im], axis=1), precision=lax.Precision.HIGH)
    lag = jnp.arange(ls)[None, :] - jnp.arange(ls)[:, None]
    tz = jnp.where((lag >= 0)[:, :, None, None, None], kern[jnp.clip(lag, 0, ls - 1)], 0.0)
    g = lam_re.shape[0]
    pch = S5_GROUP
    tz = tz.transpose(2, 0, 4, 1, 3).reshape(g, ls * pch, ls * pch)
    rr, ri = pr[ls - 1::-1][:ls], pi[ls - 1::-1][:ls]
    gm_re = rr[..., None] * bb_re[None] - ri[..., None] * bb_im[None]
    gm_im = rr[..., None] * bb_im[None] + ri[..., None] * bb_re[None]
    gm = jnp.concatenate([gm_re, gm_im], axis=2)
    gm = gm.transpose(1, 0, 3, 2).reshape(g, ls * pch, 2 * S5_STATE)
    cc = jnp.concatenate([cl_re[1:], -cl_im[1:]], axis=3)
    cc = cc.transpose(1, 3, 0, 2).reshape(g, 2 * S5_STATE, ls * pch)
    lv = []
    d = 1
    while d < nj:
        lv.append(float(ls * d))
        d *= 2
    qr, qi = powers(jnp.asarray(lv, F32))
    pw = jnp.stack([jnp.concatenate([qr, qr], axis=-1), jnp.concatenate([-qi, qi], axis=-1)], axis=1)
    pw = pw.reshape(2 * len(lv), g, 2 * S5_STATE).transpose(1, 0, 2)
    pw = jnp.concatenate([pw, jnp.zeros((g, 16 - 2 * len(lv), 2 * S5_STATE), F32)], axis=1)
    return tz.astype(BF16), gm.astype(BF16), cc.astype(BF16), pw


def _ssd_kernel(xbc_ref, misc_ref, zc_ref, cw_ref, vec_ref, nw_ref, out_ref,
                st_ref, halo_ref, buf_ref, *, L):
    c = pl.program_id(1)
    width = M2_HEADS * HEAD_DIM
    gs = M2_STATE

    @pl.when(c == 0)
    def _():
        st_ref[...] = jnp.zeros_like(st_ref)
        halo_ref[...] = jnp.zeros_like(halo_ref)

    xbc = xbc_ref[...]
    buf_ref[0:SUBLANES, :] = halo_ref[...]
    buf_ref[SUBLANES:SUBLANES + L, :] = xbc
    halo_ref[...] = xbc[L - SUBLANES:L, :]
    cw = cw_ref[...]
    nshift = M2_CONV - 1
    si = lax.broadcasted_iota(jnp.int32, (nshift * L, L + SUBLANES), 0)
    sj = lax.broadcasted_iota(jnp.int32, (nshift * L, L + SUBLANES), 1)
    tap_i = (si >= L).astype(jnp.int32) + (si >= 2 * L).astype(jnp.int32)
    shift = jnp.where(sj == (si - tap_i * L) + tap_i + (SUBLANES - nshift), 1.0, 0.0).astype(BF16)
    zsh = _dot(shift, buf_ref[...].astype(BF16))
    acc = cw[M2_CONV:M2_CONV + 1, :] + cw[nshift:M2_CONV, :] * xbc
    for tap in range(nshift):
        acc = acc + cw[tap:tap + 1, :] * zsh[tap * L:(tap + 1) * L, :]
    xc = _silu(acc)
    xs = xc[:, :width]

    vec = vec_ref[...]
    dt = _softplus(misc_ref[:, 2 * LANES:3 * LANES] + vec[0:1, :])
    a_dt = dt * vec[1:2, :]
    il = lax.broadcasted_iota(jnp.int32, (L, 3 * L), 0)
    jl = lax.broadcasted_iota(jnp.int32, (L, 3 * L), 1) & (L - 1)
    tri3 = jnp.where(jl <= il, 1.0, 0.0).astype(BF16)
    cum = _dot(tri3, jnp.concatenate(_split3(a_dt), axis=0))
    cum_t = cum.T

    eh = lax.broadcasted_iota(jnp.int32, (3 * LANES, width), 0) & (LANES - 1)
    ec = lax.broadcasted_iota(jnp.int32, (3 * LANES, width), 1)
    expand3 = jnp.where((ec >> 6) == eh, 1.0, 0.0).astype(BF16)
    dt_x = _dot(jnp.concatenate(_split3(dt), axis=1), expand3)
    cum_x = _dot(jnp.concatenate(_split3(cum), axis=1), expand3)
    cum_lx = cum_x[L - 1:L, :]
    xdt = xs * dt_x
    xdd = (xdt * jnp.exp(cum_lx - cum_x)).astype(BF16)
    xdt_b = xdt.astype(BF16)
    ecum = jnp.exp(cum_x)
    e_tot = jnp.exp(cum_lx)

    tril = (lax.broadcasted_iota(jnp.int32, (L, L), 1) <= lax.broadcasted_iota(jnp.int32, (L, L), 0))
    lane = lax.broadcasted_iota(jnp.int32, (1, LANES), 1)
    m0 = lane < HEAD_DIM
    hpg = M2_HEADS // M2_GROUPS
    gw = hpg * HEAD_DIM
    groups = range(M2_GROUPS)
    heads = range(M2_HEADS)
    bm_f = [xc[:, width + g * gs:width + (g + 1) * gs] for g in groups]
    bm = [b.astype(BF16) for b in bm_f]
    cm = [xc[:, width + (M2_GROUPS + g) * gs:width + (M2_GROUPS + g + 1) * gs].astype(BF16) for g in groups]
    cbm = [jnp.where(tril, _dot_nt(cm[g], bm[g]), 0.0) for g in groups]
    st = [st_ref[:, g * gw:(g + 1) * gw] for g in groups]
    y_off = [_dot(cm[g], st[g].astype(BF16)) for g in groups]
    for g in groups:
        st_ref[:, g * gw:(g + 1) * gw] = (st[g] * e_tot[:, g * gw:(g + 1) * gw]
                                          + _dot(bm_f[g].T.astype(BF16), xdd[:, g * gw:(g + 1) * gw]))
    mh = [(cbm[h // hpg] * jnp.exp(jnp.minimum(cum[:, h:h + 1] - cum_t[h:h + 1, :], 0.0))).astype(BF16)
          for h in heads]
    zero = jnp.zeros((), BF16)
    yd = [_dot(mh[h], jnp.where(m0 if h % 2 == 0 else jnp.logical_not(m0),
                                xdt_b[:, (h // 2) * LANES:(h // 2 + 1) * LANES], zero)) for h in heads]
    y_diag = jnp.concatenate([yd[2 * p] + yd[2 * p + 1] for p in range(M2_HEADS // 2)], axis=1)
    y = y_diag + jnp.concatenate(y_off, axis=1) * ecum + xs * nw_ref[1:2, :]
    y = y * _silu(zc_ref[...])
    gw = width // M2_GROUPS
    outs = []
    for g in range(M2_GROUPS):
        yg = y[:, g * gw:(g + 1) * gw]
        outs.append(_rms(yg, nw_ref[0:1, g * gw:(g + 1) * gw]))
    out_ref[...] = jnp.concatenate(outs, axis=1)


def _ssd(xbc, misc, z, cw, vec, nw, batch, seq):
    L = M2_CHUNK
    t, cd = xbc.shape
    nc = seq // L
    width = M2_HEADS * HEAD_DIM
    return pl.pallas_call(
        functools.partial(_ssd_kernel, L=L),
        grid=(batch, nc),
        in_specs=[pl.BlockSpec((L, cd), lambda b, c: (b * nc + c, 0)),
                  pl.BlockSpec((L, misc.shape[1]), lambda b, c: (b * nc + c, 0)),
                  pl.BlockSpec((L, width), lambda b, c: (b * nc + c, 0)),
                  pl.BlockSpec((SUBLANES, cd), lambda b, c: (0, 0)),
                  pl.BlockSpec((SUBLANES, LANES), lambda b, c: (0, 0)),
                  pl.BlockSpec((SUBLANES, width), lambda b, c: (0, 0))],
        out_specs=pl.BlockSpec((L, width), lambda b, c: (b * nc + c, 0)),
        out_shape=jax.ShapeDtypeStruct((t, width), F32),
        scratch_shapes=[pltpu.VMEM((M2_STATE, width), F32), pltpu.VMEM((SUBLANES, cd), F32),
                        pltpu.VMEM((L + SUBLANES, cd), F32)],
        compiler_params=_params("arbitrary", "arbitrary"),
        name="ssd",
    )(xbc, misc, z, cw, vec, nw)


def _mla_prep_kernel(cq_ref, ckv_ref, misc_ref, cos_ref, sin_ref, qnw_ref, kvnw_ref, wq_ref, wkv_ref,
                     q_ref, k_ref, v_ref):
    nh = MLA_HEADS
    cos, sin = cos_ref[...], sin_ref[...]
    q = _dot(_rms(cq_ref[...], qnw_ref[...]).astype(BF16), wq_ref[...])
    scale = (MLA_NOPE + MLA_ROPE) ** -0.5 * math.log2(math.e)
    rope0 = nh * MLA_NOPE
    rot0 = rope0 + nh * MLA_ROPE
    for pair in range(nh // 2):
        sl = slice(pair * LANES, (pair + 1) * LANES)
        qr = (q[:, rope0:rot0][:, sl] * cos + q[:, rot0:][:, sl] * sin) * scale
        for hh in range(2):
            h = 2 * pair + hh
            q_ref[:, h * 2 * LANES:h * 2 * LANES + LANES] = (
                q[:, h * MLA_NOPE:(h + 1) * MLA_NOPE] * scale).astype(q_ref.dtype)
            q_ref[:, h * 2 * LANES + LANES:(h + 1) * 2 * LANES] = qr.astype(q_ref.dtype)
    kv = _dot(_rms(ckv_ref[...], kvnw_ref[...]).astype(BF16), wkv_ref[...])
    kpe = misc_ref[:, :LANES] * cos + misc_ref[:, LANES:2 * LANES] * sin
    lane = lax.broadcasted_iota(jnp.int32, (1, LANES), 1)
    slots = [jnp.where(lane < MLA_ROPE, kpe, 0.0), jnp.where(lane < MLA_ROPE, 0.0, kpe)]
    for h in range(nh):
        k_ref[:, h * 2 * LANES:h * 2 * LANES + LANES] = kv[:, h * MLA_NOPE:(h + 1) * MLA_NOPE].astype(k_ref.dtype)
        k_ref[:, h * 2 * LANES + LANES:(h + 1) * 2 * LANES] = slots[h % 2].astype(k_ref.dtype)
    v_ref[...] = kv[:, nh * MLA_NOPE:].T.astype(v_ref.dtype)


def _mla_prep(cq, ckv, misc, cos, sin, qnw, kvnw, wq, wkv, tm=256):
    t = cq.shape[0]
    nh = MLA_HEADS
    row = lambda n: pl.BlockSpec((tm, n), lambda i: (i, 0))
    full = lambda a: pl.BlockSpec(a.shape, lambda i: (0, 0))
    qnw, kvnw = qnw.reshape(1, -1), kvnw.reshape(1, -1)
    return pl.pallas_call(
        _mla_prep_kernel,
        grid=(t // tm,),
        in_specs=[row(cq.shape[1]), row(ckv.shape[1]), row(misc.shape[1]), row(LANES), row(LANES),
                  full(qnw), full(kvnw), full(wq), full(wkv)],
        out_specs=[row(nh * 2 * LANES), row(nh * 2 * LANES), pl.BlockSpec((nh * MLA_V, tm), lambda i: (0, i))],
        out_shape=[jax.ShapeDtypeStruct((t, nh * 2 * LANES), BF16),
                   jax.ShapeDtypeStruct((t, nh * 2 * LANES), BF16),
                   jax.ShapeDtypeStruct((nh * MLA_V, t), BF16)],
        compiler_params=_params("arbitrary"),
        name="mla_prep",
    )(cq, ckv, misc, cos, sin, qnw, kvnw, wq, wkv)


def _flash_kernel(q_ref, k_ref, v_ref, o_ref, *, tq, hb):
    qi = pl.program_id(2)
    qw, vw = 2 * LANES, MLA_V
    qs = [q_ref[:, h * qw:(h + 1) * qw] for h in range(hb)]

    def block(j, carries, mask):
        off = pl.multiple_of(j * tq, tq)
        hs = range(hb)
        s = [_dot_nt(k_ref[pl.ds(off, tq), h * qw:(h + 1) * qw], qs[h]) for h in hs]
        if mask is not None:
            s = [jnp.where(mask, s[h], -jnp.inf) for h in hs]
        m_new = [jnp.maximum(carries[h][0], jnp.max(s[h], axis=0, keepdims=True)) for h in hs]
        alpha = [jnp.exp2(carries[h][0] - m_new[h]) for h in hs]
        p = [jnp.exp2(s[h] - m_new[h]) for h in hs]
        l_new = [alpha[h] * carries[h][1] + jnp.sum(p[h], axis=0, keepdims=True) for h in hs]
        pv = [_dot(v_ref[h * vw:(h + 1) * vw, pl.ds(off, tq)], p[h].astype(BF16)) for h in hs]
        return tuple((m_new[h], l_new[h], alpha[h] * carries[h][2] + pv[h]) for h in hs)

    init = tuple((jnp.full((1, tq), -jnp.inf, F32), jnp.zeros((1, tq), F32), jnp.zeros((vw, tq), F32))
                 for _ in range(hb))
    carries = lax.fori_loop(0, qi, lambda j, c: block(j, c, None), init)
    ki = lax.broadcasted_iota(jnp.int32, (tq, tq), 0)
    qj = lax.broadcasted_iota(jnp.int32, (tq, tq), 1)
    carries = block(qi, carries, ki <= qj)
    for h in range(hb):
        _, l_fin, acc = carries[h]
        o_ref[:, h * vw:(h + 1) * vw] = (acc / l_fin).T.astype(o_ref.dtype)


def _flash(q, k, v, batch, seq, tq=512, hb=4):
    t = q.shape[0]
    tq = min(tq, seq)
    nq = seq // tq
    ng = MLA_HEADS // hb
    return pl.pallas_call(
        functools.partial(_flash_kernel, tq=tq, hb=hb),
        grid=(batch, ng, nq),
        in_specs=[pl.BlockSpec((tq, hb * 2 * LANES), lambda b, g, i: (b * nq + i, g)),
                  pl.BlockSpec((seq, hb * 2 * LANES), lambda b, g, i: (b, g)),
                  pl.BlockSpec((hb * MLA_V, seq), lambda b, g, i: (g, b))],
        out_specs=pl.BlockSpec((tq, hb * MLA_V), lambda b, g, i: (b * nq + i, g)),
        out_shape=jax.ShapeDtypeStruct((t, MLA_HEADS * MLA_V), F32),
        compiler_params=_params("arbitrary", "arbitrary", "arbitrary"),
        name="mla_flash",
    )(q, k, v)


def _finish(h_ref, acc, fnw_ref, o_ref, final):
    hn = h_ref[...] + acc
    o_ref[...] = _rms(hn, fnw_ref[...]) if final else hn


def _even_tail_kernel(a_ref, ys_ref, u_ref, q_ref, z_ref, h_ref, mk_ref, mv_ref, vec_ref, gw_ref, ow_ref,
                      fnw_ref, o_ref, *, final):
    w = RW_WIDTH
    z = z_ref[...]
    ga = (a_ref[...] * _silu(z[:, :w])).astype(BF16)
    yb = _gelu_tanh(ys_ref[...] + vec_ref[0:1, :] * u_ref[...])
    gate = _sigmoid(_dot(yb.astype(BF16), gw_ref[...]) + vec_ref[1:2, :])
    gb = (yb * gate * _silu(z[:, w:2 * w])).astype(BF16)
    gm = (_mem_attend(q_ref[...], mk_ref[0], mv_ref[0]) * _silu(z[:, 2 * w:])).astype(BF16)
    acc = _dot(ga, ow_ref[0:w, :]) + _dot(gb, ow_ref[w:2 * w, :]) + _dot(gm, ow_ref[2 * w:, :])
    _finish(h_ref, acc, fnw_ref, o_ref, final)


def _odd_tail_kernel(c_ref, d_ref, q_ref, z_ref, h_ref, mk_ref, mv_ref, ow_ref, fnw_ref, o_ref, *, final):
    w = D_MODEL
    z = z_ref[...]
    gc = c_ref[...].astype(BF16)
    gd = (d_ref[...] * _silu(z[:, :w])).astype(BF16)
    gm = (_mem_attend(q_ref[...], mk_ref[0], mv_ref[0]) * _silu(z[:, w:])).astype(BF16)
    acc = _dot(gc, ow_ref[0:w, :]) + _dot(gd, ow_ref[w:2 * w, :]) + _dot(gm, ow_ref[2 * w:, :])
    _finish(h_ref, acc, fnw_ref, o_ref, final)


def _tail_call(kernel, rows, consts, h, mk, mv, fnw, batch, seq, final, name, tm=256):
    t, d = h.shape
    nt = seq // tm
    row = lambda a: pl.BlockSpec((tm, a.shape[1]), lambda b, i: (b * nt + i, 0))
    full = lambda a: pl.BlockSpec(a.shape, lambda b, i: (0,) * a.ndim)
    memspec = pl.BlockSpec((1,) + mk.shape[1:], lambda b, i: (b, 0, 0))
    fnw = fnw.reshape(1, d)
    return pl.pallas_call(
        functools.partial(kernel, final=final),
        grid=(batch, nt),
        in_specs=[row(a) for a in rows] + [row(h), memspec, memspec] + [full(a) for a in consts] + [full(fnw)],
        out_specs=row(h),
        out_shape=jax.ShapeDtypeStruct((t, d), F32),
        compiler_params=_params("arbitrary", "arbitrary"),
        name=name,
    )(*rows, h, mk, mv, *consts, fnw)


def _pad_rows(rows, n=SUBLANES):
    width = rows[0].shape[-1]
    return jnp.concatenate([r.reshape(1, width) for r in rows] + [jnp.zeros((n - len(rows), width), F32)], axis=0)


def _even_layer(h, batch, seq, mk, mv, nw, in_w, out_w, fnw, final, mu, w0, w2, a0, a2, k_k, k_a, r_k,
                ln_w, ln_b, s5_ops, s5_d, glu_w, glu_b):
    rw_proj = 3 * RW_WIDTH + 2 * RW_LORA
    ev_width = 2 * RW_WIDTH + MEM_WIDTH
    splits = (rw_proj, RW_WIDTH, MEM_WIDTH, ev_width)
    p, u, q_mem, z = _norm_proj(h, nw, in_w.astype(BF16), splits, (F32, F32, F32, F32))
    rvec, wa = _rwkv_param_pack(w0, w2, a0, a2, k_k, k_a, r_k, ln_w, ln_b)
    a_out = _rwkv(p, mu, rvec, wa, batch, seq)
    ys = _s5(u, *s5_ops, batch, seq)

    vec = _pad_rows([s5_d, glu_b])
    return _tail_call(_even_tail_kernel, [a_out, ys, u, q_mem, z], [vec, glu_w.astype(BF16), out_w.astype(BF16)],
                      h, mk, mv, fnw, batch, seq, final, "even_tail")


def _odd_layer(h, batch, seq, mk, mv, cos, sin, nw, in_w, out_w, fnw, final, conv_w, conv_b, dt_bias, a_log,
               m2_d, m2_norm_w, q_norm_w, wq_up, kv_norm_w, wkv_up):
    width = M2_HEADS * HEAD_DIM
    conv_dim = width + 2 * M2_GROUPS * M2_STATE
    nh = MLA_HEADS
    o_dt = conv_dim
    o_cq = o_dt + M2_HEADS
    o_ckv = o_cq + MLA_Q_RANK
    o_kr = o_ckv + MLA_KV_RANK
    o_qm = o_kr + MLA_ROPE
    o_z = o_qm + MEM_WIDTH
    half = MLA_ROPE // 2
    w_kr = in_w[:, o_kr:o_qm]
    w_kr_sw = jnp.concatenate([w_kr[:, half:], w_kr[:, :half]], axis=1)
    w_misc = jnp.concatenate([w_kr, w_kr, w_kr_sw, w_kr_sw, in_w[:, o_dt:o_cq],
                              jnp.zeros((in_w.shape[0], LANES - M2_HEADS), in_w.dtype)], axis=1)
    w_all = jnp.concatenate([in_w[:, :o_dt], in_w[:, o_cq:o_ckv], in_w[:, o_ckv:o_kr], in_w[:, o_qm:o_z],
                             in_w[:, o_z:], w_misc], axis=1).astype(BF16)
    splits = (conv_dim, MLA_Q_RANK, MLA_KV_RANK, MEM_WIDTH, width, D_MODEL + MEM_WIDTH, 3 * LANES)
    xbc, cq, ckv, q_mem, z_c, z_dm, misc = _norm_proj(h, nw, w_all, splits, (F32,) * 7)

    cw = _pad_rows(list(conv_w) + [conv_b])
    pad16 = lambda x: jnp.concatenate([x, jnp.zeros((LANES - M2_HEADS,), F32)])
    vec = _pad_rows([pad16(dt_bias), pad16(-jnp.exp(a_log))])
    nwd = _pad_rows([m2_norm_w, jnp.repeat(m2_d, HEAD_DIM)])
    c_out = _ssd(xbc, misc, z_c, cw, vec, nwd, batch, seq)

    wq = wq_up.reshape(MLA_Q_RANK, nh, MLA_NOPE + MLA_ROPE)
    wq_r = wq[:, :, MLA_NOPE:]
    wq_sw = jnp.concatenate([wq_r[:, :, half:], wq_r[:, :, :half]], axis=2)
    wq_all = jnp.concatenate([wq[:, :, :MLA_NOPE].reshape(MLA_Q_RANK, -1), wq_r.reshape(MLA_Q_RANK, -1),
                              wq_sw.reshape(MLA_Q_RANK, -1)], axis=1).astype(BF16)
    wkv = wkv_up.reshape(MLA_KV_RANK, nh, MLA_NOPE + MLA_V)
    wkv_all = jnp.concatenate([wkv[:, :, :MLA_NOPE].reshape(MLA_KV_RANK, -1),
                               wkv[:, :, MLA_NOPE:].reshape(MLA_KV_RANK, -1)], axis=1).astype(BF16)
    qc, kc, vv = _mla_prep(cq, ckv, misc, cos, sin, q_norm_w, kv_norm_w, wq_all, wkv_all)
    d_attn = _flash(qc, kc, vv, batch, seq)

    return _tail_call(_odd_tail_kernel, [c_out, d_attn, q_mem, z_dm], [out_w.astype(BF16)],
                      h, mk, mv, fnw, batch, seq, final, "odd_tail")


def _rope_tables(positions):
    inv = 1.0 / (ROPE_THETA ** (jnp.arange(0, MLA_ROPE, 2, dtype=F32) / MLA_ROPE))
    ang = positions.astype(F32).reshape(-1, 1) * inv
    cos, sin = jnp.cos(ang), jnp.sin(ang)
    return jnp.tile(cos, (1, 4)), jnp.tile(jnp.concatenate([-sin, sin], axis=1), (1, 2))


def kernel(x, mem, positions, norm_w, mem_norm_w, final_norm_w, mem_kv_w, ev_in_w, ev_out_w, rw_mu, rw_w0, rw_w2, rw_a0, rw_a2, rw_k_k, rw_k_a, rw_r_k, rw_ln_w, rw_ln_b, s5_lambda_re, s5_lambda_im, s5_b_re, s5_b_im, s5_c_re, s5_c_im, s5_d, s5_log_dt, s5_glu_w, s5_glu_b, od_in_w, od_out_w, m2_conv_w, m2_conv_b, m2_dt_bias, m2_a_log, m2_d, m2_norm_w, mla_q_norm_w, mla_wq_up, mla_kv_norm_w, mla_wkv_up):
    batch, seq, d = x.shape
    depth = norm_w.shape[0]
    mk, mv = _mem_kv(mem, mem_norm_w, mem_kv_w.astype(BF16))
    cos, sin = _rope_tables(positions)
    s5_ops = jax.vmap(functools.partial(_s5_param_pack, nj=seq // S5_SUB))(
        s5_lambda_re, s5_lambda_im, s5_b_re, s5_b_im, s5_c_re, s5_c_im, s5_log_dt)
    ev_in_w, ev_out_w, s5_glu_w = ev_in_w.astype(BF16), ev_out_w.astype(BF16), s5_glu_w.astype(BF16)
    od_out_w = od_out_w.astype(BF16)
    h = x.reshape(batch * seq, d)
    for layer in range(depth):
        i = layer // 2
        final = layer == depth - 1
        if layer % 2 == 0:
            h = _even_layer(h, batch, seq, mk[layer], mv[layer], norm_w[layer], ev_in_w[i], ev_out_w[i],
                            final_norm_w, final, rw_mu[i], rw_w0[i], rw_w2[i], rw_a0[i], rw_a2[i], rw_k_k[i],
                            rw_k_a[i], rw_r_k[i], rw_ln_w[i], rw_ln_b[i], tuple(op[i] for op in s5_ops), s5_d[i],
                            s5_glu_w[i], s5_glu_b[i])
        else:
            h = _odd_layer(h, batch, seq, mk[layer], mv[layer], cos, sin, norm_w[layer], od_in_w[i], od_out_w[i],
                           final_norm_w, final, m2_conv_w[i], m2_conv_b[i], m2_dt_bias[i], m2_a_log[i], m2_d[i],
                           m2_norm_w[i], mla_q_norm_w[i], mla_wq_up[i], mla_kv_norm_w[i], mla_wkv_up[i])
    return h.reshape(batch, seq, d)
```

```python
import functools
import math

import jax
import jax.numpy as jnp
from jax import lax
from jax.experimental import pallas as pl
from jax.experimental.pallas import tpu as pltpu

F32 = jnp.float32
BF16 = jnp.bfloat16
HIGHEST = lax.Precision.HIGHEST

LANES = 128
SUBLANES = 8
VMEM_LIMIT_BYTES = 56 * 1024 * 1024

D_MODEL = 1024
HEAD_DIM = 64
NORM_EPS = 1e-6
RW_WIDTH = 1024
RW_LORA = 64
RW_GN_EPS = 64e-5
RW_CHUNK = 64
S5_GROUP = 16
S5_STATE = 64
S5_SUB = 16
M2_HEADS = 16
M2_GROUPS = 2
M2_STATE = 128
M2_CONV = 4
M2_CHUNK = 128
MLA_HEADS = 8
MLA_NOPE = 128
MLA_ROPE = 64
MLA_V = 128
MLA_Q_RANK = 384
MLA_KV_RANK = 256
ROPE_THETA = 10000.0
MEM_HEADS = 4
MEM_WIDTH = MEM_HEADS * HEAD_DIM


def _dot(a, b, precision=None):
    return jnp.dot(a, b, preferred_element_type=F32, precision=precision)


def _dot_nt(a, b, precision=None):
    return lax.dot_general(a, b, (((1,), (1,)), ((), ())), preferred_element_type=F32,
                           precision=precision)


def _sigmoid(x):
    return 1.0 / (1.0 + jnp.exp(-x))


def _silu(x):
    return x * _sigmoid(x)


def _softplus(x):
    return jnp.maximum(x, 0.0) + jnp.log(1.0 + jnp.exp(-jnp.abs(x)))


def _gelu_tanh(x):
    return 0.5 * x * (1.0 + jnp.tanh(math.sqrt(2.0 / math.pi) * (x + 0.044715 * (x * x * x))))


def _rms(x, w, eps=NORM_EPS):
    ms = jnp.mean(x * x, axis=-1, keepdims=True)
    return x * lax.rsqrt(ms + eps) * w


def _params(*sem):
    return pltpu.CompilerParams(dimension_semantics=sem, vmem_limit_bytes=VMEM_LIMIT_BYTES)


def _norm_proj_kernel(x_ref, nw_ref, w_ref, *out_refs, splits):
    xn = _rms(x_ref[...], nw_ref[...]).astype(BF16)
    off = 0
    for o_ref, n in zip(out_refs, splits):
        o_ref[...] = _dot(xn, w_ref[:, off:off + n]).astype(o_ref.dtype)
        off += n


def _norm_proj(x, nw, w, splits, dtypes, tm=256):
    t, d = x.shape
    n = w.shape[1]
    assert sum(splits) == n and t % tm == 0
    return pl.pallas_call(
        functools.partial(_norm_proj_kernel, splits=splits),
        grid=(t // tm,),
        in_specs=[pl.BlockSpec((tm, d), lambda i: (i, 0)),
                  pl.BlockSpec((1, d), lambda i: (0, 0)),
                  pl.BlockSpec((d, n), lambda i: (0, 0))],
        out_specs=[pl.BlockSpec((tm, s), lambda i: (i, 0)) for s in splits],
        out_shape=[jax.ShapeDtypeStruct((t, s), dt) for s, dt in zip(splits, dtypes)],
        compiler_params=_params("arbitrary"),
        name="norm_proj",
    )(x, nw.reshape(1, d), w)


def _mem_kv_kernel(mem_ref, nw_ref, w_ref, k_ref, v_ref):
    mn = _rms(mem_ref[0], nw_ref[...]).astype(BF16)
    kv = _dot(mn, w_ref[0])
    k_ref[0, 0] = kv[:, :MEM_WIDTH].astype(k_ref.dtype)
    v_ref[0, 0] = kv[:, MEM_WIDTH:].astype(v_ref.dtype)


def _mem_kv(mem, nw, w):
    b, m, d = mem.shape
    depth = w.shape[0]
    shp = jax.ShapeDtypeStruct((depth, b, m, MEM_WIDTH), BF16)
    return pl.pallas_call(
        _mem_kv_kernel,
        grid=(depth, b),
        in_specs=[pl.BlockSpec((1, m, d), lambda l, i: (i, 0, 0)),
                  pl.BlockSpec((1, d), lambda l, i: (0, 0)),
                  pl.BlockSpec((1, d, 2 * MEM_WIDTH), lambda l, i: (l, 0, 0))],
        out_specs=[pl.BlockSpec((1, 1, m, MEM_WIDTH), lambda l, i: (l, i, 0, 0))] * 2,
        out_shape=[shp, shp],
        compiler_params=_params("arbitrary", "arbitrary"),
        name="mem_kv",
    )(mem, nw.reshape(1, d), w)


def _mem_attend(q, k, v):
    lane = lax.broadcasted_iota(jnp.int32, (1, MEM_WIDTH), 1)
    qb = (q * (HEAD_DIM ** -0.5)).astype(BF16)
    hs = range(MEM_HEADS)
    hm = [(lane >> 6) == h for h in hs]
    sc = [_dot_nt(qb, jnp.where(hm[h], k, jnp.zeros_like(k))) for h in hs]
    p = [jnp.exp(sc[h] - jnp.max(sc[h], axis=-1, keepdims=True)) for h in hs]
    inv = [1.0 / jnp.sum(p[h], axis=-1, keepdims=True) for h in hs]
    pv = [_dot(p[h].astype(BF16), jnp.where(hm[h], v, jnp.zeros_like(v))) * inv[h] for h in hs]
    return functools.reduce(lambda a, b: a + b, pv)


def _split3(x):
    hi = x.astype(BF16)
    r1 = x - hi.astype(F32)
    mid = r1.astype(BF16)
    lo = (r1 - mid.astype(F32)).astype(BF16)
    return hi, mid, lo


def _rwkv_kernel(p_ref, mu_ref, vec_ref, wa_ref, out_ref, st_ref, prev_ref, *, L):
    c = pl.program_id(1)
    W = RW_WIDTH

    @pl.when(c == 0)
    def _():
        st_ref[...] = jnp.zeros_like(st_ref)
        prev_ref[...] = jnp.zeros_like(prev_ref)

    x = p_ref[...]
    R = x.shape[0]
    nck = R // L
    row_l = lax.broadcasted_iota(jnp.int32, (R, 1), 0)
    xs = jnp.where(row_l == 0, prev_ref[0:1, :], pltpu.roll(x, 1, axis=0))
    prev_ref[0:1, :] = x[R - 1:R, :]
    xm = x + (xs - x) * mu_ref[...]
    r, k, v, lat = xm[:, :W], xm[:, W:2 * W], xm[:, 2 * W:3 * W], xm[:, 3 * W:]

    w0, a0, k_k, k_a, r_k, ln_w, ln_b = [vec_ref[i:i + 1, :] for i in range(7)]
    lane = lax.broadcasted_iota(jnp.int32, (1, LANES), 1)
    lat_t = jnp.where(lane < RW_LORA, jnp.tanh(lat), lat).astype(BF16)
    la = _dot(lat_t, wa_ref[...])
    log_w = -_softplus(-(w0 + la[:, :W])) - 0.5
    lw = -jnp.exp(log_w)
    iclr = _sigmoid(a0 + la[:, W:])
    kk0 = k * k_k
    kp = k * (1.0 + (iclr - 1.0) * k_a)
    rkr = r * kp * r_k

    il = lax.broadcasted_iota(jnp.int32, (R, 3 * R), 0)
    jl = lax.broadcasted_iota(jnp.int32, (R, 3 * R), 1) & (R - 1)
    same_chunk = (il & -L) == (jl & -L)
    tri3 = jnp.where(same_chunk & (jl <= il), 1.0, 0.0).astype(BF16)
    cum = _dot(tri3, jnp.concatenate(_split3(lw), axis=0))
    cum_ls = [cum[(ck + 1) * L - 1:(ck + 1) * L, :] for ck in range(nck)]
    cum_end = jnp.concatenate([jnp.broadcast_to(cl, (L, W)) for cl in cum_ls], axis=0)
    e_neg = jnp.exp(-cum)
    e_exc = jnp.exp(cum - lw)
    e_inc = jnp.exp(cum)
    dend = jnp.exp(cum_end - cum)
    e_tot = [jnp.exp(cl) for cl in cum_ls]

    i2 = lax.broadcasted_iota(jnp.int32, (LANES, LANES), 0)
    j2 = lax.broadcasted_iota(jnp.int32, (LANES, LANES), 1)
    same_head = (i2 >> 6) == (j2 >> 6)
    head_ones = jnp.where(same_head, 1.0, 0.0).astype(BF16)
    strict = same_head & (j2 < i2)
    incl = same_head & (j2 <= i2)
    eye = jnp.where(i2 == j2, 1.0, 0.0)
    levels = [((i2 & -(2 * s)) == (j2 & -(2 * s))) & ((i2 & s) != 0) & ((j2 & s) == 0)
              for s in (1, 2, 4, 8, 16, 32)]
    m0 = lane < HEAD_DIM
    stack = lambda t: jnp.concatenate([jnp.where(m0, t, 0.0), jnp.where(m0, 0.0, t)], axis=0)

    npair = W // LANES
    units = [(slice(ck * L, (ck + 1) * L), slice(p * LANES, (p + 1) * LANES))
             for ck in range(nck) for p in range(npair)]
    each = lambda f: [f(i) for i in range(len(units))]
    at = lambda arr, i: arr[units[i][0], units[i][1]]
    sums = each(lambda i: _dot(jnp.concatenate([at(kk0, i) * at(kk0, i), at(rkr, i)], axis=0).astype(BF16),
                               head_ones))
    kk = each(lambda i: at(kk0, i) * lax.rsqrt(jnp.maximum(sums[i][:L], 1e-24)))
    bvec = each(lambda i: kk[i] * at(iclr, i))
    ar = each(lambda i: jnp.concatenate([stack(-kk[i] * at(e_exc, i)), stack(at(r, i) * at(e_inc, i))],
                                        axis=0).astype(BF16))
    b2 = each(lambda i: stack(bvec[i] * at(e_neg, i)).astype(BF16))
    k2 = each(lambda i: stack(at(kp, i) * at(e_neg, i)).astype(BF16))
    v2f = each(lambda i: stack(at(v, i)))
    v2 = each(lambda i: v2f[i].astype(BF16))
    bk_end = each(lambda i: jnp.concatenate([stack(bvec[i] * at(dend, i)), stack(at(kp, i) * at(dend, i))],
                                            axis=0).astype(BF16))
    sb = each(lambda i: _dot_nt(ar[i], b2[i]))
    sk = each(lambda i: _dot_nt(ar[i], k2[i]))
    aab = each(lambda i: jnp.where(strict, sb[i][:LANES], 0.0))

    inv = each(lambda i: eye + jnp.where(levels[0], aab[i], 0.0))
    for lvl in levels[1:]:
        invb = each(lambda i: inv[i].astype(BF16))
        t1 = each(lambda i: _dot(invb[i], jnp.where(lvl, aab[i], 0.0).astype(BF16)).astype(BF16))
        inv = each(lambda i: inv[i] + _dot(t1[i], invb[i]))
    aak_v = each(lambda i: _dot(jnp.where(strict, sk[i][:LANES], 0.0).astype(BF16), v2[i]))
    rk_v = each(lambda i: _dot(jnp.where(incl, sk[i][LANES:], 0.0).astype(BF16), v2[i]))

    st = [st_ref[p] for p in range(npair)]
    y = [None] * len(units)
    for ck in range(nck):
        ids = [ck * npair + p for p in range(npair)]
        sta = [_dot_nt(ar[i], st[p].astype(BF16)) for p, i in enumerate(ids)]
        u2 = [_dot(inv[i].astype(BF16), (sta[p][:LANES] + aak_v[i]).astype(BF16)) for p, i in enumerate(ids)]
        y2 = [sta[p][LANES:] + _dot(jnp.where(incl, sb[i][LANES:], 0.0).astype(BF16), u2[p].astype(BF16)) + rk_v[i]
              for p, i in enumerate(ids)]
        for p, i in enumerate(ids):
            y[i] = y2[p][:L, :] + y2[p][L:, :]
            uv_t = jnp.concatenate([u2[p].T, v2f[i].T], axis=1).astype(BF16)
            st[p] = st[p] * e_tot[ck][:, units[i][1]] + _dot(uv_t, bk_end[i])
    for p in range(npair):
        st_ref[p] = st[p]

    mean = each(lambda i: _dot(y[i].astype(BF16), head_ones) * (1.0 / HEAD_DIM))
    dcen = each(lambda i: y[i] - mean[i])
    var = each(lambda i: _dot((dcen[i] * dcen[i]).astype(BF16), head_ones) * (1.0 / HEAD_DIM))
    for i, (rs, ls_) in enumerate(units):
        out_ref[rs, ls_] = (dcen[i] * lax.rsqrt(var[i] + RW_GN_EPS) * ln_w[:, ls_] + ln_b[:, ls_]
                            + sums[i][L:] * at(v, i))


def _rwkv(p, mu, vec, wa, batch, seq, nck=2):
    L = RW_CHUNK * (nck if seq % (nck * RW_CHUNK) == 0 else 1)
    t, pw = p.shape
    nc = seq // L
    return pl.pallas_call(
        functools.partial(_rwkv_kernel, L=RW_CHUNK),
        grid=(batch, nc),
        in_specs=[pl.BlockSpec((L, pw), lambda b, c: (b * nc + c, 0)),
                  pl.BlockSpec((1, pw), lambda b, c: (0, 0)),
                  pl.BlockSpec(vec.shape, lambda b, c: (0, 0)),
                  pl.BlockSpec(wa.shape, lambda b, c: (0, 0))],
        out_specs=pl.BlockSpec((L, RW_WIDTH), lambda b, c: (b * nc + c, 0)),
        out_shape=jax.ShapeDtypeStruct((t, RW_WIDTH), F32),
        scratch_shapes=[pltpu.VMEM((RW_WIDTH // LANES, LANES, LANES), F32), pltpu.VMEM((SUBLANES, pw), F32)],
        compiler_params=_params("arbitrary", "arbitrary"),
        name="rwkv7",
    )(p, mu.reshape(1, pw), vec, wa)


def _rwkv_param_pack(w0, w2, a0, a2, k_k, k_a, r_k, ln_w, ln_b):
    vec = _pad_rows([w0, a0, k_k, k_a, r_k.reshape(-1), ln_w, ln_b])
    z = jnp.zeros_like(w2)
    wa = jnp.concatenate([jnp.concatenate([w2, z], axis=1), jnp.concatenate([z, a2], axis=1)], axis=0)
    return vec, wa.astype(BF16)


def _s5_kernel(u_ref, tz_ref, gm_ref, cc_ref, pw_ref, y_ref, tzd_ref, gmd_ref, ccd_ref, *, nj):
    ls, pch, n = S5_SUB, S5_GROUP, S5_STATE
    gs = LANES // pch

    @pl.when(pl.program_id(1) == 0)
    def _():
        tzd_ref[...] = jnp.zeros_like(tzd_ref)
        gmd_ref[...] = jnp.zeros_like(gmd_ref)
        ccd_ref[...] = jnp.zeros_like(ccd_ref)
        for g in range(gs):
            for a in range(ls):
                r0 = a * LANES + g * pch
                for b in range(a, ls):
                    c0 = b * LANES + g * pch
                    tzd_ref[r0:r0 + pch, c0:c0 + pch] = tz_ref[g, a * pch:(a + 1) * pch, b * pch:(b + 1) * pch]
                for c in range(2):
                    gmd_ref[r0:r0 + pch, (c * gs + g) * n:(c * gs + g + 1) * n] = (
                        gm_ref[g, a * pch:(a + 1) * pch, c * n:(c + 1) * n])
                    ccd_ref[(c * gs + g) * n:(c * gs + g + 1) * n, r0:r0 + pch] = (
                        cc_ref[g, c * n:(c + 1) * n, a * pch:(a + 1) * pch])

    rows = u_ref.shape[0] // ls
    xcat = jnp.concatenate([u_ref[pl.ds(l, rows, stride=ls), :] for l in range(ls)], axis=1).astype(BF16)
    otile = 2 * LANES
    y = jnp.concatenate([_dot(xcat[:, :(m + 1) * otile], tzd_ref[:(m + 1) * otile, m * otile:(m + 1) * otile])
                         for m in range(ls * LANES // otile)], axis=1)
    x = _dot(xcat, gmd_ref[...])
    half = x.shape[1] // 2
    jrow = lax.broadcasted_iota(jnp.int32, (rows, 1), 0) & (nj - 1)
    pw = pw_ref[0]
    d, lvl = 1, 0
    while d < nj:
        xs = jnp.where(jrow >= d, pltpu.roll(x, d, axis=0), 0.0)
        xsw = jnp.concatenate([xs[:, half:], xs[:, :half]], axis=1)
        x = x + xs * pw[2 * lvl:2 * lvl + 1, :] + xsw * pw[2 * lvl + 1:2 * lvl + 2, :]
        d, lvl = 2 * d, lvl + 1
    h_in = jnp.where(jrow >= 1, pltpu.roll(x, 1, axis=0), 0.0)
    y = y + _dot(h_in.astype(BF16), ccd_ref[...])
    for l in range(ls):
        y_ref[pl.ds(l, rows, stride=ls), :] = y[:, l * LANES:(l + 1) * LANES]


def _s5(u, tz, gm, cc, pw, batch, seq, nb=2):
    t, width = u.shape
    nslab = width // LANES
    gs = LANES // S5_GROUP
    nj = seq // S5_SUB
    sub_w = S5_SUB * LANES
    st_w = 2 * gs * S5_STATE
    pwd = pw.reshape(nslab, gs, pw.shape[1], 2, S5_STATE).transpose(0, 2, 3, 1, 4).reshape(nslab, pw.shape[1], st_w)
    grp_spec = lambda a: pl.BlockSpec((gs,) + a.shape[1:], lambda s, b: (s, 0, 0))
    nb = nb if batch % nb == 0 else 1
    assert nj & (nj - 1) == 0
    return pl.pallas_call(
        functools.partial(_s5_kernel, nj=nj),
        grid=(nslab, batch // nb),
        in_specs=[pl.BlockSpec((nb * seq, LANES), lambda s, b: (b, s)),
                  grp_spec(tz), grp_spec(gm), grp_spec(cc),
                  pl.BlockSpec((1,) + pwd.shape[1:], lambda s, b: (s, 0, 0))],
        out_specs=pl.BlockSpec((nb * seq, LANES), lambda s, b: (b, s)),
        out_shape=jax.ShapeDtypeStruct((t, width), F32),
        scratch_shapes=[pltpu.VMEM((sub_w, sub_w), BF16), pltpu.VMEM((sub_w, st_w), BF16),
                        pltpu.VMEM((st_w, sub_w), BF16)],
        compiler_params=_params("arbitrary", "arbitrary"),
        name="s5",
    )(u, tz, gm, cc, pwd)


def _s5_param_pack(lam_re, lam_im, b_re, b_im, c_re, c_im, log_dt, nj):
    ls = S5_SUB
    dt = jnp.exp(log_dt)[:, None]
    zr, zi = lam_re * dt, lam_im * dt

    def powers(steps):
        st = steps[:, None, None]
        mag = jnp.exp(st * zr)
        return mag * jnp.cos(st * zi), mag * jnp.sin(st * zi)

    ab_re, ab_im = powers(jnp.ones((1,), F32))
    ab_re, ab_im = ab_re[0], ab_im[0]
    den = lam_re * lam_re + lam_im * lam_im
    nr, ni = ab_re - 1.0, ab_im
    f_re = (nr * lam_re + ni * lam_im) / den
    f_im = (ni * lam_re - nr * lam_im) / den
    bb_re = f_re[..., None] * b_re - f_im[..., None] * b_im
    bb_im = f_re[..., None] * b_im + f_im[..., None] * b_re
    pr, pi = powers(jnp.arange(0, ls + 1, dtype=F32))
    cl_re = c_re[None] * pr[:, :, None, :] - c_im[None] * pi[:, :, None, :]
    cl_im = c_re[None] * pi[:, :, None, :] + c_im[None] * pr[:, :, None, :]
    kern = jnp.einsum('dgpn,gnq->dgpq', jnp.concatenate([cl_re[:ls], -cl_im[:ls]], axis=3),
                      jnp.concatenate([bb_re, bb_im], axis=1), precision=lax.Precision.HIGH)
    tz = jnp.stack([jnp.pad(kern[:ls - a], ((a, 0), (0, 0), (0, 0), (0, 0))) for a in range(ls)])
    g = lam_re.shape[0]
    pch = S5_GROUP
    tz = tz.transpose(2, 0, 4, 1, 3).reshape(g, ls * pch, ls * pch)
    rr, ri = pr[ls - 1::-1][:ls], pi[ls - 1::-1][:ls]
    gm_re = rr[..., None] * bb_re[None] - ri[..., None] * bb_im[None]
    gm_im = rr[..., None] * bb_im[None] + ri[..., None] * bb_re[None]
    gm = jnp.concatenate([gm_re, gm_im], axis=2)
    gm = gm.transpose(1, 0, 3, 2).reshape(g, ls * pch, 2 * S5_STATE)
    cc = jnp.concatenate([cl_re[1:], -cl_im[1:]], axis=3)
    cc = cc.transpose(1, 3, 0, 2).reshape(g, 2 * S5_STATE, ls * pch)
    lv = []
    d = 1
    while d < nj:
        lv.append(float(ls * d))
        d *= 2
    qr, qi = powers(jnp.asarray(lv, F32))
    pw = jnp.stack([jnp.concatenate([qr, qr], axis=-1), jnp.concatenate([-qi, qi], axis=-1)], axis=1)
    pw = pw.reshape(2 * len(lv), g, 2 * S5_STATE).transpose(1, 0, 2)
    pw = jnp.concatenate([pw, jnp.zeros((g, 16 - 2 * len(lv), 2 * S5_STATE), F32)], axis=1)
    return tz.astype(BF16), gm.astype(BF16), cc.astype(BF16), pw


def _ssd_kernel(xbc_ref, misc_ref, zc_ref, cw_ref, vec_ref, nw_ref, out_ref,
                st_ref, halo_ref, buf_ref, *, L):
    c = pl.program_id(1)
    width = M2_HEADS * HEAD_DIM
    gs = M2_STATE

    @pl.when(c == 0)
    def _():
        st_ref[...] = jnp.zeros_like(st_ref)
        halo_ref[...] = jnp.zeros_like(halo_ref)

    xbc = xbc_ref[...]
    buf_ref[0:SUBLANES, :] = halo_ref[...]
    buf_ref[SUBLANES:SUBLANES + L, :] = xbc
    halo_ref[...] = xbc[L - SUBLANES:L, :]
    cw = cw_ref[...]
    nshift = M2_CONV - 1
    si = lax.broadcasted_iota(jnp.int32, (nshift * L, L + SUBLANES), 0)
    sj = lax.broadcasted_iota(jnp.int32, (nshift * L, L + SUBLANES), 1)
    tap_i = (si >= L).astype(jnp.int32) + (si >= 2 * L).astype(jnp.int32)
    shift = jnp.where(sj == (si - tap_i * L) + tap_i + (SUBLANES - nshift), 1.0, 0.0).astype(BF16)
    zsh = _dot(shift, buf_ref[...].astype(BF16))
    acc = cw[M2_CONV:M2_CONV + 1, :] + cw[nshift:M2_CONV, :] * xbc
    for tap in range(nshift):
        acc = acc + cw[tap:tap + 1, :] * zsh[tap * L:(tap + 1) * L, :]
    xc = _silu(acc)
    xs = xc[:, :width]

    vec = vec_ref[...]
    dt = _softplus(misc_ref[:, 2 * LANES:3 * LANES] + vec[0:1, :])
    a_dt = dt * vec[1:2, :]
    il = lax.broadcasted_iota(jnp.int32, (L, 3 * L), 0)
    jl = lax.broadcasted_iota(jnp.int32, (L, 3 * L), 1) & (L - 1)
    tri3 = jnp.where(jl <= il, 1.0, 0.0).astype(BF16)
    cum = _dot(tri3, jnp.concatenate(_split3(a_dt), axis=0))
    cum_t = cum.T

    eh = lax.broadcasted_iota(jnp.int32, (3 * LANES, width), 0) & (LANES - 1)
    ec = lax.broadcasted_iota(jnp.int32, (3 * LANES, width), 1)
    expand3 = jnp.where((ec >> 6) == eh, 1.0, 0.0).astype(BF16)
    dt_x = _dot(jnp.concatenate(_split3(dt), axis=1), expand3)
    cum_x = _dot(jnp.concatenate(_split3(cum), axis=1), expand3)
    cum_lx = cum_x[L - 1:L, :]
    xdt = xs * dt_x
    xdd = (xdt * jnp.exp(cum_lx - cum_x)).astype(BF16)
    xdt_b = xdt.astype(BF16)
    ecum = jnp.exp(cum_x)
    e_tot = jnp.exp(cum_lx)

    tril = (lax.broadcasted_iota(jnp.int32, (L, L), 1) <= lax.broadcasted_iota(jnp.int32, (L, L), 0))
    lane = lax.broadcasted_iota(jnp.int32, (1, LANES), 1)
    m0 = lane < HEAD_DIM
    hpg = M2_HEADS // M2_GROUPS
    gw = hpg * HEAD_DIM
    groups = range(M2_GROUPS)
    heads = range(M2_HEADS)
    bm_f = [xc[:, width + g * gs:width + (g + 1) * gs] for g in groups]
    bm = [b.astype(BF16) for b in bm_f]
    cm = [xc[:, width + (M2_GROUPS + g) * gs:width + (M2_GROUPS + g + 1) * gs].astype(BF16) for g in groups]
    cbm = [jnp.where(tril, _dot_nt(cm[g], bm[g]), 0.0) for g in groups]
    st = [st_ref[:, g * gw:(g + 1) * gw] for g in groups]
    y_off = [_dot(cm[g], st[g].astype(BF16)) for g in groups]
    for g in groups:
        st_ref[:, g * gw:(g + 1) * gw] = (st[g] * e_tot[:, g * gw:(g + 1) * gw]
                                          + _dot(bm_f[g].T.astype(BF16), xdd[:, g * gw:(g + 1) * gw]))
    mh = [(cbm[h // hpg] * jnp.exp(jnp.minimum(cum[:, h:h + 1] - cum_t[h:h + 1, :], 0.0))).astype(BF16)
          for h in heads]
    zero = jnp.zeros((), BF16)
    yd = [_dot(mh[h], jnp.where(m0 if h % 2 == 0 else jnp.logical_not(m0),
                                xdt_b[:, (h // 2) * LANES:(h // 2 + 1) * LANES], zero)) for h in heads]
    y_diag = jnp.concatenate([yd[2 * p] + yd[2 * p + 1] for p in range(M2_HEADS // 2)], axis=1)
    y = y_diag + jnp.concatenate(y_off, axis=1) * ecum + xs * nw_ref[1:2, :]
    y = y * _silu(zc_ref[...])
    gw = width // M2_GROUPS
    outs = []
    for g in range(M2_GROUPS):
        yg = y[:, g * gw:(g + 1) * gw]
        outs.append(_rms(yg, nw_ref[0:1, g * gw:(g + 1) * gw]))
    out_ref[...] = jnp.concatenate(outs, axis=1)


def _ssd(xbc, misc, z, cw, vec, nw, batch, seq):
    L = M2_CHUNK
    t, cd = xbc.shape
    nc = seq // L
    width = M2_HEADS * HEAD_DIM
    return pl.pallas_call(
        functools.partial(_ssd_kernel, L=L),
        grid=(batch, nc),
        in_specs=[pl.BlockSpec((L, cd), lambda b, c: (b * nc + c, 0)),
                  pl.BlockSpec((L, misc.shape[1]), lambda b, c: (b * nc + c, 0)),
                  pl.BlockSpec((L, width), lambda b, c: (b * nc + c, 0)),
                  pl.BlockSpec((SUBLANES, cd), lambda b, c: (0, 0)),
                  pl.BlockSpec((SUBLANES, LANES), lambda b, c: (0, 0)),
                  pl.BlockSpec((SUBLANES, width), lambda b, c: (0, 0))],
        out_specs=pl.BlockSpec((L, width), lambda b, c: (b * nc + c, 0)),
        out_shape=jax.ShapeDtypeStruct((t, width), F32),
        scratch_shapes=[pltpu.VMEM((M2_STATE, width), F32), pltpu.VMEM((SUBLANES, cd), F32),
                        pltpu.VMEM((L + SUBLANES, cd), F32)],
        compiler_params=_params("arbitrary", "arbitrary"),
        name="ssd",
    )(xbc, misc, z, cw, vec, nw)


def _mla_prep_kernel(cq_ref, ckv_ref, misc_ref, cos_ref, sin_ref, qnw_ref, kvnw_ref, wq_ref, wkv_ref,
                     q_ref, k_ref, v_ref):
    nh = MLA_HEADS
    cos, sin = cos_ref[...], sin_ref[...]
    q = _dot(_rms(cq_ref[...], qnw_ref[...]).astype(BF16), wq_ref[...])
    scale = (MLA_NOPE + MLA_ROPE) ** -0.5 * math.log2(math.e)
    rope0 = nh * MLA_NOPE
    rot0 = rope0 + nh * MLA_ROPE
    for pair in range(nh // 2):
        sl = slice(pair * LANES, (pair + 1) * LANES)
        qr = (q[:, rope0:rot0][:, sl] * cos + q[:, rot0:][:, sl] * sin) * scale
        for hh in range(2):
            h = 2 * pair + hh
            q_ref[:, h * 2 * LANES:h * 2 * LANES + LANES] = (
                q[:, h * MLA_NOPE:(h + 1) * MLA_NOPE] * scale).astype(q_ref.dtype)
            q_ref[:, h * 2 * LANES + LANES:(h + 1) * 2 * LANES] = qr.astype(q_ref.dtype)
    kv = _dot(_rms(ckv_ref[...], kvnw_ref[...]).astype(BF16), wkv_ref[...])
    kpe = misc_ref[:, :LANES] * cos + misc_ref[:, LANES:2 * LANES] * sin
    lane = lax.broadcasted_iota(jnp.int32, (1, LANES), 1)
    slots = [jnp.where(lane < MLA_ROPE, kpe, 0.0), jnp.where(lane < MLA_ROPE, 0.0, kpe)]
    for h in range(nh):
        k_ref[:, h * 2 * LANES:h * 2 * LANES + LANES] = kv[:, h * MLA_NOPE:(h + 1) * MLA_NOPE].astype(k_ref.dtype)
        k_ref[:, h * 2 * LANES + LANES:(h + 1) * 2 * LANES] = slots[h % 2].astype(k_ref.dtype)
    v_ref[...] = kv[:, nh * MLA_NOPE:].T.astype(v_ref.dtype)


def _mla_prep(cq, ckv, misc, cos, sin, qnw, kvnw, wq, wkv, tm=256):
    t = cq.shape[0]
    nh = MLA_HEADS
    row = lambda n: pl.BlockSpec((tm, n), lambda i: (i, 0))
    full = lambda a: pl.BlockSpec(a.shape, lambda i: (0, 0))
    qnw, kvnw = qnw.reshape(1, -1), kvnw.reshape(1, -1)
    return pl.pallas_call(
        _mla_prep_kernel,
        grid=(t // tm,),
        in_specs=[row(cq.shape[1]), row(ckv.shape[1]), row(misc.shape[1]), row(LANES), row(LANES),
                  full(qnw), full(kvnw), full(wq), full(wkv)],
        out_specs=[row(nh * 2 * LANES), row(nh * 2 * LANES), pl.BlockSpec((nh * MLA_V, tm), lambda i: (0, i))],
        out_shape=[jax.ShapeDtypeStruct((t, nh * 2 * LANES), BF16),
                   jax.ShapeDtypeStruct((t, nh * 2 * LANES), BF16),
                   jax.ShapeDtypeStruct((nh * MLA_V, t), BF16)],
        compiler_params=_params("arbitrary"),
        name="mla_prep",
    )(cq, ckv, misc, cos, sin, qnw, kvnw, wq, wkv)


def _flash_kernel(q_ref, k_ref, v_ref, o_ref, *, tq, hb):
    qi = pl.program_id(2)
    qw, vw = 2 * LANES, MLA_V
    qs = [q_ref[:, h * qw:(h + 1) * qw] for h in range(hb)]

    def block(j, carries, mask):
        off = pl.multiple_of(j * tq, tq)
        hs = range(hb)
        s = [_dot_nt(k_ref[pl.ds(off, tq), h * qw:(h + 1) * qw], qs[h]) for h in hs]
        if mask is not None:
            s = [jnp.where(mask, s[h], -jnp.inf) for h in hs]
        m_new = [jnp.maximum(carries[h][0], jnp.max(s[h], axis=0, keepdims=True)) for h in hs]
        alpha = [jnp.exp2(carries[h][0] - m_new[h]) for h in hs]
        p = [jnp.exp2(s[h] - m_new[h]) for h in hs]
        l_new = [alpha[h] * carries[h][1] + jnp.sum(p[h], axis=0, keepdims=True) for h in hs]
        pv = [_dot(v_ref[h * vw:(h + 1) * vw, pl.ds(off, tq)], p[h].astype(BF16)) for h in hs]
        return tuple((m_new[h], l_new[h], alpha[h] * carries[h][2] + pv[h]) for h in hs)

    init = tuple((jnp.full((1, tq), -jnp.inf, F32), jnp.zeros((1, tq), F32), jnp.zeros((vw, tq), F32))
                 for _ in range(hb))
    carries = lax.fori_loop(0, qi, lambda j, c: block(j, c, None), init)
    ki = lax.broadcasted_iota(jnp.int32, (tq, tq), 0)
    qj = lax.broadcasted_iota(jnp.int32, (tq, tq), 1)
    carries = block(qi, carries, ki <= qj)
    for h in range(hb):
        _, l_fin, acc = carries[h]
        o_ref[:, h * vw:(h + 1) * vw] = (acc / l_fin).T.astype(o_ref.dtype)


def _flash(q, k, v, batch, seq, tq=512, hb=4):
    t = q.shape[0]
    tq = min(tq, seq)
    nq = seq // tq
    ng = MLA_HEADS // hb
    return pl.pallas_call(
        functools.partial(_flash_kernel, tq=tq, hb=hb),
        grid=(batch, ng, nq),
        in_specs=[pl.BlockSpec((tq, hb * 2 * LANES), lambda b, g, i: (b * nq + i, g)),
                  pl.BlockSpec((seq, hb * 2 * LANES), lambda b, g, i: (b, g)),
                  pl.BlockSpec((hb * MLA_V, seq), lambda b, g, i: (g, b))],
        out_specs=pl.BlockSpec((tq, hb * MLA_V), lambda b, g, i: (b * nq + i, g)),
        out_shape=jax.ShapeDtypeStruct((t, MLA_HEADS * MLA_V), F32),
        compiler_params=_params("arbitrary", "arbitrary", "arbitrary"),
        name="mla_flash",
    )(q, k, v)


def _finish(h_ref, acc, fnw_ref, o_ref, final):
    hn = h_ref[...] + acc
    o_ref[...] = _rms(hn, fnw_ref[...]) if final else hn


def _even_tail_kernel(a_ref, ys_ref, u_ref, q_ref, z_ref, h_ref, mk_ref, mv_ref, vec_ref, gw_ref, ow_ref,
                      fnw_ref, o_ref, *, final):
    w = RW_WIDTH
    z = z_ref[...]
    ga = (a_ref[...] * _silu(z[:, :w])).astype(BF16)
    yb = _gelu_tanh(ys_ref[...] + vec_ref[0:1, :] * u_ref[...])
    gate = _sigmoid(_dot(yb.astype(BF16), gw_ref[...]) + vec_ref[1:2, :])
    gb = (yb * gate * _silu(z[:, w:2 * w])).astype(BF16)
    gm = (_mem_attend(q_ref[...], mk_ref[0], mv_ref[0]) * _silu(z[:, 2 * w:])).astype(BF16)
    acc = _dot(ga, ow_ref[0:w, :]) + _dot(gb, ow_ref[w:2 * w, :]) + _dot(gm, ow_ref[2 * w:, :])
    _finish(h_ref, acc, fnw_ref, o_ref, final)


def _odd_tail_kernel(c_ref, d_ref, q_ref, z_ref, h_ref, mk_ref, mv_ref, ow_ref, fnw_ref, o_ref, *, final):
    w = D_MODEL
    z = z_ref[...]
    gc = c_ref[...].astype(BF16)
    gd = (d_ref[...] * _silu(z[:, :w])).astype(BF16)
    gm = (_mem_attend(q_ref[...], mk_ref[0], mv_ref[0]) * _silu(z[:, w:])).astype(BF16)
    acc = _dot(gc, ow_ref[0:w, :]) + _dot(gd, ow_ref[w:2 * w, :]) + _dot(gm, ow_ref[2 * w:, :])
    _finish(h_ref, acc, fnw_ref, o_ref, final)


def _tail_call(kernel, rows, consts, h, mk, mv, fnw, batch, seq, final, name, tm=256):
    t, d = h.shape
    nt = seq // tm
    row = lambda a: pl.BlockSpec((tm, a.shape[1]), lambda b, i: (b * nt + i, 0))
    full = lambda a: pl.BlockSpec(a.shape, lambda b, i: (0,) * a.ndim)
    memspec = pl.BlockSpec((1,) + mk.shape[1:], lambda b, i: (b, 0, 0))
    fnw = fnw.reshape(1, d)
    return pl.pallas_call(
        functools.partial(kernel, final=final),
        grid=(batch, nt),
        in_specs=[row(a) for a in rows] + [row(h), memspec, memspec] + [full(a) for a in consts] + [full(fnw)],
        out_specs=row(h),
        out_shape=jax.ShapeDtypeStruct((t, d), F32),
        compiler_params=_params("arbitrary", "arbitrary"),
        name=name,
    )(*rows, h, mk, mv, *consts, fnw)


def _pad_rows(rows, n=SUBLANES):
    width = rows[0].shape[-1]
    return jnp.concatenate([r.reshape(1, width) for r in rows] + [jnp.zeros((n - len(rows), width), F32)], axis=0)


def _even_layer(h, batch, seq, mk, mv, nw, in_w, out_w, fnw, final, mu, w0, w2, a0, a2, k_k, k_a, r_k,
                ln_w, ln_b, s5_ops, s5_d, glu_w, glu_b):
    rw_proj = 3 * RW_WIDTH + 2 * RW_LORA
    ev_width = 2 * RW_WIDTH + MEM_WIDTH
    splits = (rw_proj, RW_WIDTH, MEM_WIDTH, ev_width)
    p, u, q_mem, z = _norm_proj(h, nw, in_w.astype(BF16), splits, (F32, F32, F32, F32))
    rvec, wa = _rwkv_param_pack(w0, w2, a0, a2, k_k, k_a, r_k, ln_w, ln_b)
    a_out = _rwkv(p, mu, rvec, wa, batch, seq)
    ys = _s5(u, *s5_ops, batch, seq)

    vec = _pad_rows([s5_d, glu_b])
    return _tail_call(_even_tail_kernel, [a_out, ys, u, q_mem, z], [vec, glu_w.astype(BF16), out_w.astype(BF16)],
                      h, mk, mv, fnw, batch, seq, final, "even_tail")


def _odd_layer(h, batch, seq, mk, mv, cos, sin, nw, in_w, out_w, fnw, final, conv_w, conv_b, dt_bias, a_log,
               m2_d, m2_norm_w, q_norm_w, wq_up, kv_norm_w, wkv_up):
    width = M2_HEADS * HEAD_DIM
    conv_dim = width + 2 * M2_GROUPS * M2_STATE
    nh = MLA_HEADS
    o_dt = conv_dim
    o_cq = o_dt + M2_HEADS
    o_ckv = o_cq + MLA_Q_RANK
    o_kr = o_ckv + MLA_KV_RANK
    o_qm = o_kr + MLA_ROPE
    o_z = o_qm + MEM_WIDTH
    half = MLA_ROPE // 2
    w_kr = in_w[:, o_kr:o_qm]
    w_kr_sw = jnp.concatenate([w_kr[:, half:], w_kr[:, :half]], axis=1)
    w_misc = jnp.concatenate([w_kr, w_kr, w_kr_sw, w_kr_sw, in_w[:, o_dt:o_cq],
                              jnp.zeros((in_w.shape[0], LANES - M2_HEADS), in_w.dtype)], axis=1)
    w_all = jnp.concatenate([in_w[:, :o_dt], in_w[:, o_cq:o_ckv], in_w[:, o_ckv:o_kr], in_w[:, o_qm:o_z],
                             in_w[:, o_z:], w_misc], axis=1).astype(BF16)
    splits = (conv_dim, MLA_Q_RANK, MLA_KV_RANK, MEM_WIDTH, width, D_MODEL + MEM_WIDTH, 3 * LANES)
    xbc, cq, ckv, q_mem, z_c, z_dm, misc = _norm_proj(h, nw, w_all, splits, (F32,) * 7)

    cw = _pad_rows(list(conv_w) + [conv_b])
    pad16 = lambda x: jnp.concatenate([x, jnp.zeros((LANES - M2_HEADS,), F32)])
    vec = _pad_rows([pad16(dt_bias), pad16(-jnp.exp(a_log))])
    nwd = _pad_rows([m2_norm_w, jnp.repeat(m2_d, HEAD_DIM)])
    c_out = _ssd(xbc, misc, z_c, cw, vec, nwd, batch, seq)

    wq = wq_up.reshape(MLA_Q_RANK, nh, MLA_NOPE + MLA_ROPE)
    wq_r = wq[:, :, MLA_NOPE:]
    wq_sw = jnp.concatenate([wq_r[:, :, half:], wq_r[:, :, :half]], axis=2)
    wq_all = jnp.concatenate([wq[:, :, :MLA_NOPE].reshape(MLA_Q_RANK, -1), wq_r.reshape(MLA_Q_RANK, -1),
                              wq_sw.reshape(MLA_Q_RANK, -1)], axis=1).astype(BF16)
    wkv = wkv_up.reshape(MLA_KV_RANK, nh, MLA_NOPE + MLA_V)
    wkv_all = jnp.concatenate([wkv[:, :, :MLA_NOPE].reshape(MLA_KV_RANK, -1),
                               wkv[:, :, MLA_NOPE:].reshape(MLA_KV_RANK, -1)], axis=1).astype(BF16)
    qc, kc, vv = _mla_prep(cq, ckv, misc, cos, sin, q_norm_w, kv_norm_w, wq_all, wkv_all)
    d_attn = _flash(qc, kc, vv, batch, seq)

    return _tail_call(_odd_tail_kernel, [c_out, d_attn, q_mem, z_dm], [out_w.astype(BF16)],
                      h, mk, mv, fnw, batch, seq, final, "odd_tail")


def _rope_tables(positions):
    inv = 1.0 / (ROPE_THETA ** (jnp.arange(0, MLA_ROPE, 2, dtype=F32) / MLA_ROPE))
    ang = positions.astype(F32).reshape(-1, 1) * inv
    cos, sin = jnp.cos(ang), jnp.sin(ang)
    return jnp.tile(cos, (1, 4)), jnp.tile(jnp.concatenate([-sin, sin], axis=1), (1, 2))


def kernel(x, mem, positions, norm_w, mem_norm_w, final_norm_w, mem_kv_w, ev_in_w, ev_out_w, rw_mu, rw_w0, rw_w2, rw_a0, rw_a2, rw_k_k, rw_k_a, rw_r_k, rw_ln_w, rw_ln_b, s5_lambda_re, s5_lambda_im, s5_b_re, s5_b_im, s5_c_re, s5_c_im, s5_d, s5_log_dt, s5_glu_w, s5_glu_b, od_in_w, od_out_w, m2_conv_w, m2_conv_b, m2_dt_bias, m2_a_log, m2_d, m2_norm_w, mla_q_norm_w, mla_wq_up, mla_kv_norm_w, mla_wkv_up):
    batch, seq, d = x.shape
    depth = norm_w.shape[0]
    mk, mv = _mem_kv(mem, mem_norm_w, mem_kv_w.astype(BF16))
    cos, sin = _rope_tables(positions)
    s5_ops = jax.vmap(functools.partial(_s5_param_pack, nj=seq // S5_SUB))(
        s5_lambda_re, s5_lambda_im, s5_b_re, s5_b_im, s5_c_re, s5_c_im, s5_log_dt)
    ev_in_w, ev_out_w, s5_glu_w = ev_in_w.astype(BF16), ev_out_w.astype(BF16), s5_glu_w.astype(BF16)
    od_out_w = od_out_w.astype(BF16)
    h = x.reshape(batch * seq, d)
    for layer in range(depth):
        i = layer // 2
        final = layer == depth - 1
        if layer % 2 == 0:
            h = _even_layer(h, batch, seq, mk[layer], mv[layer], norm_w[layer], ev_in_w[i], ev_out_w[i],
                            final_norm_w, final, rw_mu[i], rw_w0[i], rw_w2[i], rw_a0[i], rw_a2[i], rw_k_k[i],
                            rw_k_a[i], rw_r_k[i], rw_ln_w[i], rw_ln_b[i], tuple(op[i] for op in s5_ops), s5_d[i],
                            s5_glu_w[i], s5_glu_b[i])
        else:
            h = _odd_layer(h, batch, seq, mk[layer], mv[layer], cos, sin, norm_w[layer], od_in_w[i], od_out_w[i],
                           final_norm_w, final, m2_conv_w[i], m2_conv_b[i], m2_dt_bias[i], m2_a_log[i], m2_d[i],
                           m2_norm_w[i], mla_q_norm_w[i], mla_wq_up[i], mla_kv_norm_w[i], mla_wkv_up[i])
    return h.reshape(batch, seq, d)
```

```python
import functools
import math

import jax
import jax.numpy as jnp
from jax import lax
from jax.experimental import pallas as pl
from jax.experimental.pallas import tpu as pltpu

F32 = jnp.float32
BF16 = jnp.bfloat16
HIGHEST = lax.Precision.HIGHEST

LANES = 128
SUBLANES = 8
VMEM_LIMIT_BYTES = 56 * 1024 * 1024

D_MODEL = 1024
HEAD_DIM = 64
NORM_EPS = 1e-6
RW_WIDTH = 1024
RW_LORA = 64
RW_GN_EPS = 64e-5
RW_CHUNK = 64
S5_GROUP = 16
S5_STATE = 64
S5_SUB = 16
M2_HEADS = 16
M2_GROUPS = 2
M2_STATE = 128
M2_CONV = 4
M2_CHUNK = 128
MLA_HEADS = 8
MLA_NOPE = 128
MLA_ROPE = 64
MLA_V = 128
MLA_Q_RANK = 384
MLA_KV_RANK = 256
ROPE_THETA = 10000.0
MEM_HEADS = 4
MEM_WIDTH = MEM_HEADS * HEAD_DIM


def _dot(a, b, precision=None):
    return jnp.dot(a, b, preferred_element_type=F32, precision=precision)


def _dot_nt(a, b, precision=None):
    return lax.dot_general(a, b, (((1,), (1,)), ((), ())), preferred_element_type=F32,
                           precision=precision)


def _sigmoid(x):
    return 1.0 / (1.0 + jnp.exp(-x))


def _silu(x):
    return x * _sigmoid(x)


def _softplus(x):
    return jnp.maximum(x, 0.0) + jnp.log(1.0 + jnp.exp(-jnp.abs(x)))


def _gelu_tanh(x):
    return 0.5 * x * (1.0 + jnp.tanh(math.sqrt(2.0 / math.pi) * (x + 0.044715 * (x * x * x))))


def _rms(x, w, eps=NORM_EPS):
    ms = jnp.mean(x * x, axis=-1, keepdims=True)
    return x * lax.rsqrt(ms + eps) * w


def _params(*sem):
    return pltpu.CompilerParams(dimension_semantics=sem, vmem_limit_bytes=VMEM_LIMIT_BYTES)


def _norm_proj_kernel(x_ref, nw_ref, w_ref, *out_refs, splits):
    xn = _rms(x_ref[...], nw_ref[...]).astype(BF16)
    off = 0
    for o_ref, n in zip(out_refs, splits):
        o_ref[...] = _dot(xn, w_ref[:, off:off + n]).astype(o_ref.dtype)
        off += n


def _norm_proj(x, nw, w, splits, dtypes, tm=256):
    t, d = x.shape
    n = w.shape[1]
    assert sum(splits) == n and t % tm == 0
    return pl.pallas_call(
        functools.partial(_norm_proj_kernel, splits=splits),
        grid=(t // tm,),
        in_specs=[pl.BlockSpec((tm, d), lambda i: (i, 0)),
                  pl.BlockSpec((1, d), lambda i: (0, 0)),
                  pl.BlockSpec((d, n), lambda i: (0, 0))],
        out_specs=[pl.BlockSpec((tm, s), lambda i: (i, 0)) for s in splits],
        out_shape=[jax.ShapeDtypeStruct((t, s), dt) for s, dt in zip(splits, dtypes)],
        compiler_params=_params("arbitrary"),
        name="norm_proj",
    )(x, nw.reshape(1, d), w)


def _mem_kv_kernel(mem_ref, nw_ref, w_ref, k_ref, v_ref):
    mn = _rms(mem_ref[0], nw_ref[...]).astype(BF16)
    kv = _dot(mn, w_ref[0])
    k_ref[0, 0] = kv[:, :MEM_WIDTH].astype(k_ref.dtype)
    v_ref[0, 0] = kv[:, MEM_WIDTH:].astype(v_ref.dtype)


def _mem_kv(mem, nw, w):
    b, m, d = mem.shape
    depth = w.shape[0]
    shp = jax.ShapeDtypeStruct((depth, b, m, MEM_WIDTH), BF16)
    return pl.pallas_call(
        _mem_kv_kernel,
        grid=(depth, b),
        in_specs=[pl.BlockSpec((1, m, d), lambda l, i: (i, 0, 0)),
                  pl.BlockSpec((1, d), lambda l, i: (0, 0)),
                  pl.BlockSpec((1, d, 2 * MEM_WIDTH), lambda l, i: (l, 0, 0))],
        out_specs=[pl.BlockSpec((1, 1, m, MEM_WIDTH), lambda l, i: (l, i, 0, 0))] * 2,
        out_shape=[shp, shp],
        compiler_params=_params("arbitrary", "arbitrary"),
        name="mem_kv",
    )(mem, nw.reshape(1, d), w)


def _mem_attend(q, k, v):
    lane = lax.broadcasted_iota(jnp.int32, (1, MEM_WIDTH), 1)
    qb = (q * (HEAD_DIM ** -0.5)).astype(BF16)
    hs = range(MEM_HEADS)
    hm = [(lane >> 6) == h for h in hs]
    sc = [_dot_nt(qb, jnp.where(hm[h], k, jnp.zeros_like(k))) for h in hs]
    p = [jnp.exp(sc[h] - jnp.max(sc[h], axis=-1, keepdims=True)) for h in hs]
    inv = [1.0 / jnp.sum(p[h], axis=-1, keepdims=True) for h in hs]
    pv = [_dot(p[h].astype(BF16), jnp.where(hm[h], v, jnp.zeros_like(v))) * inv[h] for h in hs]
    return functools.reduce(lambda a, b: a + b, pv)


def _split3(x):
    hi = x.astype(BF16)
    r1 = x - hi.astype(F32)
    mid = r1.astype(BF16)
    lo = (r1 - mid.astype(F32)).astype(BF16)
    return hi, mid, lo


def _rwkv_kernel(p_ref, mu_ref, vec_ref, wa_ref, out_ref, st_ref, prev_ref, *, L):
    c = pl.program_id(1)
    W = RW_WIDTH

    @pl.when(c == 0)
    def _():
        st_ref[...] = jnp.zeros_like(st_ref)
        prev_ref[...] = jnp.zeros_like(prev_ref)

    x = p_ref[...]
    R = x.shape[0]
    nck = R // L
    row_l = lax.broadcasted_iota(jnp.int32, (R, 1), 0)
    xs = jnp.where(row_l == 0, prev_ref[0:1, :], pltpu.roll(x, 1, axis=0))
    prev_ref[0:1, :] = x[R - 1:R, :]
    xm = x + (xs - x) * mu_ref[...]
    r, k, v, lat = xm[:, :W], xm[:, W:2 * W], xm[:, 2 * W:3 * W], xm[:, 3 * W:]

    w0, a0, k_k, k_a, r_k, ln_w, ln_b = [vec_ref[i:i + 1, :] for i in range(7)]
    lane = lax.broadcasted_iota(jnp.int32, (1, LANES), 1)
    lat_t = jnp.where(lane < RW_LORA, jnp.tanh(lat), lat).astype(BF16)
    la = _dot(lat_t, wa_ref[...])
    log_w = -_softplus(-(w0 + la[:, :W])) - 0.5
    lw = -jnp.exp(log_w)
    iclr = _sigmoid(a0 + la[:, W:])
    kk0 = k * k_k
    kp = k * (1.0 + (iclr - 1.0) * k_a)
    rkr = r * kp * r_k

    il = lax.broadcasted_iota(jnp.int32, (R, 3 * R), 0)
    jl = lax.broadcasted_iota(jnp.int32, (R, 3 * R), 1) & (R - 1)
    same_chunk = (il & -L) == (jl & -L)
    tri3 = jnp.where(same_chunk & (jl <= il), 1.0, 0.0).astype(BF16)
    cum = _dot(tri3, jnp.concatenate(_split3(lw), axis=0))
    cum_ls = [cum[(ck + 1) * L - 1:(ck + 1) * L, :] for ck in range(nck)]
    cum_end = jnp.concatenate([jnp.broadcast_to(cl, (L, W)) for cl in cum_ls], axis=0)
    e_neg = jnp.exp(-cum)
    e_exc = jnp.exp(cum - lw)
    e_inc = jnp.exp(cum)
    dend = jnp.exp(cum_end - cum)
    e_tot = [jnp.exp(cl) for cl in cum_ls]

    i2 = lax.broadcasted_iota(jnp.int32, (LANES, LANES), 0)
    j2 = lax.broadcasted_iota(jnp.int32, (LANES, LANES), 1)
    same_head = (i2 >> 6) == (j2 >> 6)
    head_ones = jnp.where(same_head, 1.0, 0.0).astype(BF16)
    strict = same_head & (j2 < i2)
    incl = same_head & (j2 <= i2)
    eye = jnp.where(i2 == j2, 1.0, 0.0)
    levels = [((i2 & -(2 * s)) == (j2 & -(2 * s))) & ((i2 & s) != 0) & ((j2 & s) == 0)
              for s in (1, 2, 4, 8, 16, 32)]
    m0 = lane < HEAD_DIM
    stack = lambda t: jnp.concatenate([jnp.where(m0, t, 0.0), jnp.where(m0, 0.0, t)], axis=0)

    npair = W // LANES
    units = [(slice(ck * L, (ck + 1) * L), slice(p * LANES, (p + 1) * LANES))
             for ck in range(nck) for p in range(npair)]
    each = lambda f: [f(i) for i in range(len(units))]
    at = lambda arr, i: arr[units[i][0], units[i][1]]
    sums = each(lambda i: _dot(jnp.concatenate([at(kk0, i) * at(kk0, i), at(rkr, i)], axis=0).astype(BF16),
                               head_ones))
    kk = each(lambda i: at(kk0, i) * lax.rsqrt(jnp.maximum(sums[i][:L], 1e-24)))
    bvec = each(lambda i: kk[i] * at(iclr, i))
    ar = each(lambda i: jnp.concatenate([stack(-kk[i] * at(e_exc, i)), stack(at(r, i) * at(e_inc, i))],
                                        axis=0).astype(BF16))
    b2 = each(lambda i: stack(bvec[i] * at(e_neg, i)).astype(BF16))
    k2 = each(lambda i: stack(at(kp, i) * at(e_neg, i)).astype(BF16))
    v2f = each(lambda i: stack(at(v, i)))
    v2 = each(lambda i: v2f[i].astype(BF16))
    bk_end = each(lambda i: jnp.concatenate([stack(bvec[i] * at(dend, i)), stack(at(kp, i) * at(dend, i))],
                                            axis=0).astype(BF16))
    sb = each(lambda i: _dot_nt(ar[i], b2[i]))
    sk = each(lambda i: _dot_nt(ar[i], k2[i]))
    aab = each(lambda i: jnp.where(strict, sb[i][:LANES], 0.0))

    inv = each(lambda i: eye + jnp.where(levels[0], aab[i], 0.0))
    for lvl in levels[1:]:
        invb = each(lambda i: inv[i].astype(BF16))
        t1 = each(lambda i: _dot(invb[i], jnp.where(lvl, aab[i], 0.0).astype(BF16)).astype(BF16))
        inv = each(lambda i: inv[i] + _dot(t1[i], invb[i]))
    aak_v = each(lambda i: _dot(jnp.where(strict, sk[i][:LANES], 0.0).astype(BF16), v2[i]))
    rk_v = each(lambda i: _dot(jnp.where(incl, sk[i][LANES:], 0.0).astype(BF16), v2[i]))

    st = [st_ref[p] for p in range(npair)]
    y = [None] * len(units)
    for ck in range(nck):
        ids = [ck * npair + p for p in range(npair)]
        sta = [_dot_nt(ar[i], st[p].astype(BF16)) for p, i in enumerate(ids)]
        u2 = [_dot(inv[i].astype(BF16), (sta[p][:LANES] + aak_v[i]).astype(BF16)) for p, i in enumerate(ids)]
        y2 = [sta[p][LANES:] + _dot(jnp.where(incl, sb[i][LANES:], 0.0).astype(BF16), u2[p].astype(BF16)) + rk_v[i]
              for p, i in enumerate(ids)]
        for p, i in enumerate(ids):
            y[i] = y2[p][:L, :] + y2[p][L:, :]
            uv_t = jnp.concatenate([u2[p].T, v2f[i].T], axis=1).astype(BF16)
            st[p] = st[p] * e_tot[ck][:, units[i][1]] + _dot(uv_t, bk_end[i])
    for p in range(npair):
        st_ref[p] = st[p]

    mean = each(lambda i: _dot(y[i].astype(BF16), head_ones) * (1.0 / HEAD_DIM))
    dcen = each(lambda i: y[i] - mean[i])
    var = each(lambda i: _dot((dcen[i] * dcen[i]).astype(BF16), head_ones) * (1.0 / HEAD_DIM))
    for i, (rs, ls_) in enumerate(units):
        out_ref[rs, ls_] = (dcen[i] * lax.rsqrt(var[i] + RW_GN_EPS) * ln_w[:, ls_] + ln_b[:, ls_]
                            + sums[i][L:] * at(v, i))


def _rwkv(p, mu, vec, wa, batch, seq, nck=2):
    L = RW_CHUNK * (nck if seq % (nck * RW_CHUNK) == 0 else 1)
    t, pw = p.shape
    nc = seq // L
    return pl.pallas_call(
        functools.partial(_rwkv_kernel, L=RW_CHUNK),
        grid=(batch, nc),
        in_specs=[pl.BlockSpec((L, pw), lambda b, c: (b * nc + c, 0)),
                  pl.BlockSpec((1, pw), lambda b, c: (0, 0)),
                  pl.BlockSpec(vec.shape, lambda b, c: (0, 0)),
                  pl.BlockSpec(wa.shape, lambda b, c: (0, 0))],
        out_specs=pl.BlockSpec((L, RW_WIDTH), lambda b, c: (b * nc + c, 0)),
        out_shape=jax.ShapeDtypeStruct((t, RW_WIDTH), F32),
        scratch_shapes=[pltpu.VMEM((RW_WIDTH // LANES, LANES, LANES), F32), pltpu.VMEM((SUBLANES, pw), F32)],
        compiler_params=_params("arbitrary", "arbitrary"),
        name="rwkv7",
    )(p, mu.reshape(1, pw), vec, wa)


def _rwkv_param_pack(w0, w2, a0, a2, k_k, k_a, r_k, ln_w, ln_b):
    vec = _pad_rows([w0, a0, k_k, k_a, r_k.reshape(-1), ln_w, ln_b])
    z = jnp.zeros_like(w2)
    wa = jnp.concatenate([jnp.concatenate([w2, z], axis=1), jnp.concatenate([z, a2], axis=1)], axis=0)
    return vec, wa.astype(BF16)


def _s5_kernel(u_ref, tz_ref, gm_ref, cc_ref, pw_ref, y_ref, tzd_ref, gmd_ref, ccd_ref, *, nj):
    ls, pch, n = S5_SUB, S5_GROUP, S5_STATE
    gs = LANES // pch

    @pl.when(pl.program_id(1) == 0)
    def _():
        tzd_ref[...] = jnp.zeros_like(tzd_ref)
        gmd_ref[...] = jnp.zeros_like(gmd_ref)
        ccd_ref[...] = jnp.zeros_like(ccd_ref)
        for g in range(gs):
            for a in range(ls):
                r0 = a * LANES + g * pch
                for b in range(a, ls):
                    c0 = b * LANES + g * pch
                    tzd_ref[r0:r0 + pch, c0:c0 + pch] = tz_ref[g, a * pch:(a + 1) * pch, b * pch:(b + 1) * pch]
                for c in range(2):
                    gmd_ref[r0:r0 + pch, (c * gs + g) * n:(c * gs + g + 1) * n] = (
                        gm_ref[g, a * pch:(a + 1) * pch, c * n:(c + 1) * n])
                    ccd_ref[(c * gs + g) * n:(c * gs + g + 1) * n, r0:r0 + pch] = (
                        cc_ref[g, c * n:(c + 1) * n, a * pch:(a + 1) * pch])

    rows = u_ref.shape[0] // ls
    xcat = jnp.concatenate([u_ref[pl.ds(l, rows, stride=ls), :] for l in range(ls)], axis=1).astype(BF16)
    otile = 2 * LANES
    y = jnp.concatenate([_dot(xcat[:, :(m + 1) * otile], tzd_ref[:(m + 1) * otile, m * otile:(m + 1) * otile])
                         for m in range(ls * LANES // otile)], axis=1)
    x = _dot(xcat, gmd_ref[...])
    half = x.shape[1] // 2
    jrow = lax.broadcasted_iota(jnp.int32, (rows, 1), 0) & (nj - 1)
    pw = pw_ref[0]
    d, lvl = 1, 0
    while d < nj:
        xs = jnp.where(jrow >= d, pltpu.roll(x, d, axis=0), 0.0)
        xsw = jnp.concatenate([xs[:, half:], xs[:, :half]], axis=1)
        x = x + xs * pw[2 * lvl:2 * lvl + 1, :] + xsw * pw[2 * lvl + 1:2 * lvl + 2, :]
        d, lvl = 2 * d, lvl + 1
    h_in = jnp.where(jrow >= 1, pltpu.roll(x, 1, axis=0), 0.0)
    y = y + _dot(h_in.astype(BF16), ccd_ref[...])
    for l in range(ls):
        y_ref[pl.ds(l, rows, stride=ls), :] = y[:, l * LANES:(l + 1) * LANES]


def _s5(u, tz, gm, cc, pw, batch, seq, nb=2):
    t, width = u.shape
    nslab = width // LANES
    gs = LANES // S5_GROUP
    nj = seq // S5_SUB
    sub_w = S5_SUB * LANES
    st_w = 2 * gs * S5_STATE
    pwd = pw.reshape(nslab, gs, pw.shape[1], 2, S5_STATE).transpose(0, 2, 3, 1, 4).reshape(nslab, pw.shape[1], st_w)
    grp_spec = lambda a: pl.BlockSpec((gs,) + a.shape[1:], lambda s, b: (s, 0, 0))
    nb = nb if batch % nb == 0 else 1
    assert nj & (nj - 1) == 0
    return pl.pallas_call(
        functools.partial(_s5_kernel, nj=nj),
        grid=(nslab, batch // nb),
        in_specs=[pl.BlockSpec((nb * seq, LANES), lambda s, b: (b, s)),
                  grp_spec(tz), grp_spec(gm), grp_spec(cc),
                  pl.BlockSpec((1,) + pwd.shape[1:], lambda s, b: (s, 0, 0))],
        out_specs=pl.BlockSpec((nb * seq, LANES), lambda s, b: (b, s)),
        out_shape=jax.ShapeDtypeStruct((t, width), F32),
        scratch_shapes=[pltpu.VMEM((sub_w, sub_w), BF16), pltpu.VMEM((sub_w, st_w), BF16),
                        pltpu.VMEM((st_w, sub_w), BF16)],
        compiler_params=_params("arbitrary", "arbitrary"),
        name="s5",
    )(u, tz, gm, cc, pwd)


def _s5_param_pack(lam_re, lam_im, b_re, b_im, c_re, c_im, log_dt, nj):
    ls = S5_SUB
    dt = jnp.exp(log_dt)[:, None]
    zr, zi = lam_re * dt, lam_im * dt

    def powers(steps):
        st = steps[:, None, None]
        mag = jnp.exp(st * zr)
        return mag * jnp.cos(st * zi), mag * jnp.sin(st * zi)

    ab_re, ab_im = powers(jnp.ones((1,), F32))
    ab_re, ab_im = ab_re[0], ab_im[0]
    den = lam_re * lam_re + lam_im * lam_im
    nr, ni = ab_re - 1.0, ab_im
    f_re = (nr * lam_re + ni * lam_im) / den
    f_im = (ni * lam_re - nr * lam_im) / den
    bb_re = f_re[..., None] * b_re - f_im[..., None] * b_im
    bb_im = f_re[..., None] * b_im + f_im[..., None] * b_re
    pr, pi = powers(jnp.arange(0, ls + 1, dtype=F32))
    cl_re = c_re[None] * pr[:, :, None, :] - c_im[None] * pi[:, :, None, :]
    cl_im = c_re[None] * pi[:, :, None, :] + c_im[None] * pr[:, :, None, :]
    kern = jnp.einsum('dgpn,gnq->dgpq', jnp.concatenate([cl_re[:ls], -cl_im[:ls]], axis=3),
                      jnp.concatenate([bb_re, bb_im], axis=1), precision=lax.Precision.HIGH)
    lag = jnp.arange(ls)[None, :] - jnp.arange(ls)[:, None]
    tz = jnp.where((lag >= 0)[:, :, None, None, None], kern[jnp.clip(lag, 0, ls - 1)], 0.0)
    g = lam_re.shape[0]
    pch = S5_GROUP
    tz = tz.transpose(2, 0, 4, 1, 3).reshape(g, ls * pch, ls * pch)
    rr, ri = pr[ls - 1::-1][:ls], pi[ls - 1::-1][:ls]
    gm_re = rr[..., None] * bb_re[None] - ri[..., None] * bb_im[None]
    gm_im = rr[..., None] * bb_im[None] + ri[..., None] * bb_re[None]
    gm = jnp.concatenate([gm_re, gm_im], axis=2)
    gm = gm.transpose(1, 0, 3, 2).reshape(g, ls * pch, 2 * S5_STATE)
    cc = jnp.concatenate([cl_re[1:], -cl_im[1:]], axis=3)
    cc = cc.transpose(1, 3, 0, 2).reshape(g, 2 * S5_STATE, ls * pch)
    lv = []
    d = 1
    while d < nj:
        lv.append(float(ls * d))
        d *= 2
    qr, qi = powers(jnp.asarray(lv, F32))
    pw = jnp.stack([jnp.concatenate([qr, qr], axis=-1), jnp.concatenate([-qi, qi], axis=-1)], axis=1)
    pw = pw.reshape(2 * len(lv), g, 2 * S5_STATE).transpose(1, 0, 2)
    pw = jnp.concatenate([pw, jnp.zeros((g, 16 - 2 * len(lv), 2 * S5_STATE), F32)], axis=1)
    return tz.astype(BF16), gm.astype(BF16), cc.astype(BF16), pw


def _ssd_kernel(xbc_ref, misc_ref, zc_ref, cw_ref, vec_ref, nw_ref, out_ref,
                st_ref, halo_ref, buf_ref, *, L):
    c = pl.program_id(1)
    width = M2_HEADS * HEAD_DIM
    gs = M2_STATE

    @pl.when(c == 0)
    def _():
        st_ref[...] = jnp.zeros_like(st_ref)
        halo_ref[...] = jnp.zeros_like(halo_ref)

    xbc = xbc_ref[...]
    buf_ref[0:SUBLANES, :] = halo_ref[...]
    buf_ref[SUBLANES:SUBLANES + L, :] = xbc
    halo_ref[...] = xbc[L - SUBLANES:L, :]
    cw = cw_ref[...]
    nshift = M2_CONV - 1
    si = lax.broadcasted_iota(jnp.int32, (nshift * L, L + SUBLANES), 0)
    sj = lax.broadcasted_iota(jnp.int32, (nshift * L, L + SUBLANES), 1)
    tap_i = (si >= L).astype(jnp.int32) + (si >= 2 * L).astype(jnp.int32)
    shift = jnp.where(sj == (si - tap_i * L) + tap_i + (SUBLANES - nshift), 1.0, 0.0).astype(BF16)
    zsh = _dot(shift, buf_ref[...].astype(BF16))
    acc = cw[M2_CONV:M2_CONV + 1, :] + cw[nshift:M2_CONV, :] * xbc
    for tap in range(nshift):
        acc = acc + cw[tap:tap + 1, :] * zsh[tap * L:(tap + 1) * L, :]
    xc = _silu(acc)
    xs = xc[:, :width]

    vec = vec_ref[...]
    dt = _softplus(misc_ref[:, 2 * LANES:3 * LANES] + vec[0:1, :])
    a_dt = dt * vec[1:2, :]
    il = lax.broadcasted_iota(jnp.int32, (L, 3 * L), 0)
    jl = lax.broadcasted_iota(jnp.int32, (L, 3 * L), 1) & (L - 1)
    tri3 = jnp.where(jl <= il, 1.0, 0.0).astype(BF16)
    cum = _dot(tri3, jnp.concatenate(_split3(a_dt), axis=0))
    cum_t = cum.T

    eh = lax.broadcasted_iota(jnp.int32, (3 * LANES, width), 0) & (LANES - 1)
    ec = lax.broadcasted_iota(jnp.int32, (3 * LANES, width), 1)
    expand3 = jnp.where((ec >> 6) == eh, 1.0, 0.0).astype(BF16)
    dt_x = _dot(jnp.concatenate(_split3(dt), axis=1), expand3)
    cum_x = _dot(jnp.concatenate(_split3(cum), axis=1), expand3)
    cum_lx = cum_x[L - 1:L, :]
    xdt = xs * dt_x
    xdd = (xdt * jnp.exp(cum_lx - cum_x)).astype(BF16)
    xdt_b = xdt.astype(BF16)
    ecum = jnp.exp(cum_x)
    e_tot = jnp.exp(cum_lx)

    tril = (lax.broadcasted_iota(jnp.int32, (L, L), 1) <= lax.broadcasted_iota(jnp.int32, (L, L), 0))
    lane = lax.broadcasted_iota(jnp.int32, (1, LANES), 1)
    m0 = lane < HEAD_DIM
    hpg = M2_HEADS // M2_GROUPS
    gw = hpg * HEAD_DIM
    groups = range(M2_GROUPS)
    heads = range(M2_HEADS)
    bm_f = [xc[:, width + g * gs:width + (g + 1) * gs] for g in groups]
    bm = [b.astype(BF16) for b in bm_f]
    cm = [xc[:, width + (M2_GROUPS + g) * gs:width + (M2_GROUPS + g + 1) * gs].astype(BF16) for g in groups]
    cbm = [jnp.where(tril, _dot_nt(cm[g], bm[g]), 0.0) for g in groups]
    st = [st_ref[:, g * gw:(g + 1) * gw] for g in groups]
    y_off = [_dot(cm[g], st[g].astype(BF16)) for g in groups]
    for g in groups:
        st_ref[:, g * gw:(g + 1) * gw] = (st[g] * e_tot[:, g * gw:(g + 1) * gw]
                                          + _dot(bm_f[g].T.astype(BF16), xdd[:, g * gw:(g + 1) * gw]))
    mh = [(cbm[h // hpg] * jnp.exp(jnp.minimum(cum[:, h:h + 1] - cum_t[h:h + 1, :], 0.0))).astype(BF16)
          for h in heads]
    zero = jnp.zeros((), BF16)
    yd = [_dot(mh[h], jnp.where(m0 if h % 2 == 0 else jnp.logical_not(m0),
                                xdt_b[:, (h // 2) * LANES:(h // 2 + 1) * LANES], zero)) for h in heads]
    y_diag = jnp.concatenate([yd[2 * p] + yd[2 * p + 1] for p in range(M2_HEADS // 2)], axis=1)
    y = y_diag + jnp.concatenate(y_off, axis=1) * ecum + xs * nw_ref[1:2, :]
    y = y * _silu(zc_ref[...])
    gw = width // M2_GROUPS
    outs = []
    for g in range(M2_GROUPS):
        yg = y[:, g * gw:(g + 1) * gw]
        outs.append(_rms(yg, nw_ref[0:1, g * gw:(g + 1) * gw]))
    out_ref[...] = jnp.concatenate(outs, axis=1)


def _ssd(xbc, misc, z, cw, vec, nw, batch, seq):
    L = M2_CHUNK
    t, cd = xbc.shape
    nc = seq // L
    width = M2_HEADS * HEAD_DIM
    return pl.pallas_call(
        functools.partial(_ssd_kernel, L=L),
        grid=(batch, nc),
        in_specs=[pl.BlockSpec((L, cd), lambda b, c: (b * nc + c, 0)),
                  pl.BlockSpec((L, misc.shape[1]), lambda b, c: (b * nc + c, 0)),
                  pl.BlockSpec((L, width), lambda b, c: (b * nc + c, 0)),
                  pl.BlockSpec((SUBLANES, cd), lambda b, c: (0, 0)),
                  pl.BlockSpec((SUBLANES, LANES), lambda b, c: (0, 0)),
                  pl.BlockSpec((SUBLANES, width), lambda b, c: (0, 0))],
        out_specs=pl.BlockSpec((L, width), lambda b, c: (b * nc + c, 0)),
        out_shape=jax.ShapeDtypeStruct((t, width), F32),
        scratch_shapes=[pltpu.VMEM((M2_STATE, width), F32), pltpu.VMEM((SUBLANES, cd), F32),
                        pltpu.VMEM((L + SUBLANES, cd), F32)],
        compiler_params=_params("arbitrary", "arbitrary"),
        name="ssd",
    )(xbc, misc, z, cw, vec, nw)


def _mla_prep_kernel(cq_ref, ckv_ref, misc_ref, cos_ref, sin_ref, qnw_ref, kvnw_ref, wq_ref, wkv_ref,
                     q_ref, k_ref, v_ref):
    nh = MLA_HEADS
    cos, sin = cos_ref[...], sin_ref[...]
    q = _dot(_rms(cq_ref[...], qnw_ref[...]).astype(BF16), wq_ref[...])
    scale = (MLA_NOPE + MLA_ROPE) ** -0.5 * math.log2(math.e)
    rope0 = nh * MLA_NOPE
    rot0 = rope0 + nh * MLA_ROPE
    for pair in range(nh // 2):
        sl = slice(pair * LANES, (pair + 1) * LANES)
        qr = (q[:, rope0:rot0][:, sl] * cos + q[:, rot0:][:, sl] * sin) * scale
        for hh in range(2):
            h = 2 * pair + hh
            q_ref[:, h * 2 * LANES:h * 2 * LANES + LANES] = (
                q[:, h * MLA_NOPE:(h + 1) * MLA_NOPE] * scale).astype(q_ref.dtype)
            q_ref[:, h * 2 * LANES + LANES:(h + 1) * 2 * LANES] = qr.astype(q_ref.dtype)
    kv = _dot(_rms(ckv_ref[...], kvnw_ref[...]).astype(BF16), wkv_ref[...])
    kpe = misc_ref[:, :LANES] * cos + misc_ref[:, LANES:2 * LANES] * sin
    lane = lax.broadcasted_iota(jnp.int32, (1, LANES), 1)
    slots = [jnp.where(lane < MLA_ROPE, kpe, 0.0), jnp.where(lane < MLA_ROPE, 0.0, kpe)]
    for h in range(nh):
        k_ref[:, h * 2 * LANES:h * 2 * LANES + LANES] = kv[:, h * MLA_NOPE:(h + 1) * MLA_NOPE].astype(k_ref.dtype)
        k_ref[:, h * 2 * LANES + LANES:(h + 1) * 2 * LANES] = slots[h % 2].astype(k_ref.dtype)
    v_ref[...] = kv[:, nh * MLA_NOPE:].T.astype(v_ref.dtype)


def _mla_prep(cq, ckv, misc, cos, sin, qnw, kvnw, wq, wkv, tm=256):
    t = cq.shape[0]
    nh = MLA_HEADS
    row = lambda n: pl.BlockSpec((tm, n), lambda i: (i, 0))
    full = lambda a: pl.BlockSpec(a.shape, lambda i: (0, 0))
    qnw, kvnw = qnw.reshape(1, -1), kvnw.reshape(1, -1)
    return pl.pallas_call(
        _mla_prep_kernel,
        grid=(t // tm,),
        in_specs=[row(cq.shape[1]), row(ckv.shape[1]), row(misc.shape[1]), row(LANES), row(LANES),
                  full(qnw), full(kvnw), full(wq), full(wkv)],
        out_specs=[row(nh * 2 * LANES), row(nh * 2 * LANES), pl.BlockSpec((nh * MLA_V, tm), lambda i: (0, i))],
        out_shape=[jax.ShapeDtypeStruct((t, nh * 2 * LANES), BF16),
                   jax.ShapeDtypeStruct((t, nh * 2 * LANES), BF16),
                   jax.ShapeDtypeStruct((nh * MLA_V, t), BF16)],
        compiler_params=_params("arbitrary"),
        name="mla_prep",
    )(cq, ckv, misc, cos, sin, qnw, kvnw, wq, wkv)


def _flash_kernel(q_ref, k_ref, v_ref, o_ref, *, tq, hb):
    qi = pl.program_id(2)
    qw, vw = 2 * LANES, MLA_V
    qs = [q_ref[:, h * qw:(h + 1) * qw] for h in range(hb)]

    def block(j, carries, mask):
        off = pl.multiple_of(j * tq, tq)
        hs = range(hb)
        s = [_dot_nt(k_ref[pl.ds(off, tq), h * qw:(h + 1) * qw], qs[h]) for h in hs]
        if mask is not None:
            s = [jnp.where(mask, s[h], -jnp.inf) for h in hs]
        m_new = [jnp.maximum(carries[h][0], jnp.max(s[h], axis=0, keepdims=True)) for h in hs]
        alpha = [jnp.exp2(carries[h][0] - m_new[h]) for h in hs]
        p = [jnp.exp2(s[h] - m_new[h]) for h in hs]
        l_new = [alpha[h] * carries[h][1] + jnp.sum(p[h], axis=0, keepdims=True) for h in hs]
        pv = [_dot(v_ref[h * vw:(h + 1) * vw, pl.ds(off, tq)], p[h].astype(BF16)) for h in hs]
        return tuple((m_new[h], l_new[h], alpha[h] * carries[h][2] + pv[h]) for h in hs)

    init = tuple((jnp.full((1, tq), -jnp.inf, F32), jnp.zeros((1, tq), F32), jnp.zeros((vw, tq), F32))
                 for _ in range(hb))
    carries = lax.fori_loop(0, qi, lambda j, c: block(j, c, None), init)
    ki = lax.broadcasted_iota(jnp.int32, (tq, tq), 0)
    qj = lax.broadcasted_iota(jnp.int32, (tq, tq), 1)
    carries = block(qi, carries, ki <= qj)
    for h in range(hb):
        _, l_fin, acc = carries[h]
        o_ref[:, h * vw:(h + 1) * vw] = (acc / l_fin).T.astype(o_ref.dtype)


def _flash(q, k, v, batch, seq, tq=512, hb=4):
    t = q.shape[0]
    tq = min(tq, seq)
    nq = seq // tq
    ng = MLA_HEADS // hb
    return pl.pallas_call(
        functools.partial(_flash_kernel, tq=tq, hb=hb),
        grid=(batch, ng, nq),
        in_specs=[pl.BlockSpec((tq, hb * 2 * LANES), lambda b, g, i: (b * nq + i, g)),
                  pl.BlockSpec((seq, hb * 2 * LANES), lambda b, g, i: (b, g)),
                  pl.BlockSpec((hb * MLA_V, seq), lambda b, g, i: (g, b))],
        out_specs=pl.BlockSpec((tq, hb * MLA_V), lambda b, g, i: (b * nq + i, g)),
        out_shape=jax.ShapeDtypeStruct((t, MLA_HEADS * MLA_V), F32),
        compiler_params=_params("arbitrary", "arbitrary", "arbitrary"),
        name="mla_flash",
    )(q, k, v)


def _finish(h_ref, acc, fnw_ref, o_ref, final):
    hn = h_ref[...] + acc
    o_ref[...] = _rms(hn, fnw_ref[...]) if final else hn


def _even_tail_kernel(a_ref, ys_ref, u_ref, q_ref, z_ref, h_ref, mk_ref, mv_ref, vec_ref, gw_ref, ow_ref,
                      fnw_ref, o_ref, *, final):
    w = RW_WIDTH
    z = z_ref[...]
    ga = (a_ref[...] * _silu(z[:, :w])).astype(BF16)
    yb = _gelu_tanh(ys_ref[...] + vec_ref[0:1, :] * u_ref[...])
    gate = _sigmoid(_dot(yb.astype(BF16), gw_ref[...]) + vec_ref[1:2, :])
    gb = (yb * gate * _silu(z[:, w:2 * w])).astype(BF16)
    gm = (_mem_attend(q_ref[...], mk_ref[0], mv_ref[0]) * _silu(z[:, 2 * w:])).astype(BF16)
    acc = _dot(ga, ow_ref[0:w, :]) + _dot(gb, ow_ref[w:2 * w, :]) + _dot(gm, ow_ref[2 * w:, :])
    _finish(h_ref, acc, fnw_ref, o_ref, final)


def _odd_tail_kernel(c_ref, d_ref, q_ref, z_ref, h_ref, mk_ref, mv_ref, ow_ref, fnw_ref, o_ref, *, final):
    w = D_MODEL
    z = z_ref[...]
    gc = c_ref[...].astype(BF16)
    gd = (d_ref[...] * _silu(z[:, :w])).astype(BF16)
    gm = (_mem_attend(q_ref[...], mk_ref[0], mv_ref[0]) * _silu(z[:, w:])).astype(BF16)
    acc = _dot(gc, ow_ref[0:w, :]) + _dot(gd, ow_ref[w:2 * w, :]) + _dot(gm, ow_ref[2 * w:, :])
    _finish(h_ref, acc, fnw_ref, o_ref, final)


def _tail_call(kernel, rows, consts, h, mk, mv, fnw, batch, seq, final, name, tm=256):
    t, d = h.shape
    nt = seq // tm
    row = lambda a: pl.BlockSpec((tm, a.shape[1]), lambda b, i: (b * nt + i, 0))
    full = lambda a: pl.BlockSpec(a.shape, lambda b, i: (0,) * a.ndim)
    memspec = pl.BlockSpec((1,) + mk.shape[1:], lambda b, i: (b, 0, 0))
    fnw = fnw.reshape(1, d)
    return pl.pallas_call(
        functools.partial(kernel, final=final),
        grid=(batch, nt),
        in_specs=[row(a) for a in rows] + [row(h), memspec, memspec] + [full(a) for a in consts] + [full(fnw)],
        out_specs=row(h),
        out_shape=jax.ShapeDtypeStruct((t, d), F32),
        compiler_params=_params("arbitrary", "arbitrary"),
        name=name,
    )(*rows, h, mk, mv, *consts, fnw)


def _pad_rows(rows, n=SUBLANES):
    width = rows[0].shape[-1]
    return jnp.concatenate([r.reshape(1, width) for r in rows] + [jnp.zeros((n - len(rows), width), F32)], axis=0)


def _even_layer(h, batch, seq, mk, mv, nw, in_w, out_w, fnw, final, mu, w0, w2, a0, a2, k_k, k_a, r_k,
                ln_w, ln_b, s5_ops, s5_d, glu_w, glu_b):
    rw_proj = 3 * RW_WIDTH + 2 * RW_LORA
    ev_width = 2 * RW_WIDTH + MEM_WIDTH
    splits = (rw_proj, RW_WIDTH, MEM_WIDTH, ev_width)
    p, u, q_mem, z = _norm_proj(h, nw, in_w.astype(BF16), splits, (F32, F32, F32, F32))
    rvec, wa = _rwkv_param_pack(w0, w2, a0, a2, k_k, k_a, r_k, ln_w, ln_b)
    a_out = _rwkv(p, mu, rvec, wa, batch, seq)
    ys = _s5(u, *s5_ops, batch, seq)

    vec = _pad_rows([s5_d, glu_b])
    return _tail_call(_even_tail_kernel, [a_out, ys, u, q_mem, z], [vec, glu_w.astype(BF16), out_w.astype(BF16)],
                      h, mk, mv, fnw, batch, seq, final, "even_tail")


def _odd_layer(h, batch, seq, mk, mv, cos, sin, nw, in_w, out_w, fnw, final, conv_w, conv_b, dt_bias, a_log,
               m2_d, m2_norm_w, q_norm_w, wq_up, kv_norm_w, wkv_up):
    width = M2_HEADS * HEAD_DIM
    conv_dim = width + 2 * M2_GROUPS * M2_STATE
    nh = MLA_HEADS
    o_dt = conv_dim
    o_cq = o_dt + M2_HEADS
    o_ckv = o_cq + MLA_Q_RANK
    o_kr = o_ckv + MLA_KV_RANK
    o_qm = o_kr + MLA_ROPE
    o_z = o_qm + MEM_WIDTH
    half = MLA_ROPE // 2
    w_kr = in_w[:, o_kr:o_qm]
    w_kr_sw = jnp.concatenate([w_kr[:, half:], w_kr[:, :half]], axis=1)
    w_misc = jnp.concatenate([w_kr, w_kr, w_kr_sw, w_kr_sw, in_w[:, o_dt:o_cq],
                              jnp.zeros((in_w.shape[0], LANES - M2_HEADS), in_w.dtype)], axis=1)
    w_all = jnp.concatenate([in_w[:, :o_dt], in_w[:, o_cq:o_ckv], in_w[:, o_ckv:o_kr], in_w[:, o_qm:o_z],
                             in_w[:, o_z:], w_misc], axis=1).astype(BF16)
    splits = (conv_dim, MLA_Q_RANK, MLA_KV_RANK, MEM_WIDTH, width, D_MODEL + MEM_WIDTH, 3 * LANES)
    xbc, cq, ckv, q_mem, z_c, z_dm, misc = _norm_proj(h, nw, w_all, splits, (F32,) * 7)

    cw = _pad_rows(list(conv_w) + [conv_b])
    pad16 = lambda x: jnp.concatenate([x, jnp.zeros((LANES - M2_HEADS,), F32)])
    vec = _pad_rows([pad16(dt_bias), pad16(-jnp.exp(a_log))])
    nwd = _pad_rows([m2_norm_w, jnp.repeat(m2_d, HEAD_DIM)])
    c_out = _ssd(xbc, misc, z_c, cw, vec, nwd, batch, seq)

    wq = wq_up.reshape(MLA_Q_RANK, nh, MLA_NOPE + MLA_ROPE)
    wq_r = wq[:, :, MLA_NOPE:]
    wq_sw = jnp.concatenate([wq_r[:, :, half:], wq_r[:, :, :half]], axis=2)
    wq_all = jnp.concatenate([wq[:, :, :MLA_NOPE].reshape(MLA_Q_RANK, -1), wq_r.reshape(MLA_Q_RANK, -1),
                              wq_sw.reshape(MLA_Q_RANK, -1)], axis=1).astype(BF16)
    wkv = wkv_up.reshape(MLA_KV_RANK, nh, MLA_NOPE + MLA_V)
    wkv_all = jnp.concatenate([wkv[:, :, :MLA_NOPE].reshape(MLA_KV_RANK, -1),
                               wkv[:, :, MLA_NOPE:].reshape(MLA_KV_RANK, -1)], axis=1).astype(BF16)
    qc, kc, vv = _mla_prep(cq, ckv, misc, cos, sin, q_norm_w, kv_norm_w, wq_all, wkv_all)
    d_attn = _flash(qc, kc, vv, batch, seq)

    return _tail_call(_odd_tail_kernel, [c_out, d_attn, q_mem, z_dm], [out_w.astype(BF16)],
                      h, mk, mv, fnw, batch, seq, final, "odd_tail")


def _rope_tables(positions):
    inv = 1.0 / (ROPE_THETA ** (jnp.arange(0, MLA_ROPE, 2, dtype=F32) / MLA_ROPE))
    ang = positions.astype(F32).reshape(-1, 1) * inv
    cos, sin = jnp.cos(ang), jnp.sin(ang)
    return jnp.tile(cos, (1, 4)), jnp.tile(jnp.concatenate([-sin, sin], axis=1), (1, 2))


def kernel(x, mem, positions, norm_w, mem_norm_w, final_norm_w, mem_kv_w, ev_in_w, ev_out_w, rw_mu, rw_w0, rw_w2, rw_a0, rw_a2, rw_k_k, rw_k_a, rw_r_k, rw_ln_w, rw_ln_b, s5_lambda_re, s5_lambda_im, s5_b_re, s5_b_im, s5_c_re, s5_c_im, s5_d, s5_log_dt, s5_glu_w, s5_glu_b, od_in_w, od_out_w, m2_conv_w, m2_conv_b, m2_dt_bias, m2_a_log, m2_d, m2_norm_w, mla_q_norm_w, mla_wq_up, mla_kv_norm_w, mla_wkv_up):
    batch, seq, d = x.shape
    depth = norm_w.shape[0]
    mk, mv = _mem_kv(mem, mem_norm_w, mem_kv_w.astype(BF16))
    cos, sin = _rope_tables(positions)
    s5_ops = jax.vmap(functools.partial(_s5_param_pack, nj=seq // S5_SUB))(
        s5_lambda_re, s5_lambda_im, s5_b_re, s5_b_im, s5_c_re, s5_c_im, s5_log_dt)
    ev_in_w, ev_out_w, s5_glu_w = ev_in_w.astype(BF16), ev_out_w.astype(BF16), s5_glu_w.astype(BF16)
    od_out_w = od_out_w.astype(BF16)
    h = x.reshape(batch * seq, d)
    for layer in range(depth):
        i = layer // 2
        final = layer == depth - 1
        if layer % 2 == 0:
            h = _even_layer(h, batch, seq, mk[layer], mv[layer], norm_w[layer], ev_in_w[i], ev_out_w[i],
                            final_norm_w, final, rw_mu[i], rw_w0[i], rw_w2[i], rw_a0[i], rw_a2[i], rw_k_k[i],
                            rw_k_a[i], rw_r_k[i], rw_ln_w[i], rw_ln_b[i], tuple(op[i] for op in s5_ops), s5_d[i],
                            s5_glu_w[i], s5_glu_b[i])
        else:
            h = _odd_layer(h, batch, seq, mk[layer], mv[layer], cos, sin, norm_w[layer], od_in_w[i], od_out_w[i],
                           final_norm_w, final, m2_conv_w[i], m2_conv_b[i], m2_dt_bias[i], m2_a_log[i], m2_d[i],
                           m2_norm_w[i], mla_q_norm_w[i], mla_wq_up[i], mla_kv_norm_w[i], mla_wkv_up[i])
    return h.reshape(batch, seq, d)
```

```python
import functools
import math

import jax
import jax.numpy as jnp
from jax import lax
from jax.experimental import pallas as pl
from jax.experimental.pallas import tpu as pltpu

F32 = jnp.float32
BF16 = jnp.bfloat16
HIGHEST = lax.Precision.HIGHEST

LANES = 128
SUBLANES = 8
VMEM_LIMIT_BYTES = 56 * 1024 * 1024

D_MODEL = 1024
HEAD_DIM = 64
NORM_EPS = 1e-6
RW_WIDTH = 1024
RW_LORA = 64
RW_GN_EPS = 64e-5
RW_CHUNK = 64
S5_GROUP = 16
S5_STATE = 64
S5_SUB = 16
M2_HEADS = 16
M2_GROUPS = 2
M2_STATE = 128
M2_CONV = 4
M2_CHUNK = 128
MLA_HEADS = 8
MLA_NOPE = 128
MLA_ROPE = 64
MLA_V = 128
MLA_Q_RANK = 384
MLA_KV_RANK = 256
ROPE_THETA = 10000.0
MEM_HEADS = 4
MEM_WIDTH = MEM_HEADS * HEAD_DIM


def _dot(a, b, precision=None):
    return jnp.dot(a, b, preferred_element_type=F32, precision=precision)


def _dot_nt(a, b, precision=None):
    return lax.dot_general(a, b, (((1,), (1,)), ((), ())), preferred_element_type=F32,
                           precision=precision)


def _sigmoid(x):
    return 1.0 / (1.0 + jnp.exp(-x))


def _silu(x):
    return x * _sigmoid(x)


def _softplus(x):
    return jnp.maximum(x, 0.0) + jnp.log(1.0 + jnp.exp(-jnp.abs(x)))


def _gelu_tanh(x):
    return 0.5 * x * (1.0 + jnp.tanh(math.sqrt(2.0 / math.pi) * (x + 0.044715 * (x * x * x))))


def _rms(x, w, eps=NORM_EPS):
    ms = jnp.mean(x * x, axis=-1, keepdims=True)
    return x * lax.rsqrt(ms + eps) * w


def _params(*sem):
    return pltpu.CompilerParams(dimension_semantics=sem, vmem_limit_bytes=VMEM_LIMIT_BYTES)


def _norm_proj_kernel(x_ref, nw_ref, w_ref, *out_refs, splits):
    xn = _rms(x_ref[...], nw_ref[...]).astype(BF16)
    off = 0
    for o_ref, n in zip(out_refs, splits):
        o_ref[...] = _dot(xn, w_ref[:, off:off + n]).astype(o_ref.dtype)
        off += n


def _norm_proj(x, nw, w, splits, dtypes, tm=256):
    t, d = x.shape
    n = w.shape[1]
    assert sum(splits) == n and t % tm == 0
    return pl.pallas_call(
        functools.partial(_norm_proj_kernel, splits=splits),
        grid=(t // tm,),
        in_specs=[pl.BlockSpec((tm, d), lambda i: (i, 0)),
                  pl.BlockSpec((1, d), lambda i: (0, 0)),
                  pl.BlockSpec((d, n), lambda i: (0, 0))],
        out_specs=[pl.BlockSpec((tm, s), lambda i: (i, 0)) for s in splits],
        out_shape=[jax.ShapeDtypeStruct((t, s), dt) for s, dt in zip(splits, dtypes)],
        compiler_params=_params("arbitrary"),
        name="norm_proj",
    )(x, nw.reshape(1, d), w)


def _mem_kv_kernel(mem_ref, nw_ref, w_ref, k_ref, v_ref):
    mn = _rms(mem_ref[0], nw_ref[...]).astype(BF16)
    kv = _dot(mn, w_ref[0])
    k_ref[0, 0] = kv[:, :MEM_WIDTH].astype(k_ref.dtype)
    v_ref[0, 0] = kv[:, MEM_WIDTH:].astype(v_ref.dtype)


def _mem_kv(mem, nw, w):
    b, m, d = mem.shape
    depth = w.shape[0]
    shp = jax.ShapeDtypeStruct((depth, b, m, MEM_WIDTH), BF16)
    return pl.pallas_call(
        _mem_kv_kernel,
        grid=(depth, b),
        in_specs=[pl.BlockSpec((1, m, d), lambda l, i: (i, 0, 0)),
                  pl.BlockSpec((1, d), lambda l, i: (0, 0)),
                  pl.BlockSpec((1, d, 2 * MEM_WIDTH), lambda l, i: (l, 0, 0))],
        out_specs=[pl.BlockSpec((1, 1, m, MEM_WIDTH), lambda l, i: (l, i, 0, 0))] * 2,
        out_shape=[shp, shp],
        compiler_params=_params("arbitrary", "arbitrary"),
        name="mem_kv",
    )(mem, nw.reshape(1, d), w)


def _mem_attend(q, k, v):
    lane = lax.broadcasted_iota(jnp.int32, (1, MEM_WIDTH), 1)
    qb = (q * (HEAD_DIM ** -0.5)).astype(BF16)
    hs = range(MEM_HEADS)
    hm = [(lane >> 6) == h for h in hs]
    sc = [_dot_nt(qb, jnp.where(hm[h], k, jnp.zeros_like(k))) for h in hs]
    p = [jnp.exp(sc[h] - jnp.max(sc[h], axis=-1, keepdims=True)) for h in hs]
    inv = [1.0 / jnp.sum(p[h], axis=-1, keepdims=True) for h in hs]
    pv = [_dot(p[h].astype(BF16), jnp.where(hm[h], v, jnp.zeros_like(v))) * inv[h] for h in hs]
    return functools.reduce(lambda a, b: a + b, pv)


def _split3(x):
    hi = x.astype(BF16)
    r1 = x - hi.astype(F32)
    mid = r1.astype(BF16)
    lo = (r1 - mid.astype(F32)).astype(BF16)
    return hi, mid, lo


def _rwkv_kernel(p_ref, mu_ref, vec_ref, wa_ref, out_ref, st_ref, prev_ref, *, L):
    c = pl.program_id(1)
    W = RW_WIDTH

    @pl.when(c == 0)
    def _():
        st_ref[...] = jnp.zeros_like(st_ref)
        prev_ref[...] = jnp.zeros_like(prev_ref)

    x = p_ref[...]
    R = x.shape[0]
    nck = R // L
    row_l = lax.broadcasted_iota(jnp.int32, (R, 1), 0)
    xs = jnp.where(row_l == 0, prev_ref[0:1, :], pltpu.roll(x, 1, axis=0))
    prev_ref[0:1, :] = x[R - 1:R, :]
    xm = x + (xs - x) * mu_ref[...]
    r, k, v, lat = xm[:, :W], xm[:, W:2 * W], xm[:, 2 * W:3 * W], xm[:, 3 * W:]

    w0, a0, k_k, k_a, r_k, ln_w, ln_b = [vec_ref[i:i + 1, :] for i in range(7)]
    lane = lax.broadcasted_iota(jnp.int32, (1, LANES), 1)
    lat_t = jnp.where(lane < RW_LORA, jnp.tanh(lat), lat).astype(BF16)
    la = _dot(lat_t, wa_ref[...])
    log_w = -_softplus(-(w0 + la[:, :W])) - 0.5
    lw = -jnp.exp(log_w)
    iclr = _sigmoid(a0 + la[:, W:])
    kk0 = k * k_k
    kp = k * (1.0 + (iclr - 1.0) * k_a)
    rkr = r * kp * r_k

    il = lax.broadcasted_iota(jnp.int32, (R, 3 * R), 0)
    jl = lax.broadcasted_iota(jnp.int32, (R, 3 * R), 1) & (R - 1)
    same_chunk = (il & -L) == (jl & -L)
    tri3 = jnp.where(same_chunk & (jl <= il), 1.0, 0.0).astype(BF16)
    cum = _dot(tri3, jnp.concatenate(_split3(lw), axis=0))
    cum_ls = [cum[(ck + 1) * L - 1:(ck + 1) * L, :] for ck in range(nck)]
    cum_end = jnp.concatenate([jnp.broadcast_to(cl, (L, W)) for cl in cum_ls], axis=0)
    e_neg = jnp.exp(-cum)
    e_exc = jnp.exp(cum - lw)
    e_inc = jnp.exp(cum)
    dend = jnp.exp(cum_end - cum)
    e_tot = [jnp.exp(cl) for cl in cum_ls]

    i2 = lax.broadcasted_iota(jnp.int32, (LANES, LANES), 0)
    j2 = lax.broadcasted_iota(jnp.int32, (LANES, LANES), 1)
    same_head = (i2 >> 6) == (j2 >> 6)
    head_ones = jnp.where(same_head, 1.0, 0.0).astype(BF16)
    strict = same_head & (j2 < i2)
    incl = same_head & (j2 <= i2)
    eye = jnp.where(i2 == j2, 1.0, 0.0)
    levels = [((i2 & -(2 * s)) == (j2 & -(2 * s))) & ((i2 & s) != 0) & ((j2 & s) == 0)
              for s in (1, 2, 4, 8, 16, 32)]
    m0 = lane < HEAD_DIM
    stack = lambda t: jnp.concatenate([jnp.where(m0, t, 0.0), jnp.where(m0, 0.0, t)], axis=0)

    npair = W // LANES
    units = [(slice(ck * L, (ck + 1) * L), slice(p * LANES, (p + 1) * LANES))
             for ck in range(nck) for p in range(npair)]
    each = lambda f: [f(i) for i in range(len(units))]
    at = lambda arr, i: arr[units[i][0], units[i][1]]
    sums = each(lambda i: _dot(jnp.concatenate([at(kk0, i) * at(kk0, i), at(rkr, i)], axis=0).astype(BF16),
                               head_ones))
    kk = each(lambda i: at(kk0, i) * lax.rsqrt(jnp.maximum(sums[i][:L], 1e-24)))
    bvec = each(lambda i: kk[i] * at(iclr, i))
    ar = each(lambda i: jnp.concatenate([stack(-kk[i] * at(e_exc, i)), stack(at(r, i) * at(e_inc, i))],
                                        axis=0).astype(BF16))
    b2 = each(lambda i: stack(bvec[i] * at(e_neg, i)).astype(BF16))
    k2 = each(lambda i: stack(at(kp, i) * at(e_neg, i)).astype(BF16))
    v2f = each(lambda i: stack(at(v, i)))
    v2 = each(lambda i: v2f[i].astype(BF16))
    bk_end = each(lambda i: jnp.concatenate([stack(bvec[i] * at(dend, i)), stack(at(kp, i) * at(dend, i))],
                                            axis=0).astype(BF16))
    sb = each(lambda i: _dot_nt(ar[i], b2[i]))
    sk = each(lambda i: _dot_nt(ar[i], k2[i]))
    aab = each(lambda i: jnp.where(strict, sb[i][:LANES], 0.0))

    inv = each(lambda i: eye + jnp.where(levels[0], aab[i], 0.0))
    for lvl in levels[1:]:
        invb = each(lambda i: inv[i].astype(BF16))
        t1 = each(lambda i: _dot(invb[i], jnp.where(lvl, aab[i], 0.0).astype(BF16)).astype(BF16))
        inv = each(lambda i: inv[i] + _dot(t1[i], invb[i]))
    aak_v = each(lambda i: _dot(jnp.where(strict, sk[i][:LANES], 0.0).astype(BF16), v2[i]))
    rk_v = each(lambda i: _dot(jnp.where(incl, sk[i][LANES:], 0.0).astype(BF16), v2[i]))

    st = [st_ref[p] for p in range(npair)]
    y = [None] * len(units)
    for ck in range(nck):
        ids = [ck * npair + p for p in range(npair)]
        sta = [_dot_nt(ar[i], st[p].astype(BF16)) for p, i in enumerate(ids)]
        u2 = [_dot(inv[i].astype(BF16), (sta[p][:LANES] + aak_v[i]).astype(BF16)) for p, i in enumerate(ids)]
        y2 = [sta[p][LANES:] + _dot(jnp.where(incl, sb[i][LANES:], 0.0).astype(BF16), u2[p].astype(BF16)) + rk_v[i]
              for p, i in enumerate(ids)]
        for p, i in enumerate(ids):
            y[i] = y2[p][:L, :] + y2[p][L:, :]
            uv_t = jnp.concatenate([u2[p].T, v2f[i].T], axis=1).astype(BF16)
            st[p] = st[p] * e_tot[ck][:, units[i][1]] + _dot(uv_t, bk_end[i])
    for p in range(npair):
        st_ref[p] = st[p]

    def head_sum(t):
        s0 = jnp.sum(jnp.where(m0, t, 0.0), axis=1, keepdims=True)
        s1 = jnp.sum(jnp.where(m0, 0.0, t), axis=1, keepdims=True)
        return jnp.where(m0, s0, s1)

    mean = each(lambda i: head_sum(y[i]) * (1.0 / HEAD_DIM))
    dcen = each(lambda i: y[i] - mean[i])
    var = each(lambda i: head_sum(dcen[i] * dcen[i]) * (1.0 / HEAD_DIM))
    for i, (rs, ls_) in enumerate(units):
        out_ref[rs, ls_] = (dcen[i] * lax.rsqrt(var[i] + RW_GN_EPS) * ln_w[:, ls_] + ln_b[:, ls_]
                            + sums[i][L:] * at(v, i))


def _rwkv(p, mu, vec, wa, batch, seq, nck=2):
    L = RW_CHUNK * (nck if seq % (nck * RW_CHUNK) == 0 else 1)
    t, pw = p.shape
    nc = seq // L
    return pl.pallas_call(
        functools.partial(_rwkv_kernel, L=RW_CHUNK),
        grid=(batch, nc),
        in_specs=[pl.BlockSpec((L, pw), lambda b, c: (b * nc + c, 0)),
                  pl.BlockSpec((1, pw), lambda b, c: (0, 0)),
                  pl.BlockSpec(vec.shape, lambda b, c: (0, 0)),
                  pl.BlockSpec(wa.shape, lambda b, c: (0, 0))],
        out_specs=pl.BlockSpec((L, RW_WIDTH), lambda b, c: (b * nc + c, 0)),
        out_shape=jax.ShapeDtypeStruct((t, RW_WIDTH), F32),
        scratch_shapes=[pltpu.VMEM((RW_WIDTH // LANES, LANES, LANES), F32), pltpu.VMEM((SUBLANES, pw), F32)],
        compiler_params=_params("arbitrary", "arbitrary"),
        name="rwkv7",
    )(p, mu.reshape(1, pw), vec, wa)


def _rwkv_param_pack(w0, w2, a0, a2, k_k, k_a, r_k, ln_w, ln_b):
    vec = _pad_rows([w0, a0, k_k, k_a, r_k.reshape(-1), ln_w, ln_b])
    z = jnp.zeros_like(w2)
    wa = jnp.concatenate([jnp.concatenate([w2, z], axis=1), jnp.concatenate([z, a2], axis=1)], axis=0)
    return vec, wa.astype(BF16)


def _s5_kernel(u_ref, tz_ref, gm_ref, cc_ref, pw_ref, y_ref, tzd_ref, gmd_ref, ccd_ref, *, nj):
    ls, pch, n = S5_SUB, S5_GROUP, S5_STATE
    gs = LANES // pch

    @pl.when(pl.program_id(1) == 0)
    def _():
        tzd_ref[...] = jnp.zeros_like(tzd_ref)
        gmd_ref[...] = jnp.zeros_like(gmd_ref)
        ccd_ref[...] = jnp.zeros_like(ccd_ref)
        for g in range(gs):
            for a in range(ls):
                r0 = a * LANES + g * pch
                for b in range(a, ls):
                    c0 = b * LANES + g * pch
                    tzd_ref[r0:r0 + pch, c0:c0 + pch] = tz_ref[g, a * pch:(a + 1) * pch, b * pch:(b + 1) * pch]
                for c in range(2):
                    gmd_ref[r0:r0 + pch, (c * gs + g) * n:(c * gs + g + 1) * n] = (
                        gm_ref[g, a * pch:(a + 1) * pch, c * n:(c + 1) * n])
                    ccd_ref[(c * gs + g) * n:(c * gs + g + 1) * n, r0:r0 + pch] = (
                        cc_ref[g, c * n:(c + 1) * n, a * pch:(a + 1) * pch])

    rows = u_ref.shape[0] // ls
    xcat = jnp.concatenate([u_ref[pl.ds(l, rows, stride=ls), :] for l in range(ls)], axis=1).astype(BF16)
    otile = 2 * LANES
    y = jnp.concatenate([_dot(xcat[:, :(m + 1) * otile], tzd_ref[:(m + 1) * otile, m * otile:(m + 1) * otile])
                         for m in range(ls * LANES // otile)], axis=1)
    x = _dot(xcat, gmd_ref[...])
    half = x.shape[1] // 2
    jrow = lax.broadcasted_iota(jnp.int32, (rows, 1), 0) & (nj - 1)
    pw = pw_ref[0]
    d, lvl = 1, 0
    while d < nj:
        xs = jnp.where(jrow >= d, pltpu.roll(x, d, axis=0), 0.0)
        xsw = jnp.concatenate([xs[:, half:], xs[:, :half]], axis=1)
        x = x + xs * pw[2 * lvl:2 * lvl + 1, :] + xsw * pw[2 * lvl + 1:2 * lvl + 2, :]
        d, lvl = 2 * d, lvl + 1
    h_in = jnp.where(jrow >= 1, pltpu.roll(x, 1, axis=0), 0.0)
    y = y + _dot(h_in.astype(BF16), ccd_ref[...])
    for l in range(ls):
        y_ref[pl.ds(l, rows, stride=ls), :] = y[:, l * LANES:(l + 1) * LANES]


def _s5(u, tz, gm, cc, pw, batch, seq, nb=2):
    t, width = u.shape
    nslab = width // LANES
    gs = LANES // S5_GROUP
    nj = seq // S5_SUB
    sub_w = S5_SUB * LANES
    st_w = 2 * gs * S5_STATE
    pwd = pw.reshape(nslab, gs, pw.shape[1], 2, S5_STATE).transpose(0, 2, 3, 1, 4).reshape(nslab, pw.shape[1], st_w)
    grp_spec = lambda a: pl.BlockSpec((gs,) + a.shape[1:], lambda s, b: (s, 0, 0))
    nb = nb if batch % nb == 0 else 1
    assert nj & (nj - 1) == 0
    return pl.pallas_call(
        functools.partial(_s5_kernel, nj=nj),
        grid=(nslab, batch // nb),
        in_specs=[pl.BlockSpec((nb * seq, LANES), lambda s, b: (b, s)),
                  grp_spec(tz), grp_spec(gm), grp_spec(cc),
                  pl.BlockSpec((1,) + pwd.shape[1:], lambda s, b: (s, 0, 0))],
        out_specs=pl.BlockSpec((nb * seq, LANES), lambda s, b: (b, s)),
        out_shape=jax.ShapeDtypeStruct((t, width), F32),
        scratch_shapes=[pltpu.VMEM((sub_w, sub_w), BF16), pltpu.VMEM((sub_w, st_w), BF16),
                        pltpu.VMEM((st_w, sub_w), BF16)],
        compiler_params=_params("arbitrary", "arbitrary"),
        name="s5",
    )(u, tz, gm, cc, pwd)


def _s5_param_pack(lam_re, lam_im, b_re, b_im, c_re, c_im, log_dt, nj):
    ls = S5_SUB
    dt = jnp.exp(log_dt)[:, None]
    zr, zi = lam_re * dt, lam_im * dt

    def powers(steps):
        st = steps[:, None, None]
        mag = jnp.exp(st * zr)
        return mag * jnp.cos(st * zi), mag * jnp.sin(st * zi)

    ab_re, ab_im = powers(jnp.ones((1,), F32))
    ab_re, ab_im = ab_re[0], ab_im[0]
    den = lam_re * lam_re + lam_im * lam_im
    nr, ni = ab_re - 1.0, ab_im
    f_re = (nr * lam_re + ni * lam_im) / den
    f_im = (ni * lam_re - nr * lam_im) / den
    bb_re = f_re[..., None] * b_re - f_im[..., None] * b_im
    bb_im = f_re[..., None] * b_im + f_im[..., None] * b_re
    pr, pi = powers(jnp.arange(0, ls + 1, dtype=F32))
    cl_re = c_re[None] * pr[:, :, None, :] - c_im[None] * pi[:, :, None, :]
    cl_im = c_re[None] * pi[:, :, None, :] + c_im[None] * pr[:, :, None, :]
    kern = jnp.einsum('dgpn,gnq->dgpq', jnp.concatenate([cl_re[:ls], -cl_im[:ls]], axis=3),
                      jnp.concatenate([bb_re, bb_im], axis=1), precision=lax.Precision.HIGH)
    lag = jnp.arange(ls)[None, :] - jnp.arange(ls)[:, None]
    tz = jnp.where((lag >= 0)[:, :, None, None, None], kern[jnp.clip(lag, 0, ls - 1)], 0.0)
    g = lam_re.shape[0]
    pch = S5_GROUP
    tz = tz.transpose(2, 0, 4, 1, 3).reshape(g, ls * pch, ls * pch)
    rr, ri = pr[ls - 1::-1][:ls], pi[ls - 1::-1][:ls]
    gm_re = rr[..., None] * bb_re[None] - ri[..., None] * bb_im[None]
    gm_im = rr[..., None] * bb_im[None] + ri[..., None] * bb_re[None]
    gm = jnp.concatenate([gm_re, gm_im], axis=2)
    gm = gm.transpose(1, 0, 3, 2).reshape(g, ls * pch, 2 * S5_STATE)
    cc = jnp.concatenate([cl_re[1:], -cl_im[1:]], axis=3)
    cc = cc.transpose(1, 3, 0, 2).reshape(g, 2 * S5_STATE, ls * pch)
    lv = []
    d = 1
    while d < nj:
        lv.append(float(ls * d))
        d *= 2
    qr, qi = powers(jnp.asarray(lv, F32))
    pw = jnp.stack([jnp.concatenate([qr, qr], axis=-1), jnp.concatenate([-qi, qi], axis=-1)], axis=1)
    pw = pw.reshape(2 * len(lv), g, 2 * S5_STATE).transpose(1, 0, 2)
    pw = jnp.concatenate([pw, jnp.zeros((g, 16 - 2 * len(lv), 2 * S5_STATE), F32)], axis=1)
    return tz.astype(BF16), gm.astype(BF16), cc.astype(BF16), pw


def _ssd_kernel(xbc_ref, misc_ref, zc_ref, cw_ref, vec_ref, nw_ref, out_ref,
                st_ref, halo_ref, buf_ref, *, L):
    c = pl.program_id(1)
    width = M2_HEADS * HEAD_DIM
    gs = M2_STATE

    @pl.when(c == 0)
    def _():
        st_ref[...] = jnp.zeros_like(st_ref)
        halo_ref[...] = jnp.zeros_like(halo_ref)

    xbc = xbc_ref[...]
    buf_ref[0:SUBLANES, :] = halo_ref[...]
    buf_ref[SUBLANES:SUBLANES + L, :] = xbc
    halo_ref[...] = xbc[L - SUBLANES:L, :]
    cw = cw_ref[...]
    nshift = M2_CONV - 1
    si = lax.broadcasted_iota(jnp.int32, (nshift * L, L + SUBLANES), 0)
    sj = lax.broadcasted_iota(jnp.int32, (nshift * L, L + SUBLANES), 1)
    tap_i = (si >= L).astype(jnp.int32) + (si >= 2 * L).astype(jnp.int32)
    shift = jnp.where(sj == (si - tap_i * L) + tap_i + (SUBLANES - nshift), 1.0, 0.0).astype(BF16)
    zsh = _dot(shift, buf_ref[...].astype(BF16))
    acc = cw[M2_CONV:M2_CONV + 1, :] + cw[nshift:M2_CONV, :] * xbc
    for tap in range(nshift):
        acc = acc + cw[tap:tap + 1, :] * zsh[tap * L:(tap + 1) * L, :]
    xc = _silu(acc)
    xs = xc[:, :width]

    vec = vec_ref[...]
    dt = _softplus(misc_ref[:, 2 * LANES:3 * LANES] + vec[0:1, :])
    a_dt = dt * vec[1:2, :]
    il = lax.broadcasted_iota(jnp.int32, (L, 3 * L), 0)
    jl = lax.broadcasted_iota(jnp.int32, (L, 3 * L), 1) & (L - 1)
    tri3 = jnp.where(jl <= il, 1.0, 0.0).astype(BF16)
    cum = _dot(tri3, jnp.concatenate(_split3(a_dt), axis=0))
    cum_t = cum.T

    eh = lax.broadcasted_iota(jnp.int32, (3 * LANES, width), 0) & (LANES - 1)
    ec = lax.broadcasted_iota(jnp.int32, (3 * LANES, width), 1)
    expand3 = jnp.where((ec >> 6) == eh, 1.0, 0.0).astype(BF16)
    dt_x = _dot(jnp.concatenate(_split3(dt), axis=1), expand3)
    cum_x = _dot(jnp.concatenate(_split3(cum), axis=1), expand3)
    cum_lx = cum_x[L - 1:L, :]
    xdt = xs * dt_x
    xdd = (xdt * jnp.exp(cum_lx - cum_x)).astype(BF16)
    xdt_b = xdt.astype(BF16)
    ecum = jnp.exp(cum_x)
    e_tot = jnp.exp(cum_lx)

    tril = (lax.broadcasted_iota(jnp.int32, (L, L), 1) <= lax.broadcasted_iota(jnp.int32, (L, L), 0))
    lane = lax.broadcasted_iota(jnp.int32, (1, LANES), 1)
    m0 = lane < HEAD_DIM
    hpg = M2_HEADS // M2_GROUPS
    gw = hpg * HEAD_DIM
    groups = range(M2_GROUPS)
    heads = range(M2_HEADS)
    bm_f = [xc[:, width + g * gs:width + (g + 1) * gs] for g in groups]
    bm = [b.astype(BF16) for b in bm_f]
    cm = [xc[:, width + (M2_GROUPS + g) * gs:width + (M2_GROUPS + g + 1) * gs].astype(BF16) for g in groups]
    cbm = [jnp.where(tril, _dot_nt(cm[g], bm[g]), 0.0) for g in groups]
    st = [st_ref[:, g * gw:(g + 1) * gw] for g in groups]
    y_off = [_dot(cm[g], st[g].astype(BF16)) for g in groups]
    for g in groups:
        st_ref[:, g * gw:(g + 1) * gw] = (st[g] * e_tot[:, g * gw:(g + 1) * gw]
                                          + _dot(bm_f[g].T.astype(BF16), xdd[:, g * gw:(g + 1) * gw]))
    mh = [(cbm[h // hpg] * jnp.exp(jnp.minimum(cum[:, h:h + 1] - cum_t[h:h + 1, :], 0.0))).astype(BF16)
          for h in heads]
    zero = jnp.zeros((), BF16)
    yd = [_dot(mh[h], jnp.where(m0 if h % 2 == 0 else jnp.logical_not(m0),
                                xdt_b[:, (h // 2) * LANES:(h // 2 + 1) * LANES], zero)) for h in heads]
    y_diag = jnp.concatenate([yd[2 * p] + yd[2 * p + 1] for p in range(M2_HEADS // 2)], axis=1)
    y = y_diag + jnp.concatenate(y_off, axis=1) * ecum + xs * nw_ref[1:2, :]
    y = y * _silu(zc_ref[...])
    gw = width // M2_GROUPS
    outs = []
    for g in range(M2_GROUPS):
        yg = y[:, g * gw:(g + 1) * gw]
        outs.append(_rms(yg, nw_ref[0:1, g * gw:(g + 1) * gw]))
    out_ref[...] = jnp.concatenate(outs, axis=1)


def _ssd(xbc, misc, z, cw, vec, nw, batch, seq):
    L = M2_CHUNK
    t, cd = xbc.shape
    nc = seq // L
    width = M2_HEADS * HEAD_DIM
    return pl.pallas_call(
        functools.partial(_ssd_kernel, L=L),
        grid=(batch, nc),
        in_specs=[pl.BlockSpec((L, cd), lambda b, c: (b * nc + c, 0)),
                  pl.BlockSpec((L, misc.shape[1]), lambda b, c: (b * nc + c, 0)),
                  pl.BlockSpec((L, width), lambda b, c: (b * nc + c, 0)),
                  pl.BlockSpec((SUBLANES, cd), lambda b, c: (0, 0)),
                  pl.BlockSpec((SUBLANES, LANES), lambda b, c: (0, 0)),
                  pl.BlockSpec((SUBLANES, width), lambda b, c: (0, 0))],
        out_specs=pl.BlockSpec((L, width), lambda b, c: (b * nc + c, 0)),
        out_shape=jax.ShapeDtypeStruct((t, width), F32),
        scratch_shapes=[pltpu.VMEM((M2_STATE, width), F32), pltpu.VMEM((SUBLANES, cd), F32),
                        pltpu.VMEM((L + SUBLANES, cd), F32)],
        compiler_params=_params("arbitrary", "arbitrary"),
        name="ssd",
    )(xbc, misc, z, cw, vec, nw)


def _mla_prep_kernel(cq_ref, ckv_ref, misc_ref, cos_ref, sin_ref, qnw_ref, kvnw_ref, wq_ref, wkv_ref,
                     q_ref, k_ref, v_ref):
    nh = MLA_HEADS
    cos, sin = cos_ref[...], sin_ref[...]
    q = _dot(_rms(cq_ref[...], qnw_ref[...]).astype(BF16), wq_ref[...])
    scale = (MLA_NOPE + MLA_ROPE) ** -0.5 * math.log2(math.e)
    rope0 = nh * MLA_NOPE
    rot0 = rope0 + nh * MLA_ROPE
    for pair in range(nh // 2):
        sl = slice(pair * LANES, (pair + 1) * LANES)
        qr = (q[:, rope0:rot0][:, sl] * cos + q[:, rot0:][:, sl] * sin) * scale
        for hh in range(2):
            h = 2 * pair + hh
            q_ref[:, h * 2 * LANES:h * 2 * LANES + LANES] = (
                q[:, h * MLA_NOPE:(h + 1) * MLA_NOPE] * scale).astype(q_ref.dtype)
            q_ref[:, h * 2 * LANES + LANES:(h + 1) * 2 * LANES] = qr.astype(q_ref.dtype)
    kv = _dot(_rms(ckv_ref[...], kvnw_ref[...]).astype(BF16), wkv_ref[...])
    kpe = misc_ref[:, :LANES] * cos + misc_ref[:, LANES:2 * LANES] * sin
    lane = lax.broadcasted_iota(jnp.int32, (1, LANES), 1)
    slots = [jnp.where(lane < MLA_ROPE, kpe, 0.0), jnp.where(lane < MLA_ROPE, 0.0, kpe)]
    for h in range(nh):
        k_ref[:, h * 2 * LANES:h * 2 * LANES + LANES] = kv[:, h * MLA_NOPE:(h + 1) * MLA_NOPE].astype(k_ref.dtype)
        k_ref[:, h * 2 * LANES + LANES:(h + 1) * 2 * LANES] = slots[h % 2].astype(k_ref.dtype)
    v_ref[...] = kv[:, nh * MLA_NOPE:].T.astype(v_ref.dtype)


def _mla_prep(cq, ckv, misc, cos, sin, qnw, kvnw, wq, wkv, tm=256):
    t = cq.shape[0]
    nh = MLA_HEADS
    row = lambda n: pl.BlockSpec((tm, n), lambda i: (i, 0))
    full = lambda a: pl.BlockSpec(a.shape, lambda i: (0, 0))
    qnw, kvnw = qnw.reshape(1, -1), kvnw.reshape(1, -1)
    return pl.pallas_call(
        _mla_prep_kernel,
        grid=(t // tm,),
        in_specs=[row(cq.shape[1]), row(ckv.shape[1]), row(misc.shape[1]), row(LANES), row(LANES),
                  full(qnw), full(kvnw), full(wq), full(wkv)],
        out_specs=[row(nh * 2 * LANES), row(nh * 2 * LANES), pl.BlockSpec((nh * MLA_V, tm), lambda i: (0, i))],
        out_shape=[jax.ShapeDtypeStruct((t, nh * 2 * LANES), BF16),
                   jax.ShapeDtypeStruct((t, nh * 2 * LANES), BF16),
                   jax.ShapeDtypeStruct((nh * MLA_V, t), BF16)],
        compiler_params=_params("arbitrary"),
        name="mla_prep",
    )(cq, ckv, misc, cos, sin, qnw, kvnw, wq, wkv)


def _flash_kernel(q_ref, k_ref, v_ref, o_ref, *, tq, hb):
    qi = pl.program_id(2)
    qw, vw = 2 * LANES, MLA_V
    qs = [q_ref[:, h * qw:(h + 1) * qw] for h in range(hb)]

    def block(j, carries, mask):
        off = pl.multiple_of(j * tq, tq)
        hs = range(hb)
        s = [_dot_nt(k_ref[pl.ds(off, tq), h * qw:(h + 1) * qw], qs[h]) for h in hs]
        if mask is not None:
            s = [jnp.where(mask, s[h], -jnp.inf) for h in hs]
        m_new = [jnp.maximum(carries[h][0], jnp.max(s[h], axis=0, keepdims=True)) for h in hs]
        alpha = [jnp.exp2(carries[h][0] - m_new[h]) for h in hs]
        p = [jnp.exp2(s[h] - m_new[h]) for h in hs]
        l_new = [alpha[h] * carries[h][1] + jnp.sum(p[h], axis=0, keepdims=True) for h in hs]
        pv = [_dot(v_ref[h * vw:(h + 1) * vw, pl.ds(off, tq)], p[h].astype(BF16)) for h in hs]
        return tuple((m_new[h], l_new[h], alpha[h] * carries[h][2] + pv[h]) for h in hs)

    init = tuple((jnp.full((1, tq), -jnp.inf, F32), jnp.zeros((1, tq), F32), jnp.zeros((vw, tq), F32))
                 for _ in range(hb))
    carries = lax.fori_loop(0, qi, lambda j, c: block(j, c, None), init)
    ki = lax.broadcasted_iota(jnp.int32, (tq, tq), 0)
    qj = lax.broadcasted_iota(jnp.int32, (tq, tq), 1)
    carries = block(qi, carries, ki <= qj)
    for h in range(hb):
        _, l_fin, acc = carries[h]
        o_ref[:, h * vw:(h + 1) * vw] = (acc / l_fin).T.astype(o_ref.dtype)


def _flash(q, k, v, batch, seq, tq=512, hb=4):
    t = q.shape[0]
    tq = min(tq, seq)
    nq = seq // tq
    ng = MLA_HEADS // hb
    return pl.pallas_call(
        functools.partial(_flash_kernel, tq=tq, hb=hb),
        grid=(batch, ng, nq),
        in_specs=[pl.BlockSpec((tq, hb * 2 * LANES), lambda b, g, i: (b * nq + i, g)),
                  pl.BlockSpec((seq, hb * 2 * LANES), lambda b, g, i: (b, g)),
                  pl.BlockSpec((hb * MLA_V, seq), lambda b, g, i: (g, b))],
        out_specs=pl.BlockSpec((tq, hb * MLA_V), lambda b, g, i: (b * nq + i, g)),
        out_shape=jax.ShapeDtypeStruct((t, MLA_HEADS * MLA_V), F32),
        compiler_params=_params("arbitrary", "arbitrary", "arbitrary"),
        name="mla_flash",
    )(q, k, v)


def _finish(h_ref, acc, fnw_ref, o_ref, final):
    hn = h_ref[...] + acc
    o_ref[...] = _rms(hn, fnw_ref[...]) if final else hn


def _even_tail_kernel(a_ref, ys_ref, u_ref, q_ref, z_ref, h_ref, mk_ref, mv_ref, vec_ref, gw_ref, ow_ref,
                      fnw_ref, o_ref, *, final):
    w = RW_WIDTH
    z = z_ref[...]
    ga = (a_ref[...] * _silu(z[:, :w])).astype(BF16)
    yb = _gelu_tanh(ys_ref[...] + vec_ref[0:1, :] * u_ref[...])
    gate = _sigmoid(_dot(yb.astype(BF16), gw_ref[...]) + vec_ref[1:2, :])
    gb = (yb * gate * _silu(z[:, w:2 * w])).astype(BF16)
    gm = (_mem_attend(q_ref[...], mk_ref[0], mv_ref[0]) * _silu(z[:, 2 * w:])).astype(BF16)
    acc = _dot(ga, ow_ref[0:w, :]) + _dot(gb, ow_ref[w:2 * w, :]) + _dot(gm, ow_ref[2 * w:, :])
    _finish(h_ref, acc, fnw_ref, o_ref, final)


def _odd_tail_kernel(c_ref, d_ref, q_ref, z_ref, h_ref, mk_ref, mv_ref, ow_ref, fnw_ref, o_ref, *, final):
    w = D_MODEL
    z = z_ref[...]
    gc = c_ref[...].astype(BF16)
    gd = (d_ref[...] * _silu(z[:, :w])).astype(BF16)
    gm = (_mem_attend(q_ref[...], mk_ref[0], mv_ref[0]) * _silu(z[:, w:])).astype(BF16)
    acc = _dot(gc, ow_ref[0:w, :]) + _dot(gd, ow_ref[w:2 * w, :]) + _dot(gm, ow_ref[2 * w:, :])
    _finish(h_ref, acc, fnw_ref, o_ref, final)


def _tail_call(kernel, rows, consts, h, mk, mv, fnw, batch, seq, final, name, tm=256):
    t, d = h.shape
    nt = seq // tm
    row = lambda a: pl.BlockSpec((tm, a.shape[1]), lambda b, i: (b * nt + i, 0))
    full = lambda a: pl.BlockSpec(a.shape, lambda b, i: (0,) * a.ndim)
    memspec = pl.BlockSpec((1,) + mk.shape[1:], lambda b, i: (b, 0, 0))
    fnw = fnw.reshape(1, d)
    return pl.pallas_call(
        functools.partial(kernel, final=final),
        grid=(batch, nt),
        in_specs=[row(a) for a in rows] + [row(h), memspec, memspec] + [full(a) for a in consts] + [full(fnw)],
        out_specs=row(h),
        out_shape=jax.ShapeDtypeStruct((t, d), F32),
        compiler_params=_params("arbitrary", "arbitrary"),
        name=name,
    )(*rows, h, mk, mv, *consts, fnw)


def _pad_rows(rows, n=SUBLANES):
    width = rows[0].shape[-1]
    return jnp.concatenate([r.reshape(1, width) for r in rows] + [jnp.zeros((n - len(rows), width), F32)], axis=0)


def _even_layer(h, batch, seq, mk, mv, nw, in_w, out_w, fnw, final, mu, w0, w2, a0, a2, k_k, k_a, r_k,
                ln_w, ln_b, s5_ops, s5_d, glu_w, glu_b):
    rw_proj = 3 * RW_WIDTH + 2 * RW_LORA
    ev_width = 2 * RW_WIDTH + MEM_WIDTH
    splits = (rw_proj, RW_WIDTH, MEM_WIDTH, ev_width)
    p, u, q_mem, z = _norm_proj(h, nw, in_w.astype(BF16), splits, (F32, F32, F32, F32))
    rvec, wa = _rwkv_param_pack(w0, w2, a0, a2, k_k, k_a, r_k, ln_w, ln_b)
    a_out = _rwkv(p, mu, rvec, wa, batch, seq)
    ys = _s5(u, *s5_ops, batch, seq)

    vec = _pad_rows([s5_d, glu_b])
    return _tail_call(_even_tail_kernel, [a_out, ys, u, q_mem, z], [vec, glu_w.astype(BF16), out_w.astype(BF16)],
                      h, mk, mv, fnw, batch, seq, final, "even_tail")


def _odd_layer(h, batch, seq, mk, mv, cos, sin, nw, in_w, out_w, fnw, final, conv_w, conv_b, dt_bias, a_log,
               m2_d, m2_norm_w, q_norm_w, wq_up, kv_norm_w, wkv_up):
    width = M2_HEADS * HEAD_DIM
    conv_dim = width + 2 * M2_GROUPS * M2_STATE
    nh = MLA_HEADS
    o_dt = conv_dim
    o_cq = o_dt + M2_HEADS
    o_ckv = o_cq + MLA_Q_RANK
    o_kr = o_ckv + MLA_KV_RANK
    o_qm = o_kr + MLA_ROPE
    o_z = o_qm + MEM_WIDTH
    half = MLA_ROPE // 2
    w_kr = in_w[:, o_kr:o_qm]
    w_kr_sw = jnp.concatenate([w_kr[:, half:], w_kr[:, :half]], axis=1)
    w_misc = jnp.concatenate([w_kr, w_kr, w_kr_sw, w_kr_sw, in_w[:, o_dt:o_cq],
                              jnp.zeros((in_w.shape[0], LANES - M2_HEADS), in_w.dtype)], axis=1)
    w_all = jnp.concatenate([in_w[:, :o_dt], in_w[:, o_cq:o_ckv], in_w[:, o_ckv:o_kr], in_w[:, o_qm:o_z],
                             in_w[:, o_z:], w_misc], axis=1).astype(BF16)
    splits = (conv_dim, MLA_Q_RANK, MLA_KV_RANK, MEM_WIDTH, width, D_MODEL + MEM_WIDTH, 3 * LANES)
    xbc, cq, ckv, q_mem, z_c, z_dm, misc = _norm_proj(h, nw, w_all, splits, (F32,) * 7)

    cw = _pad_rows(list(conv_w) + [conv_b])
    pad16 = lambda x: jnp.concatenate([x, jnp.zeros((LANES - M2_HEADS,), F32)])
    vec = _pad_rows([pad16(dt_bias), pad16(-jnp.exp(a_log))])
    nwd = _pad_rows([m2_norm_w, jnp.repeat(m2_d, HEAD_DIM)])
    c_out = _ssd(xbc, misc, z_c, cw, vec, nwd, batch, seq)

    wq = wq_up.reshape(MLA_Q_RANK, nh, MLA_NOPE + MLA_ROPE)
    wq_r = wq[:, :, MLA_NOPE:]
    wq_sw = jnp.concatenate([wq_r[:, :, half:], wq_r[:, :, :half]], axis=2)
    wq_all = jnp.concatenate([wq[:, :, :MLA_NOPE].reshape(MLA_Q_RANK, -1), wq_r.reshape(MLA_Q_RANK, -1),
                              wq_sw.reshape(MLA_Q_RANK, -1)], axis=1).astype(BF16)
    wkv = wkv_up.reshape(MLA_KV_RANK, nh, MLA_NOPE + MLA_V)
    wkv_all = jnp.concatenate([wkv[:, :, :MLA_NOPE].reshape(MLA_KV_RANK, -1),
                               wkv[:, :, MLA_NOPE:].reshape(MLA_KV_RANK, -1)], axis=1).astype(BF16)
    qc, kc, vv = _mla_prep(cq, ckv, misc, cos, sin, q_norm_w, kv_norm_w, wq_all, wkv_all)
    d_attn = _flash(qc, kc, vv, batch, seq)

    return _tail_call(_odd_tail_kernel, [c_out, d_attn, q_mem, z_dm], [out_w.astype(BF16)],
                      h, mk, mv, fnw, batch, seq, final, "odd_tail")


def _rope_tables(positions):
    inv = 1.0 / (ROPE_THETA ** (jnp.arange(0, MLA_ROPE, 2, dtype=F32) / MLA_ROPE))
    ang = positions.astype(F32).reshape(-1, 1) * inv
    cos, sin = jnp.cos(ang), jnp.sin(ang)
    return jnp.tile(cos, (1, 4)), jnp.tile(jnp.concatenate([-sin, sin], axis=1), (1, 2))


def kernel(x, mem, positions, norm_w, mem_norm_w, final_norm_w, mem_kv_w, ev_in_w, ev_out_w, rw_mu, rw_w0, rw_w2, rw_a0, rw_a2, rw_k_k, rw_k_a, rw_r_k, rw_ln_w, rw_ln_b, s5_lambda_re, s5_lambda_im, s5_b_re, s5_b_im, s5_c_re, s5_c_im, s5_d, s5_log_dt, s5_glu_w, s5_glu_b, od_in_w, od_out_w, m2_conv_w, m2_conv_b, m2_dt_bias, m2_a_log, m2_d, m2_norm_w, mla_q_norm_w, mla_wq_up, mla_kv_norm_w, mla_wkv_up):
    batch, seq, d = x.shape
    depth = norm_w.shape[0]
    mk, mv = _mem_kv(mem, mem_norm_w, mem_kv_w.astype(BF16))
    cos, sin = _rope_tables(positions)
    s5_ops = jax.vmap(functools.partial(_s5_param_pack, nj=seq // S5_SUB))(
        s5_lambda_re, s5_lambda_im, s5_b_re, s5_b_im, s5_c_re, s5_c_im, s5_log_dt)
    ev_in_w, ev_out_w, s5_glu_w = ev_in_w.astype(BF16), ev_out_w.astype(BF16), s5_glu_w.astype(BF16)
    od_out_w = od_out_w.astype(BF16)
    h = x.reshape(batch * seq, d)
    for layer in range(depth):
        i = layer // 2
        final = layer == depth - 1
        if layer % 2 == 0:
            h = _even_layer(h, batch, seq, mk[layer], mv[layer], norm_w[layer], ev_in_w[i], ev_out_w[i],
                            final_norm_w, final, rw_mu[i], rw_w0[i], rw_w2[i], rw_a0[i], rw_a2[i], rw_k_k[i],
                            rw_k_a[i], rw_r_k[i], rw_ln_w[i], rw_ln_b[i], tuple(op[i] for op in s5_ops), s5_d[i],
                            s5_glu_w[i], s5_glu_b[i])
        else:
            h = _odd_layer(h, batch, seq, mk[layer], mv[layer], cos, sin, norm_w[layer], od_in_w[i], od_out_w[i],
                           final_norm_w, final, m2_conv_w[i], m2_conv_b[i], m2_dt_bias[i], m2_a_log[i], m2_d[i],
                           m2_norm_w[i], mla_q_norm_w[i], mla_wq_up[i], mla_kv_norm_w[i], mla_wkv_up[i])
    return h.reshape(batch, seq, d)
```
